```python
import jax, jax.numpy as jnp
from jax import lax
import numpy as np

D_MODEL = 1024
BATCH = 8
SEQ = 4096
DEPTH = 1

DA_HEADS = 4
DA_QK_DIM = 64
DA_V_DIM = 2 * DA_QK_DIM
DA_ROT_DIM = DA_QK_DIM // 4
ROPE_THETA = 500000.0
RET_HEADS = 4
RET_QK_DIM = 128
RET_V_DIM = 256
RET_THETA_BASE = 10000.0
D_FF = 4 * D_MODEL
N_BRANCH = 2
Q_BLOCK = 128
RET_CHUNK = 128
NORM_EPS = 1e-6
SUBLN_EPS = 1e-5
GN_EPS = 1e-5
MASK_VALUE = -1e30

DA_QK_W = DA_HEADS * 2 * DA_QK_DIM
DA_V_W = DA_HEADS * DA_V_DIM
RET_QK_W = RET_HEADS * RET_QK_DIM
RET_V_W = RET_HEADS * RET_V_DIM
GATE_W = N_BRANCH * D_MODEL
SPLIT_SIZES = (DA_QK_W, DA_QK_W, DA_V_W, RET_QK_W, RET_QK_W, RET_V_W, RET_V_W, GATE_W)
IN_W = sum(SPLIT_SIZES)
SPLIT_POINTS = tuple(int(v) for v in np.cumsum(SPLIT_SIZES)[:-1])

kernel_name = "hybrid_diffattn_retention_block"


def rms_norm(t, w, eps):
    tf = t.astype(jnp.float32)
    tf = tf * lax.rsqrt(jnp.mean(tf * tf, axis=-1, keepdims=True) + eps)
    return tf * w.astype(jnp.float32)


def rotate_half(t, cos, sin):
    t1, t2 = jnp.split(t, 2, axis=-1)
    return jnp.concatenate([t1 * cos - t2 * sin, t1 * sin + t2 * cos], axis=-1)


def diff_attention(q, k, v, lam):
    B, S, H, _, d = q.shape
    nb = S // Q_BLOCK
    qb = q.reshape(B, nb, Q_BLOCK, H, 2, d).transpose(1, 0, 2, 3, 4, 5)
    k_pos = jnp.arange(S)
    scale = DA_QK_DIM ** -0.5

    def block(args):
        q_blk, i = args
        q_pos = i * Q_BLOCK + jnp.arange(Q_BLOCK)
        s = jnp.einsum('bqhcd,bkhcd->bhcqk', q_blk, k) * scale
        causal = k_pos[None, :] <= q_pos[:, None]
        s = jnp.where(causal, s, MASK_VALUE)
        p = jax.nn.softmax(s, axis=-1)
        a = p[:, :, 0] - lam * p[:, :, 1]
        return jnp.einsum('bhqk,bkhe->bqhe', a, v)

    o = lax.map(block, (qb, jnp.arange(nb)))
    return o.transpose(1, 0, 2, 3, 4).reshape(B, S, H, v.shape[-1])


def retention_chunkwise(q, k, v, log_gamma):
    B, S, H, dk = q.shape
    dv = v.shape[-1]
    C = RET_CHUNK
    nc = S // C

    def to_chunks(t):
        return t.reshape(B, nc, C, H, t.shape[-1]).transpose(1, 0, 3, 2, 4)

    idx = jnp.arange(C, dtype=jnp.float32)
    lg = log_gamma[:, None, None]
    rel = idx[:, None] - idx[None, :]
    decay = jnp.where(rel >= 0, jnp.exp(lg * jnp.maximum(rel, 0.0)), 0.0)
    xi = jnp.exp(log_gamma[:, None] * (idx + 1.0))[None, :, :, None]
    zeta = jnp.exp(log_gamma[:, None] * (C - 1.0 - idx))[None, :, :, None]
    gamma_c = jnp.exp(log_gamma * C)[None, :, None, None]

    def step(state, chunk):
        qc, kc, vc = chunk
        scores = jnp.einsum('bhnd,bhmd->bhnm', qc, kc) * decay[None]
        inner = jnp.einsum('bhnm,bhme->bhne', scores, vc)
        cross = jnp.einsum('bhnd,bhde->bhne', qc, state) * xi
        new_state = gamma_c * state + jnp.einsum('bhmd,bhme->bhde', kc * zeta, vc)
        return new_state, inner + cross

    state0 = jnp.zeros((B, H, dk, dv), jnp.float32)
    _, out = lax.scan(step, state0, (to_chunks(q), to_chunks(k), to_chunks(v)))
    return out.transpose(1, 0, 3, 2, 4).reshape(B, S, H, dv)


def hybrid_layer(x, cos_da, sin_da, cos_rt, sin_rt, log_gamma, layer_idx,
                 norm1_w, w_in, q_norm_w, k_norm_w, lambda_q1, lambda_k1, lambda_q2, lambda_k2,
                 da_subln_w, w_da_branch, ret_gn_w, w_ret_branch, w_out,
                 norm2_w, w_mlp_in, w_mlp_out):
    B, S, _ = x.shape
    dt = x.dtype
    h = rms_norm(x, norm1_w, NORM_EPS).astype(dt)
    proj = h @ w_in
    dq, dk, dv, rq, rk, rv, rg, gl = jnp.split(proj, SPLIT_POINTS, axis=-1)

    lambda_init = 0.8 - 0.6 * float(np.exp(-0.3 * layer_idx))
    qa = rms_norm(dq.reshape(B, S, DA_HEADS, 2, DA_QK_DIM), q_norm_w, NORM_EPS)
    ka = rms_norm(dk.reshape(B, S, DA_HEADS, 2, DA_QK_DIM), k_norm_w, NORM_EPS)
    qa = jnp.concatenate([rotate_half(qa[..., :DA_ROT_DIM], cos_da, sin_da), qa[..., DA_ROT_DIM:]], axis=-1)
    ka = jnp.concatenate([rotate_half(ka[..., :DA_ROT_DIM], cos_da, sin_da), ka[..., DA_ROT_DIM:]], axis=-1)
    va = dv.reshape(B, S, DA_HEADS, DA_V_DIM).astype(jnp.float32)
    lam = (jnp.exp(jnp.sum(lambda_q1.astype(jnp.float32) * lambda_k1.astype(jnp.float32)))
           - jnp.exp(jnp.sum(lambda_q2.astype(jnp.float32) * lambda_k2.astype(jnp.float32)))
           + lambda_init)
    oa = diff_attention(qa, ka, va, lam)
    oa = rms_norm(oa, da_subln_w, SUBLN_EPS) * (1.0 - lambda_init)
    y_a = oa.reshape(B, S, DA_V_W).astype(dt) @ w_da_branch

    qr = rotate_half(rq.reshape(B, S, RET_HEADS, RET_QK_DIM).astype(jnp.float32), cos_rt, sin_rt)
    kr = rotate_half(rk.reshape(B, S, RET_HEADS, RET_QK_DIM).astype(jnp.float32), cos_rt, sin_rt)
    kr = kr * (RET_QK_DIM ** -0.5)
    vr = rv.reshape(B, S, RET_HEADS, RET_V_DIM).astype(jnp.float32)
    orr = retention_chunkwise(qr, kr, vr, log_gamma)
    mu = jnp.mean(orr, axis=-1, keepdims=True)
    var = jnp.mean(jnp.square(orr - mu), axis=-1, keepdims=True)
    orr = (orr - mu) * lax.rsqrt(var + GN_EPS) * ret_gn_w.astype(jnp.float32).reshape(RET_HEADS, RET_V_DIM)
    orr = jax.nn.silu(rg.astype(jnp.float32)) * orr.reshape(B, S, RET_V_W)
    y_r = orr.astype(dt) @ w_ret_branch

    g_a, g_r = jnp.split(gl, N_BRANCH, axis=-1)
    merged = jax.nn.sigmoid(g_a) * y_a + jax.nn.sigmoid(g_r) * y_r
    x = x + (merged.astype(dt) @ w_out)

    h2 = rms_norm(x, norm2_w, NORM_EPS).astype(dt)
    x = x + (jnp.square(jax.nn.relu(h2 @ w_mlp_in)) @ w_mlp_out).astype(dt)
    return x


def setup_inputs(seed: int = 0) -> dict:
    key = jax.random.key(seed)
    ks = jax.random.split(key, 20)
    f32 = jnp.float32

    def nrm(k, shape, scale):
        return jax.random.normal(k, shape, f32) * scale

    def gain(k, shape):
        return 1.0 + 0.02 * jax.random.normal(k, shape, f32)

    x = jax.random.normal(ks[0], (BATCH, SEQ, D_MODEL), f32)
    offsets = jax.random.randint(ks[1], (BATCH, 1), 0, 4096, dtype=jnp.int32)
    positions = (offsets + jnp.arange(SEQ, dtype=jnp.int32)[None, :]).astype(jnp.int32)
    return {
        "x": x,
        "positions": positions,
        "norm1_w": gain(ks[2], (DEPTH, D_MODEL)),
        "w_in": nrm(ks[3], (DEPTH, D_MODEL, IN_W), D_MODEL ** -0.5),
        "q_norm_w": gain(ks[4], (DEPTH, DA_QK_DIM)),
        "k_norm_w": gain(ks[5], (DEPTH, DA_QK_DIM)),
        "lambda_q1": nrm(ks[6], (DEPTH, DA_QK_DIM), 0.1),
        "lambda_k1": nrm(ks[7], (DEPTH, DA_QK_DIM), 0.1),
        "lambda_q2": nrm(ks[8], (DEPTH, DA_QK_DIM), 0.1),
        "lambda_k2": nrm(ks[9], (DEPTH, DA_QK_DIM), 0.1),
        "da_subln_w": gain(ks[10], (DEPTH, DA_V_DIM)),
        "w_da_branch": nrm(ks[11], (DEPTH, DA_V_W, D_MODEL), DA_V_W ** -0.5),
        "ret_gn_w": gain(ks[12], (DEPTH, RET_V_W)),
        "w_ret_branch": nrm(ks[13], (DEPTH, RET_V_W, D_MODEL), RET_V_W ** -0.5),
        "w_out": nrm(ks[14], (DEPTH, D_MODEL, D_MODEL), D_MODEL ** -0.5),
        "norm2_w": gain(ks[15], (DEPTH, D_MODEL)),
        "w_mlp_in": nrm(ks[16], (DEPTH, D_MODEL, D_FF), D_MODEL ** -0.5),
        "w_mlp_out": nrm(ks[17], (DEPTH, D_FF, D_MODEL), D_FF ** -0.5),
    }


def reference(x, positions, norm1_w, w_in, q_norm_w, k_norm_w, lambda_q1, lambda_k1,
              lambda_q2, lambda_k2, da_subln_w, w_da_branch, ret_gn_w, w_ret_branch,
              w_out, norm2_w, w_mlp_in, w_mlp_out):
    pos = positions.astype(jnp.float32)[..., None]
    inv_freq_da = ROPE_THETA ** (-jnp.arange(0, DA_ROT_DIM, 2, dtype=jnp.float32) / DA_ROT_DIM)
    ang_da = (pos * inv_freq_da)[:, :, None, None, :]
    cos_da, sin_da = jnp.cos(ang_da), jnp.sin(ang_da)
    inv_freq_rt = 1.0 / (RET_THETA_BASE ** jnp.linspace(0.0, 1.0, RET_QK_DIM // 2, dtype=jnp.float32))
    ang_rt = (pos * inv_freq_rt)[:, :, None, :]
    cos_rt, sin_rt = jnp.cos(ang_rt), jnp.sin(ang_rt)
    log_gamma = jnp.log1p(-jnp.exp2(-5.0 - jnp.arange(RET_HEADS, dtype=jnp.float32)))

    for l in range(DEPTH):
        x = hybrid_layer(x, cos_da, sin_da, cos_rt, sin_rt, log_gamma, l,
                         norm1_w[l], w_in[l], q_norm_w[l], k_norm_w[l],
                         lambda_q1[l], lambda_k1[l], lambda_q2[l], lambda_k2[l],
                         da_subln_w[l], w_da_branch[l], ret_gn_w[l], w_ret_branch[l],
                         w_out[l], norm2_w[l], w_mlp_in[l], w_mlp_out[l])
    return x
```

```python
import functools
import math

import jax
import jax.numpy as jnp
import numpy as np
from jax import lax
from jax.experimental import pallas as pl
from jax.experimental.pallas import tpu as pltpu

D_MODEL = 1024
DA_HEADS = 4
DA_QK_DIM = 64
DA_V_DIM = 2 * DA_QK_DIM
DA_ROT_DIM = DA_QK_DIM // 4
ROPE_THETA = 500000.0
RET_HEADS = 4
RET_QK_DIM = 128
RET_V_DIM = 256
RET_THETA_BASE = 10000.0
D_FF = 4 * D_MODEL
NORM_EPS = 1e-6
SUBLN_EPS = 1e-5
GN_EPS = 1e-5
MASK_VALUE = -1e30
LAMBDA_INIT = 0.8 - 0.6 * math.exp(-0.3 * 0)

DA_QK_W = DA_HEADS * 2 * DA_QK_DIM
DA_V_W = DA_HEADS * DA_V_DIM
RET_QK_W = RET_HEADS * RET_QK_DIM
RET_V_W = RET_HEADS * RET_V_DIM
GATE_W = 2 * D_MODEL
SECTION_WIDTHS = (DA_QK_W, DA_QK_W, DA_V_W, RET_QK_W, RET_QK_W, RET_V_W, RET_V_W, GATE_W)
SECTION_STARTS = tuple(int(v) for v in np.cumsum((0,) + SECTION_WIDTHS[:-1]))
IN_W = sum(SECTION_WIDTHS)

V7X_LANES = 128
V7X_VMEM_LIMIT_BYTES = 56 * 1024 * 1024

LOG2_E = math.log2(math.e)

PROJ_ROWS = 512
ATTN_Q_ROWS = 256
ATTN_KV_ROWS = 512
RET_CHUNK = 256
MLP_ROWS = 512
MLP_FF_CHUNK = 1024


def _bf16(t):
    return t.astype(jnp.bfloat16)


def _dot(a, b):
    return jnp.dot(a, b, preferred_element_type=jnp.float32)


def _dot_nt(a, b):
    return lax.dot_general(a, b, (((1,), (1,)), ((), ())), preferred_element_type=jnp.float32)


def _dot_tn(a, b):
    return lax.dot_general(a, b, (((0,), (0,)), ((), ())), preferred_element_type=jnp.float32)


def _in_proj_kernel(x_ref, n1w_ref, w_ref, qkw_ref, dac_ref, das_ref, rtc_ref, rts_ref, ones_ref,
                    dq_ref, dk_ref, dv_ref, rq_ref, rk_ref, rv_ref, rg_ref, gl_ref):
    x = x_ref[...]
    h = x * lax.rsqrt(jnp.mean(x * x, axis=-1, keepdims=True) + NORM_EPS) * n1w_ref[...]
    h = _bf16(h)

    def section(idx):
        lo = SECTION_STARTS[idx]
        return _dot(h, w_ref[:, lo:lo + SECTION_WIDTHS[idx]])

    rows = x.shape[0]
    reps = DA_QK_W // V7X_LANES
    cos_da = jnp.concatenate([dac_ref[...]] * reps, axis=1)
    sin_da = jnp.concatenate([das_ref[...]] * reps, axis=1)
    lane = lax.broadcasted_iota(jnp.int32, (rows, DA_QK_W), 1)
    first_half = (lane % DA_QK_DIM) < (DA_ROT_DIM // 2)
    ones_bd = ones_ref[...]
    for idx, out_ref in ((0, dq_ref), (1, dk_ref)):
        t = section(idx)
        ms = _dot(_bf16(t * t), ones_bd) * (1.0 / DA_QK_DIM)
        t = t * lax.rsqrt(ms + NORM_EPS) * qkw_ref[idx:idx + 1, :]
        partner = jnp.where(first_half,
                            pltpu.roll(t, DA_QK_W - DA_ROT_DIM // 2, axis=1),
                            pltpu.roll(t, DA_ROT_DIM // 2, axis=1))
        out_ref[...] = _bf16(t * cos_da + partner * sin_da)

    dv_ref[...] = _bf16(section(2))

    cos_rt = rtc_ref[...]
    sin_rt = rts_ref[...]
    for idx, out_ref, scale in ((3, rq_ref, 1.0), (4, rk_ref, RET_QK_DIM ** -0.5)):
        t = section(idx)
        for hh in range(RET_HEADS):
            th = t[:, hh * RET_QK_DIM:(hh + 1) * RET_QK_DIM]
            rot = th * cos_rt + pltpu.roll(th, RET_QK_DIM // 2, axis=1) * sin_rt
            if scale != 1.0:
                rot = rot * scale
            out_ref[:, hh * RET_QK_DIM:(hh + 1) * RET_QK_DIM] = _bf16(rot)

    rv_ref[...] = _bf16(section(5))
    rg_ref[...] = _bf16(section(6))
    gl_ref[...] = _bf16(section(7))


def _in_proj(x2, n1w, w_in, qkw, dac, das, rtc, rts, ones_bd):
    T = x2.shape[0]
    rows = min(PROJ_ROWS, T)
    assert T % rows == 0
    row_spec = lambda width: pl.BlockSpec((rows, width), lambda i: (i, 0))
    const_spec = lambda shape: pl.BlockSpec(shape, lambda i: (0, 0))
    out_widths = SECTION_WIDTHS
    return pl.pallas_call(
        _in_proj_kernel,
        grid=(T // rows,),
        in_specs=[
            row_spec(D_MODEL),
            const_spec((1, D_MODEL)),
            pl.BlockSpec((D_MODEL, IN_W), lambda i: (0, 0), pipeline_mode=pl.Buffered(1)),
            const_spec((2, DA_QK_W)),
            row_spec(V7X_LANES), row_spec(V7X_LANES), row_spec(V7X_LANES), row_spec(V7X_LANES),
            const_spec((DA_QK_W, DA_QK_W)),
        ],
        out_specs=[row_spec(w) for w in out_widths],
        out_shape=[jax.ShapeDtypeStruct((T, w), jnp.bfloat16) for w in out_widths],
        compiler_params=pltpu.CompilerParams(
            dimension_semantics=("parallel",), vmem_limit_bytes=V7X_VMEM_LIMIT_BYTES),
        name="in_proj",
    )(x2, n1w, w_in, qkw, dac, das, rtc, rts, ones_bd)


def _diff_attn_kernel(q_ref, k_ref, v_ref, lam_ref, sw_ref, o_ref, *, tq, tk):
    qi = pl.program_id(2)
    q = q_ref[...]
    lane = lax.broadcasted_iota(jnp.int32, q.shape, 1)
    zero = jnp.zeros_like(q)
    qs = jnp.concatenate([jnp.where(lane < DA_QK_DIM, q, zero),
                          jnp.where(lane >= DA_QK_DIM, q, zero)], axis=0)
    nrow = 2 * tq

    def step(j, carry, masked):
        m, l, acc = carry
        start = pl.multiple_of(j * tk, tk)
        kj = k_ref[pl.ds(start, tk), :]
        vj = v_ref[pl.ds(start, tk), :]
        s = _dot_nt(qs, kj)
        if masked:
            row = lax.broadcasted_iota(jnp.int32, (nrow, tk), 0)
            qpos = qi * tq + jnp.where(row >= tq, row - tq, row)
            kpos = start + lax.broadcasted_iota(jnp.int32, (nrow, tk), 1)
            s = jnp.where(kpos <= qpos, s, MASK_VALUE)
        m_new = jnp.maximum(m, jnp.max(s, axis=-1, keepdims=True))
        alpha = jnp.exp2(m - m_new)
        p = jnp.exp2(s - m_new)
        l = alpha * l + jnp.sum(p, axis=-1, keepdims=True)
        acc = alpha * acc + _dot(_bf16(p), vj)
        return m_new, l, acc

    init = (jnp.full((nrow, 1), MASK_VALUE, jnp.float32),
            jnp.zeros((nrow, 1), jnp.float32),
            jnp.zeros((nrow, DA_V_DIM), jnp.float32))
    q_end = (qi + 1) * tq
    n_full = (qi * tq) // tk
    carry = lax.fori_loop(0, n_full, lambda j, c: step(j, c, False), init)
    n_all = (q_end + tk - 1) // tk
    carry = lax.fori_loop(n_full, n_all, lambda j, c: step(j, c, True), carry)
    m, l, acc = carry
    o = acc / l
    o = o[:tq] - lam_ref[...] * o[tq:]
    o = o * lax.rsqrt(jnp.mean(o * o, axis=-1, keepdims=True) + SUBLN_EPS)
    o_ref[...] = _bf16(o * sw_ref[...])


def _diff_attn(dq, dk, dv, lam, subln_w, B, S):
    tq = min(ATTN_Q_ROWS, S)
    tk = min(ATTN_KV_ROWS, S)
    assert S % tq == 0 and S % tk == 0 and tk % tq == 0
    nq = S // tq
    kernel = functools.partial(_diff_attn_kernel, tq=tq, tk=tk)
    return pl.pallas_call(
        kernel,
        grid=(B, DA_HEADS, nq),
        in_specs=[
            pl.BlockSpec((tq, V7X_LANES), lambda b, h, i: (b * nq + i, h)),
            pl.BlockSpec((S, V7X_LANES), lambda b, h, i: (b, h)),
            pl.BlockSpec((S, V7X_LANES), lambda b, h, i: (b, h)),
            pl.BlockSpec((1, 1), lambda b, h, i: (0, 0)),
            pl.BlockSpec((1, DA_V_DIM), lambda b, h, i: (0, 0)),
        ],
        out_specs=pl.BlockSpec((tq, DA_V_DIM), lambda b, h, i: (b * nq + i, h)),
        out_shape=jax.ShapeDtypeStruct((B * S, DA_V_W), jnp.bfloat16),
        compiler_params=pltpu.CompilerParams(
            dimension_semantics=("parallel", "parallel", "arbitrary"),
            vmem_limit_bytes=V7X_VMEM_LIMIT_BYTES),
        name="diff_attn",
    )(dq, dk, dv, lam, subln_w)


def _retention_kernel(q_ref, k_ref, v_ref, g_ref, lg_ref, gnw_ref, o_ref, *, chunk, n_chunks):
    lg = lg_ref[0, :, 0:1]
    row = lax.broadcasted_iota(jnp.int32, (chunk, chunk), 0)
    col = lax.broadcasted_iota(jnp.int32, (chunk, chunk), 1)
    rel = (row - col).astype(jnp.float32)
    decay = jnp.where(rel >= 0, jnp.exp(lg * jnp.maximum(rel, 0.0)), 0.0)
    idx = lax.broadcasted_iota(jnp.int32, (chunk, 1), 0).astype(jnp.float32)
    xi = jnp.exp(lg * (idx + 1.0))
    zeta = jnp.exp(lg * (chunk - 1.0 - idx))
    gamma_c = jnp.exp(lg * float(chunk))
    gnw = gnw_ref[...]

    def body(c, state):
        start = pl.multiple_of(c * chunk, chunk)
        qc = q_ref[pl.ds(start, chunk), :]
        kc = k_ref[pl.ds(start, chunk), :]
        vc = v_ref[pl.ds(start, chunk), :]
        scores = _dot_nt(qc, kc) * decay
        inner = _dot(_bf16(scores), vc)
        cross = _dot(qc, _bf16(state)) * xi
        kz = _bf16(kc.astype(jnp.float32) * zeta)
        new_state = gamma_c * state + _dot_tn(kz, vc)
        o = inner + cross
        mu = jnp.mean(o, axis=-1, keepdims=True)
        d = o - mu
        var = jnp.mean(d * d, axis=-1, keepdims=True)
        on = d * lax.rsqrt(var + GN_EPS) * gnw
        g = g_ref[pl.ds(start, chunk), :].astype(jnp.float32)
        o_ref[pl.ds(start, chunk), :] = _bf16(g * jax.nn.sigmoid(g) * on)
        return new_state

    lax.fori_loop(0, n_chunks, body, jnp.zeros((RET_QK_DIM, RET_V_DIM), jnp.float32))


def _retention(rq, rk, rv, rg, lg_rows, gn_w, B, S):
    chunk = min(RET_CHUNK, S)
    assert S % chunk == 0
    kernel = functools.partial(_retention_kernel, chunk=chunk, n_chunks=S // chunk)
    qk_spec = pl.BlockSpec((S, RET_QK_DIM), lambda b, h: (b, h))
    v_spec = pl.BlockSpec((S, RET_V_DIM), lambda b, h: (b, h))
    return pl.pallas_call(
        kernel,
        grid=(B, RET_HEADS),
        in_specs=[
            qk_spec, qk_spec, v_spec, v_spec,
            pl.BlockSpec((1, 1, V7X_LANES), lambda b, h: (h, 0, 0)),
            pl.BlockSpec((1, RET_V_DIM), lambda b, h: (0, h)),
        ],
        out_specs=v_spec,
        out_shape=jax.ShapeDtypeStruct((B * S, RET_V_W), jnp.bfloat16),
        compiler_params=pltpu.CompilerParams(
            dimension_semantics=("parallel", "parallel"),
            vmem_limit_bytes=V7X_VMEM_LIMIT_BYTES),
        name="retention",
    )(rq, rk, rv, rg, lg_rows, gn_w)


def _out_mlp_kernel(x_ref, oa_ref, or_ref, gl_ref, wa_ref, wr_ref, wo_ref, n2w_ref, w1_ref, w2_ref,
                    o_ref):
    ya = _dot(oa_ref[...], wa_ref[...])
    yr = _dot(or_ref[...], wr_ref[...])
    ga = jax.nn.sigmoid(gl_ref[:, :D_MODEL].astype(jnp.float32))
    gr = jax.nn.sigmoid(gl_ref[:, D_MODEL:].astype(jnp.float32))
    merged = _bf16(ga * ya + gr * yr)
    x1 = x_ref[...] + _dot(merged, wo_ref[...])
    h2 = x1 * lax.rsqrt(jnp.mean(x1 * x1, axis=-1, keepdims=True) + NORM_EPS) * n2w_ref[...]
    h2 = _bf16(h2)
    acc = x1
    for c in range(D_FF // MLP_FF_CHUNK):
        lo = c * MLP_FF_CHUNK
        hid = jnp.maximum(_dot(h2, w1_ref[:, lo:lo + MLP_FF_CHUNK]), 0.0)
        acc = acc + _dot(_bf16(hid * hid), w2_ref[lo:lo + MLP_FF_CHUNK, :])
    o_ref[...] = acc


def _out_mlp(x2, oa, orr, gl, wa, wr, wo, n2w, w1, w2):
    T = x2.shape[0]
    rows = min(MLP_ROWS, T)
    assert T % rows == 0
    row_spec = lambda width: pl.BlockSpec((rows, width), lambda i: (i, 0))
    weight_spec = lambda shape: pl.BlockSpec(shape, lambda i: (0, 0), pipeline_mode=pl.Buffered(1))
    return pl.pallas_call(
        _out_mlp_kernel,
        grid=(T // rows,),
        in_specs=[
            row_spec(D_MODEL), row_spec(DA_V_W), row_spec(RET_V_W), row_spec(GATE_W),
            weight_spec((DA_V_W, D_MODEL)), weight_spec((RET_V_W, D_MODEL)),
            weight_spec((D_MODEL, D_MODEL)),
            pl.BlockSpec((1, D_MODEL), lambda i: (0, 0)),
            weight_spec((D_MODEL, D_FF)), weight_spec((D_FF, D_MODEL)),
        ],
        out_specs=row_spec(D_MODEL),
        out_shape=jax.ShapeDtypeStruct((T, D_MODEL), jnp.float32),
        compiler_params=pltpu.CompilerParams(
            dimension_semantics=("parallel",), vmem_limit_bytes=V7X_VMEM_LIMIT_BYTES),
        name="out_mlp",
    )(x2, oa, orr, gl, wa, wr, wo, n2w, w1, w2)


def _rotary_tables(positions):
    pos = positions.reshape(-1, 1).astype(jnp.float32)
    half = DA_ROT_DIM // 2
    inv_da = ROPE_THETA ** (-jnp.arange(0, DA_ROT_DIM, 2, dtype=jnp.float32) / DA_ROT_DIM)
    ang = pos * inv_da
    cos, sin = jnp.cos(ang), jnp.sin(ang)
    pad1 = jnp.ones((pos.shape[0], DA_QK_DIM - DA_ROT_DIM), jnp.float32)
    pad0 = jnp.zeros_like(pad1)
    dac = jnp.concatenate([cos, cos, pad1] * (V7X_LANES // DA_QK_DIM), axis=1)
    das = jnp.concatenate([-sin, sin, pad0] * (V7X_LANES // DA_QK_DIM), axis=1)
    inv_rt = 1.0 / (RET_THETA_BASE ** jnp.linspace(0.0, 1.0, RET_QK_DIM // 2, dtype=jnp.float32))
    ang = pos * inv_rt
    cos, sin = jnp.cos(ang), jnp.sin(ang)
    rtc = jnp.concatenate([cos, cos], axis=1)
    rts = jnp.concatenate([-sin, sin], axis=1)
    del half
    return dac, das, rtc, rts


def kernel(x, positions, norm1_w, w_in, q_norm_w, k_norm_w, lambda_q1, lambda_k1, lambda_q2,
           lambda_k2, da_subln_w, w_da_branch, ret_gn_w, w_ret_branch, w_out, norm2_w, w_mlp_in,
           w_mlp_out):
    B, S, _ = x.shape
    T = B * S
    f32 = jnp.float32
    x2 = x.reshape(T, D_MODEL)

    dac, das, rtc, rts = _rotary_tables(positions)
    q_scale = (DA_QK_DIM ** -0.5) * LOG2_E
    qkw = jnp.stack([jnp.tile(q_norm_w[0].astype(f32) * q_scale, DA_QK_W // DA_QK_DIM),
                     jnp.tile(k_norm_w[0].astype(f32), DA_QK_W // DA_QK_DIM)])
    blk = np.arange(DA_QK_W) // DA_QK_DIM
    ones_bd = jnp.asarray(blk[:, None] == blk[None, :], jnp.bfloat16)

    dq, dk, dv, rq, rk, rv, rg, gl = _in_proj(
        x2, norm1_w[0].astype(f32).reshape(1, D_MODEL), _bf16(w_in[0]), qkw, dac, das, rtc, rts,
        ones_bd)

    lam = (jnp.exp(jnp.sum(lambda_q1[0].astype(f32) * lambda_k1[0].astype(f32)))
           - jnp.exp(jnp.sum(lambda_q2[0].astype(f32) * lambda_k2[0].astype(f32)))
           + LAMBDA_INIT).reshape(1, 1)
    subln = (da_subln_w[0].astype(f32) * (1.0 - LAMBDA_INIT)).reshape(1, DA_V_DIM)
    oa = _diff_attn(dq, dk, dv, lam, subln, B, S)

    log_gamma = np.log1p(-np.exp2(-5.0 - np.arange(RET_HEADS, dtype=np.float64)))
    lg_rows = jnp.asarray(np.broadcast_to(log_gamma[:, None, None], (RET_HEADS, 1, V7X_LANES)), f32)
    orr = _retention(rq, rk, rv, rg, lg_rows, ret_gn_w[0].astype(f32).reshape(1, RET_V_W), B, S)

    out = _out_mlp(x2, oa, orr, gl, _bf16(w_da_branch[0]), _bf16(w_ret_branch[0]), _bf16(w_out[0]),
                   norm2_w[0].astype(f32).reshape(1, D_MODEL), _bf16(w_mlp_in[0]),
                   _bf16(w_mlp_out[0]))
    return out.reshape(B, S, D_MODEL)
```

```python
import functools
import math

import jax
import jax.numpy as jnp
import numpy as np
from jax import lax
from jax.experimental import pallas as pl
from jax.experimental.pallas import tpu as pltpu

D_MODEL = 1024
DA_HEADS = 4
DA_QK_DIM = 64
DA_V_DIM = 2 * DA_QK_DIM
DA_ROT_DIM = DA_QK_DIM // 4
ROPE_THETA = 500000.0
RET_HEADS = 4
RET_QK_DIM = 128
RET_V_DIM = 256
RET_THETA_BASE = 10000.0
D_FF = 4 * D_MODEL
NORM_EPS = 1e-6
SUBLN_EPS = 1e-5
GN_EPS = 1e-5
MASK_VALUE = -1e30
LAMBDA_INIT = 0.8 - 0.6 * math.exp(-0.3 * 0)

DA_QK_W = DA_HEADS * 2 * DA_QK_DIM
DA_V_W = DA_HEADS * DA_V_DIM
RET_QK_W = RET_HEADS * RET_QK_DIM
RET_V_W = RET_HEADS * RET_V_DIM
GATE_W = 2 * D_MODEL
SECTION_WIDTHS = (DA_QK_W, DA_QK_W, DA_V_W, RET_QK_W, RET_QK_W, RET_V_W, RET_V_W, GATE_W)
SECTION_STARTS = tuple(int(v) for v in np.cumsum((0,) + SECTION_WIDTHS[:-1]))
IN_W = sum(SECTION_WIDTHS)

V7X_LANES = 128
V7X_VMEM_LIMIT_BYTES = 56 * 1024 * 1024

LOG2_E = math.log2(math.e)

PROJ_ROWS = 512
ATTN_BLOCK = 512
RET_CHUNK = 256
MLP_ROWS = 512
MLP_FF_CHUNK = 1024


def _bf16(t):
    return t.astype(jnp.bfloat16)


def _dot(a, b):
    return jnp.dot(a, b, preferred_element_type=jnp.float32)


def _dot_nt(a, b):
    return lax.dot_general(a, b, (((1,), (1,)), ((), ())), preferred_element_type=jnp.float32)


def _dot_tn(a, b):
    return lax.dot_general(a, b, (((0,), (0,)), ((), ())), preferred_element_type=jnp.float32)


def _in_proj_kernel(x_ref, n1w_ref, w_ref, qkw_ref, dac_ref, das_ref, rtc_ref, rts_ref, ones_ref,
                    dq_ref, dk_ref, dv_ref, rq_ref, rk_ref, rv_ref, rg_ref, gl_ref):
    x = x_ref[...]
    h = x * lax.rsqrt(jnp.mean(x * x, axis=-1, keepdims=True) + NORM_EPS) * n1w_ref[...]
    h = _bf16(h)

    def section(idx):
        lo = SECTION_STARTS[idx]
        return _dot(h, w_ref[:, lo:lo + SECTION_WIDTHS[idx]])

    rows = x.shape[0]
    reps = DA_QK_W // V7X_LANES
    cos_da = jnp.concatenate([dac_ref[...]] * reps, axis=1)
    sin_da = jnp.concatenate([das_ref[...]] * reps, axis=1)
    lane = lax.broadcasted_iota(jnp.int32, (rows, DA_QK_W), 1)
    first_half = (lane % DA_QK_DIM) < (DA_ROT_DIM // 2)
    ones_bd = ones_ref[...]
    for idx, out_ref in ((0, dq_ref), (1, dk_ref)):
        t = section(idx)
        ms = _dot(_bf16(t * t), ones_bd) * (1.0 / DA_QK_DIM)
        t = t * lax.rsqrt(ms + NORM_EPS) * qkw_ref[idx:idx + 1, :]
        partner = jnp.where(first_half,
                            pltpu.roll(t, DA_QK_W - DA_ROT_DIM // 2, axis=1),
                            pltpu.roll(t, DA_ROT_DIM // 2, axis=1))
        out_ref[...] = _bf16(t * cos_da + partner * sin_da)

    dv_ref[...] = _bf16(section(2))

    cos_rt = rtc_ref[...]
    sin_rt = rts_ref[...]
    for idx, out_ref, scale in ((3, rq_ref, 1.0), (4, rk_ref, RET_QK_DIM ** -0.5)):
        t = section(idx)
        for hh in range(RET_HEADS):
            th = t[:, hh * RET_QK_DIM:(hh + 1) * RET_QK_DIM]
            rot = th * cos_rt + pltpu.roll(th, RET_QK_DIM // 2, axis=1) * sin_rt
            if scale != 1.0:
                rot = rot * scale
            out_ref[:, hh * RET_QK_DIM:(hh + 1) * RET_QK_DIM] = _bf16(rot)

    rv_ref[...] = _bf16(section(5))
    rg_ref[...] = _bf16(section(6))
    gl_ref[...] = _bf16(section(7))


def _in_proj(x2, n1w, w_in, qkw, dac, das, rtc, rts, ones_bd):
    T = x2.shape[0]
    rows = min(PROJ_ROWS, T)
    assert T % rows == 0
    row_spec = lambda width: pl.BlockSpec((rows, width), lambda i: (i, 0))
    const_spec = lambda shape: pl.BlockSpec(shape, lambda i: (0, 0))
    out_widths = SECTION_WIDTHS
    return pl.pallas_call(
        _in_proj_kernel,
        grid=(T // rows,),
        in_specs=[
            row_spec(D_MODEL),
            const_spec((1, D_MODEL)),
            pl.BlockSpec((D_MODEL, IN_W), lambda i: (0, 0), pipeline_mode=pl.Buffered(1)),
            const_spec((2, DA_QK_W)),
            row_spec(V7X_LANES), row_spec(V7X_LANES), row_spec(V7X_LANES), row_spec(V7X_LANES),
            const_spec((DA_QK_W, DA_QK_W)),
        ],
        out_specs=[row_spec(w) for w in out_widths],
        out_shape=[jax.ShapeDtypeStruct((T, w), jnp.bfloat16) for w in out_widths],
        compiler_params=pltpu.CompilerParams(
            dimension_semantics=("parallel",), vmem_limit_bytes=V7X_VMEM_LIMIT_BYTES),
        name="in_proj",
    )(x2, n1w, w_in, qkw, dac, das, rtc, rts, ones_bd)


def _diff_attn_kernel(q_ref, k_ref, v_ref, lam_ref, sw_ref, o_ref,
                      qst_ref, s_ref, p_ref, acc_ref, *, qb, kb):
    qi = pl.program_id(2)
    nq = 2 * qb
    n_strip = nq // V7X_LANES
    f32 = jnp.float32

    qt = q_ref[...].astype(f32).T
    dim = lax.broadcasted_iota(jnp.int32, qt.shape, 0)
    qst_ref[:, :qb] = _bf16(jnp.where(dim < DA_QK_DIM, qt, 0.0))
    qst_ref[:, qb:] = _bf16(jnp.where(dim >= DA_QK_DIM, qt, 0.0))
    acc_ref[...] = jnp.zeros(acc_ref.shape, f32)
    p_ref[1] = jnp.zeros(p_ref.shape[1:], p_ref.dtype)

    def stage_a(t, slot):
        start = pl.multiple_of(t * kb, kb)
        s_ref[slot] = _dot(k_ref[pl.ds(start, kb), :], qst_ref[...])

    def stage_b(slot, m, l, key_offset):
        m_out, l_out, a_out = [], [], []
        for c in range(n_strip):
            cols = slice(c * V7X_LANES, (c + 1) * V7X_LANES)
            st = s_ref[slot, :, cols]
            if key_offset is not None:
                krow = lax.broadcasted_iota(jnp.int32, st.shape, 0) + key_offset
                qcol = lax.broadcasted_iota(jnp.int32, st.shape, 1) + (c * V7X_LANES) % qb
                st = jnp.where(krow <= qcol, st, MASK_VALUE)
            m_new = jnp.maximum(m[c], jnp.max(st, axis=0, keepdims=True))
            alpha = jnp.exp2(m[c] - m_new)
            p = jnp.exp2(st - m_new)
            l_out.append(alpha * l[c] + jnp.sum(p, axis=0, keepdims=True))
            p_ref[slot, :, cols] = _bf16(p)
            m_out.append(m_new)
            a_out.append(alpha)
        return tuple(m_out), tuple(l_out), tuple(a_out)

    def stage_c(t, slot, alpha):
        start = pl.multiple_of(jnp.maximum(t, 0) * kb, kb)
        vt = v_ref[pl.ds(start, kb), :]
        for c in range(0, n_strip, 2):
            cols = slice(c * V7X_LANES, (c + 2) * V7X_LANES)
            a = jnp.concatenate([alpha[c], alpha[c + 1]], axis=1)
            acc_ref[:, cols] = a * acc_ref[:, cols] + _dot_tn(vt, p_ref[slot, :, cols])

    def body(u, carry):
        m, l, alpha = carry
        t = 2 * u
        stage_c(t - 1, 1, alpha)
        stage_a(t + 1, 1)
        m, l, alpha = stage_b(0, m, l, None)
        stage_c(t, 0, alpha)
        stage_a(t + 2, 0)
        m, l, alpha = stage_b(1, m, l, None)
        return m, l, alpha

    row = lambda v: tuple(jnp.full((1, V7X_LANES), v, f32) for _ in range(n_strip))
    stage_a(0, 0)
    m, l, alpha = lax.fori_loop(0, qi, body, (row(MASK_VALUE), row(0.0), row(1.0)))
    t = 2 * qi
    stage_c(t - 1, 1, alpha)
    stage_a(t + 1, 1)
    m, l, alpha = stage_b(0, m, l, 0)
    stage_c(t, 0, alpha)
    m, l, alpha = stage_b(1, m, l, kb)
    stage_c(t + 1, 1, alpha)

    o = acc_ref[...] / jnp.concatenate(l, axis=1)
    o = o[:, :qb] - lam_ref[...] * o[:, qb:]
    o = o * lax.rsqrt(jnp.mean(o * o, axis=0, keepdims=True) + SUBLN_EPS)
    o_ref[...] = _bf16(o.T * sw_ref[...])


def _diff_attn(dq, dk, dv, lam, subln_w, B, S):
    blk = min(ATTN_BLOCK, S)
    kb = blk // 2
    assert S % blk == 0 and kb % V7X_LANES == 0
    nq = S // blk
    kernel = functools.partial(_diff_attn_kernel, qb=blk, kb=kb)
    return pl.pallas_call(
        kernel,
        grid=(B, DA_HEADS, nq),
        scratch_shapes=[
            pltpu.VMEM((DA_V_DIM, 2 * blk), jnp.bfloat16),
            pltpu.VMEM((2, kb, 2 * blk), jnp.float32),
            pltpu.VMEM((2, kb, 2 * blk), jnp.bfloat16),
            pltpu.VMEM((DA_V_DIM, 2 * blk), jnp.float32),
        ],
        in_specs=[
            pl.BlockSpec((blk, V7X_LANES), lambda b, h, i: (b * nq + i, h)),
            pl.BlockSpec((S, V7X_LANES), lambda b, h, i: (b, h)),
            pl.BlockSpec((S, V7X_LANES), lambda b, h, i: (b, h)),
            pl.BlockSpec((1, 1), lambda b, h, i: (0, 0)),
            pl.BlockSpec((1, DA_V_DIM), lambda b, h, i: (0, 0)),
        ],
        out_specs=pl.BlockSpec((blk, DA_V_DIM), lambda b, h, i: (b * nq + i, h)),
        out_shape=jax.ShapeDtypeStruct((B * S, DA_V_W), jnp.bfloat16),
        compiler_params=pltpu.CompilerParams(
            dimension_semantics=("parallel", "parallel", "arbitrary"),
            vmem_limit_bytes=V7X_VMEM_LIMIT_BYTES),
        name="diff_attn",
    )(dq, dk, dv, lam, subln_w)


def _retention_kernel(q_ref, k_ref, v_ref, g_ref, lg_ref, gnw_ref, o_ref, *, chunk, n_chunks):
    lg = lg_ref[0, :, 0:1]
    row = lax.broadcasted_iota(jnp.int32, (chunk, chunk), 0)
    col = lax.broadcasted_iota(jnp.int32, (chunk, chunk), 1)
    rel = (row - col).astype(jnp.float32)
    decay = jnp.where(rel >= 0, jnp.exp(lg * jnp.maximum(rel, 0.0)), 0.0)
    idx = lax.broadcasted_iota(jnp.int32, (chunk, 1), 0).astype(jnp.float32)
    xi = jnp.exp(lg * (idx + 1.0))
    zeta = jnp.exp(lg * (chunk - 1.0 - idx))
    gamma_c = jnp.exp(lg * float(chunk))
    gnw = gnw_ref[...]

    def body(c, state):
        start = pl.multiple_of(c * chunk, chunk)
        qc = q_ref[pl.ds(start, chunk), :]
        kc = k_ref[pl.ds(start, chunk), :]
        vc = v_ref[pl.ds(start, chunk), :]
        scores = _dot_nt(qc, kc) * decay
        inner = _dot(_bf16(scores), vc)
        cross = _dot(qc, _bf16(state)) * xi
        kz = _bf16(kc.astype(jnp.float32) * zeta)
        new_state = gamma_c * state + _dot_tn(kz, vc)
        o = inner + cross
        mu = jnp.mean(o, axis=-1, keepdims=True)
        d = o - mu
        var = jnp.mean(d * d, axis=-1, keepdims=True)
        on = d * lax.rsqrt(var + GN_EPS) * gnw
        g = g_ref[pl.ds(start, chunk), :].astype(jnp.float32)
        o_ref[pl.ds(start, chunk), :] = _bf16(g * jax.nn.sigmoid(g) * on)
        return new_state

    lax.fori_loop(0, n_chunks, body, jnp.zeros((RET_QK_DIM, RET_V_DIM), jnp.float32))


def _retention(rq, rk, rv, rg, lg_rows, gn_w, B, S):
    chunk = min(RET_CHUNK, S)
    assert S % chunk == 0
    kernel = functools.partial(_retention_kernel, chunk=chunk, n_chunks=S // chunk)
    qk_spec = pl.BlockSpec((S, RET_QK_DIM), lambda b, h: (b, h))
    v_spec = pl.BlockSpec((S, RET_V_DIM), lambda b, h: (b, h))
    return pl.pallas_call(
        kernel,
        grid=(B, RET_HEADS),
        in_specs=[
            qk_spec, qk_spec, v_spec, v_spec,
            pl.BlockSpec((1, 1, V7X_LANES), lambda b, h: (h, 0, 0)),
            pl.BlockSpec((1, RET_V_DIM), lambda b, h: (0, h)),
        ],
        out_specs=v_spec,
        out_shape=jax.ShapeDtypeStruct((B * S, RET_V_W), jnp.bfloat16),
        compiler_params=pltpu.CompilerParams(
            dimension_semantics=("parallel", "parallel"),
            vmem_limit_bytes=V7X_VMEM_LIMIT_BYTES),
        name="retention",
    )(rq, rk, rv, rg, lg_rows, gn_w)


def _out_mlp_kernel(x_ref, oa_ref, or_ref, gl_ref, wa_ref, wr_ref, wo_ref, n2w_ref, w1_ref, w2_ref,
                    o_ref):
    ya = _dot(oa_ref[...], wa_ref[...])
    yr = _dot(or_ref[...], wr_ref[...])
    ga = jax.nn.sigmoid(gl_ref[:, :D_MODEL].astype(jnp.float32))
    gr = jax.nn.sigmoid(gl_ref[:, D_MODEL:].astype(jnp.float32))
    merged = _bf16(ga * ya + gr * yr)
    x1 = x_ref[...] + _dot(merged, wo_ref[...])
    h2 = x1 * lax.rsqrt(jnp.mean(x1 * x1, axis=-1, keepdims=True) + NORM_EPS) * n2w_ref[...]
    h2 = _bf16(h2)
    acc = x1
    for c in range(D_FF // MLP_FF_CHUNK):
        lo = c * MLP_FF_CHUNK
        hid = jnp.maximum(_dot(h2, w1_ref[:, lo:lo + MLP_FF_CHUNK]), 0.0)
        acc = acc + _dot(_bf16(hid * hid), w2_ref[lo:lo + MLP_FF_CHUNK, :])
    o_ref[...] = acc


def _out_mlp(x2, oa, orr, gl, wa, wr, wo, n2w, w1, w2):
    T = x2.shape[0]
    rows = min(MLP_ROWS, T)
    assert T % rows == 0
    row_spec = lambda width: pl.BlockSpec((rows, width), lambda i: (i, 0))
    weight_spec = lambda shape: pl.BlockSpec(shape, lambda i: (0, 0), pipeline_mode=pl.Buffered(1))
    return pl.pallas_call(
        _out_mlp_kernel,
        grid=(T // rows,),
        in_specs=[
            row_spec(D_MODEL), row_spec(DA_V_W), row_spec(RET_V_W), row_spec(GATE_W),
            weight_spec((DA_V_W, D_MODEL)), weight_spec((RET_V_W, D_MODEL)),
            weight_spec((D_MODEL, D_MODEL)),
            pl.BlockSpec((1, D_MODEL), lambda i: (0, 0)),
            weight_spec((D_MODEL, D_FF)), weight_spec((D_FF, D_MODEL)),
        ],
        out_specs=row_spec(D_MODEL),
        out_shape=jax.ShapeDtypeStruct((T, D_MODEL), jnp.float32),
        compiler_params=pltpu.CompilerParams(
            dimension_semantics=("parallel",), vmem_limit_bytes=V7X_VMEM_LIMIT_BYTES),
        name="out_mlp",
    )(x2, oa, orr, gl, wa, wr, wo, n2w, w1, w2)


def _rotary_tables(positions):
    pos = positions.reshape(-1, 1).astype(jnp.float32)
    half = DA_ROT_DIM // 2
    inv_da = ROPE_THETA ** (-jnp.arange(0, DA_ROT_DIM, 2, dtype=jnp.float32) / DA_ROT_DIM)
    ang = pos * inv_da
    cos, sin = jnp.cos(ang), jnp.sin(ang)
    pad1 = jnp.ones((pos.shape[0], DA_QK_DIM - DA_ROT_DIM), jnp.float32)
    pad0 = jnp.zeros_like(pad1)
    dac = jnp.concatenate([cos, cos, pad1] * (V7X_LANES // DA_QK_DIM), axis=1)
    das = jnp.concatenate([-sin, sin, pad0] * (V7X_LANES // DA_QK_DIM), axis=1)
    inv_rt = 1.0 / (RET_THETA_BASE ** jnp.linspace(0.0, 1.0, RET_QK_DIM // 2, dtype=jnp.float32))
    ang = pos * inv_rt
    cos, sin = jnp.cos(ang), jnp.sin(ang)
    rtc = jnp.concatenate([cos, cos], axis=1)
    rts = jnp.concatenate([-sin, sin], axis=1)
    del half
    return dac, das, rtc, rts


def kernel(x, positions, norm1_w, w_in, q_norm_w, k_norm_w, lambda_q1, lambda_k1, lambda_q2,
           lambda_k2, da_subln_w, w_da_branch, ret_gn_w, w_ret_branch, w_out, norm2_w, w_mlp_in,
           w_mlp_out):
    B, S, _ = x.shape
    T = B * S
    f32 = jnp.float32
    x2 = x.reshape(T, D_MODEL)

    dac, das, rtc, rts = _rotary_tables(positions)
    q_scale = (DA_QK_DIM ** -0.5) * LOG2_E
    qkw = jnp.stack([jnp.tile(q_norm_w[0].astype(f32) * q_scale, DA_QK_W // DA_QK_DIM),
                     jnp.tile(k_norm_w[0].astype(f32), DA_QK_W // DA_QK_DIM)])
    blk = np.arange(DA_QK_W) // DA_QK_DIM
    ones_bd = jnp.asarray(blk[:, None] == blk[None, :], jnp.bfloat16)

    dq, dk, dv, rq, rk, rv, rg, gl = _in_proj(
        x2, norm1_w[0].astype(f32).reshape(1, D_MODEL), _bf16(w_in[0]), qkw, dac, das, rtc, rts,
        ones_bd)

    lam = (jnp.exp(jnp.sum(lambda_q1[0].astype(f32) * lambda_k1[0].astype(f32)))
           - jnp.exp(jnp.sum(lambda_q2[0].astype(f32) * lambda_k2[0].astype(f32)))
           + LAMBDA_INIT).reshape(1, 1)
    subln = (da_subln_w[0].astype(f32) * (1.0 - LAMBDA_INIT)).reshape(1, DA_V_DIM)
    oa = _diff_attn(dq, dk, dv, lam, subln, B, S)

    log_gamma = np.log1p(-np.exp2(-5.0 - np.arange(RET_HEADS, dtype=np.float64)))
    lg_rows = jnp.asarray(np.broadcast_to(log_gamma[:, None, None], (RET_HEADS, 1, V7X_LANES)), f32)
    orr = _retention(rq, rk, rv, rg, lg_rows, ret_gn_w[0].astype(f32).reshape(1, RET_V_W), B, S)

    out = _out_mlp(x2, oa, orr, gl, _bf16(w_da_branch[0]), _bf16(w_ret_branch[0]), _bf16(w_out[0]),
                   norm2_w[0].astype(f32).reshape(1, D_MODEL), _bf16(w_mlp_in[0]),
                   _bf16(w_mlp_out[0]))
    return out.reshape(B, S, D_MODEL)
```

```python
import functools
import math

import jax
import jax.numpy as jnp
import numpy as np
from jax import lax
from jax.experimental import pallas as pl
from jax.experimental.pallas import tpu as pltpu

D_MODEL = 1024
DA_HEADS = 4
DA_QK_DIM = 64
DA_V_DIM = 2 * DA_QK_DIM
DA_ROT_DIM = DA_QK_DIM // 4
ROPE_THETA = 500000.0
RET_HEADS = 4
RET_QK_DIM = 128
RET_V_DIM = 256
RET_THETA_BASE = 10000.0
D_FF = 4 * D_MODEL
NORM_EPS = 1e-6
SUBLN_EPS = 1e-5
GN_EPS = 1e-5
MASK_VALUE = -1e30
LAMBDA_INIT = 0.8 - 0.6 * math.exp(-0.3 * 0)

DA_QK_W = DA_HEADS * 2 * DA_QK_DIM
DA_V_W = DA_HEADS * DA_V_DIM
RET_QK_W = RET_HEADS * RET_QK_DIM
RET_V_W = RET_HEADS * RET_V_DIM
GATE_W = 2 * D_MODEL
SECTION_WIDTHS = (DA_QK_W, DA_QK_W, DA_V_W, RET_QK_W, RET_QK_W, RET_V_W, RET_V_W, GATE_W)
SECTION_STARTS = tuple(int(v) for v in np.cumsum((0,) + SECTION_WIDTHS[:-1]))
IN_W = sum(SECTION_WIDTHS)

V7X_LANES = 128
V7X_MXU_WIDTH = 256
V7X_VMEM_LIMIT_BYTES = 56 * 1024 * 1024

LOG2_E = math.log2(math.e)

PROJ_ROWS = 512
ATTN_BLOCK = 512
ATTN_BOUNDED_BLOCK = 1024
ATTN_BOUNDED_LOGIT = 60.0
RET_CHUNK = 256
MLP_ROWS = 512
MLP_FF_CHUNK = 1024


def _bf16(t):
    return t.astype(jnp.bfloat16)


def _dot(a, b):
    return jnp.dot(a, b, preferred_element_type=jnp.float32)


def _dot_nt(a, b):
    return lax.dot_general(a, b, (((1,), (1,)), ((), ())), preferred_element_type=jnp.float32)


def _dot_tn(a, b):
    return lax.dot_general(a, b, (((0,), (0,)), ((), ())), preferred_element_type=jnp.float32)


def _in_proj_kernel(x_ref, n1w_ref, w_ref, qkw_ref, dac_ref, das_ref, rtc_ref, rts_ref, ones_ref,
                    dq_ref, dk_ref, dv_ref, rq_ref, rk_ref, rv_ref, rg_ref, gl_ref):
    x = x_ref[...]
    h = x * lax.rsqrt(jnp.mean(x * x, axis=-1, keepdims=True) + NORM_EPS) * n1w_ref[...]
    h = _bf16(h)

    def section(idx):
        lo = SECTION_STARTS[idx]
        return _dot(h, w_ref[:, lo:lo + SECTION_WIDTHS[idx]])

    rows = x.shape[0]
    reps = DA_QK_W // V7X_LANES
    cos_da = jnp.concatenate([dac_ref[...]] * reps, axis=1)
    sin_da = jnp.concatenate([das_ref[...]] * reps, axis=1)
    lane = lax.broadcasted_iota(jnp.int32, (rows, DA_QK_W), 1)
    first_half = (lane % DA_QK_DIM) < (DA_ROT_DIM // 2)
    ones_bd = ones_ref[...]
    for idx, out_ref in ((0, dq_ref), (1, dk_ref)):
        t = section(idx)
        ms = _dot(_bf16(t * t), ones_bd) * (1.0 / DA_QK_DIM)
        t = t * lax.rsqrt(ms + NORM_EPS) * qkw_ref[idx:idx + 1, :]
        partner = jnp.where(first_half,
                            pltpu.roll(t, DA_QK_W - DA_ROT_DIM // 2, axis=1),
                            pltpu.roll(t, DA_ROT_DIM // 2, axis=1))
        out_ref[...] = _bf16(t * cos_da + partner * sin_da)

    dv_ref[...] = _bf16(section(2))

    cos_rt = rtc_ref[...]
    sin_rt = rts_ref[...]
    for idx, out_ref, scale in ((3, rq_ref, 1.0), (4, rk_ref, RET_QK_DIM ** -0.5)):
        t = section(idx)
        for hh in range(RET_HEADS):
            th = t[:, hh * RET_QK_DIM:(hh + 1) * RET_QK_DIM]
            rot = th * cos_rt + pltpu.roll(th, RET_QK_DIM // 2, axis=1) * sin_rt
            if scale != 1.0:
                rot = rot * scale
            out_ref[:, hh * RET_QK_DIM:(hh + 1) * RET_QK_DIM] = _bf16(rot)

    rv_ref[...] = _bf16(section(5))
    rg_ref[...] = _bf16(section(6))
    gl_ref[...] = _bf16(section(7))


def _in_proj(x2, n1w, w_in, qkw, dac, das, rtc, rts, ones_bd):
    T = x2.shape[0]
    rows = min(PROJ_ROWS, T)
    assert T % rows == 0
    row_spec = lambda width: pl.BlockSpec((rows, width), lambda i: (i, 0))
    const_spec = lambda shape: pl.BlockSpec(shape, lambda i: (0, 0))
    out_widths = SECTION_WIDTHS
    return pl.pallas_call(
        _in_proj_kernel,
        grid=(T // rows,),
        in_specs=[
            row_spec(D_MODEL),
            const_spec((1, D_MODEL)),
            pl.BlockSpec((D_MODEL, IN_W), lambda i: (0, 0), pipeline_mode=pl.Buffered(1)),
            const_spec((2, DA_QK_W)),
            row_spec(V7X_LANES), row_spec(V7X_LANES), row_spec(V7X_LANES), row_spec(V7X_LANES),
            const_spec((DA_QK_W, DA_QK_W)),
        ],
        out_specs=[row_spec(w) for w in out_widths],
        out_shape=[jax.ShapeDtypeStruct((T, w), jnp.bfloat16) for w in out_widths],
        compiler_params=pltpu.CompilerParams(
            dimension_semantics=("parallel",), vmem_limit_bytes=V7X_VMEM_LIMIT_BYTES),
        name="in_proj",
    )(x2, n1w, w_in, qkw, dac, das, rtc, rts, ones_bd)


def _diff_attn_kernel(q_ref, k_ref, v_ref, lam_ref, sw_ref, o_ref,
                      qst_ref, s_ref, p_ref, acc_ref, *, qb, kb):
    qi = pl.program_id(2)
    nq = 2 * qb
    n_strip = nq // V7X_LANES
    f32 = jnp.float32

    qt = q_ref[...].astype(f32).T
    dim = lax.broadcasted_iota(jnp.int32, qt.shape, 0)
    qst_ref[:, :qb] = _bf16(jnp.where(dim < DA_QK_DIM, qt, 0.0))
    qst_ref[:, qb:] = _bf16(jnp.where(dim >= DA_QK_DIM, qt, 0.0))
    acc_ref[...] = jnp.zeros(acc_ref.shape, f32)
    p_ref[1] = jnp.zeros(p_ref.shape[1:], p_ref.dtype)

    def stage_a(t, slot):
        start = pl.multiple_of(t * kb, kb)
        s_ref[slot] = _dot(k_ref[pl.ds(start, kb), :], qst_ref[...])

    def stage_b(slot, m, l, key_offset):
        m_out, l_out, a_out = [], [], []
        for c in range(n_strip):
            cols = slice(c * V7X_LANES, (c + 1) * V7X_LANES)
            st = s_ref[slot, :, cols]
            if key_offset is not None:
                krow = lax.broadcasted_iota(jnp.int32, st.shape, 0) + key_offset
                qcol = lax.broadcasted_iota(jnp.int32, st.shape, 1) + (c * V7X_LANES) % qb
                st = jnp.where(krow <= qcol, st, MASK_VALUE)
            m_new = jnp.maximum(m[c], jnp.max(st, axis=0, keepdims=True))
            alpha = jnp.exp2(m[c] - m_new)
            p = jnp.exp2(st - m_new)
            l_out.append(alpha * l[c] + jnp.sum(p, axis=0, keepdims=True))
            p_ref[slot, :, cols] = _bf16(p)
            m_out.append(m_new)
            a_out.append(alpha)
        return tuple(m_out), tuple(l_out), tuple(a_out)

    def stage_c(t, slot, alpha):
        start = pl.multiple_of(jnp.maximum(t, 0) * kb, kb)
        vt = v_ref[pl.ds(start, kb), :]
        for c in range(0, n_strip, 2):
            cols = slice(c * V7X_LANES, (c + 2) * V7X_LANES)
            a = jnp.concatenate([alpha[c], alpha[c + 1]], axis=1)
            acc_ref[:, cols] = a * acc_ref[:, cols] + _dot_tn(vt, p_ref[slot, :, cols])

    def body(u, carry):
        m, l, alpha = carry
        t = 2 * u
        stage_c(t - 1, 1, alpha)
        stage_a(t + 1, 1)
        m, l, alpha = stage_b(0, m, l, None)
        stage_c(t, 0, alpha)
        stage_a(t + 2, 0)
        m, l, alpha = stage_b(1, m, l, None)
        return m, l, alpha

    row = lambda v: tuple(jnp.full((1, V7X_LANES), v, f32) for _ in range(n_strip))
    stage_a(0, 0)
    m, l, alpha = lax.fori_loop(0, qi, body, (row(MASK_VALUE), row(0.0), row(1.0)))
    t = 2 * qi
    stage_c(t - 1, 1, alpha)
    stage_a(t + 1, 1)
    m, l, alpha = stage_b(0, m, l, 0)
    stage_c(t, 0, alpha)
    m, l, alpha = stage_b(1, m, l, kb)
    stage_c(t + 1, 1, alpha)

    o = acc_ref[...] / jnp.concatenate(l, axis=1)
    o = o[:, :qb] - lam_ref[...] * o[:, qb:]
    o = o * lax.rsqrt(jnp.mean(o * o, axis=0, keepdims=True) + SUBLN_EPS)
    o_ref[...] = _bf16(o.T * sw_ref[...])


def _diff_attn(dq, dk, dv, lam, subln_w, B, S):
    blk = min(ATTN_BLOCK, S)
    kb = blk // 2
    assert S % blk == 0 and kb % V7X_LANES == 0
    nq = S // blk
    kernel = functools.partial(_diff_attn_kernel, qb=blk, kb=kb)
    return pl.pallas_call(
        kernel,
        grid=(B, DA_HEADS, nq),
        scratch_shapes=[
            pltpu.VMEM((DA_V_DIM, 2 * blk), jnp.bfloat16),
            pltpu.VMEM((2, kb, 2 * blk), jnp.float32),
            pltpu.VMEM((2, kb, 2 * blk), jnp.bfloat16),
            pltpu.VMEM((DA_V_DIM, 2 * blk), jnp.float32),
        ],
        in_specs=[
            pl.BlockSpec((blk, V7X_LANES), lambda b, h, i: (b * nq + i, h)),
            pl.BlockSpec((S, V7X_LANES), lambda b, h, i: (b, h)),
            pl.BlockSpec((S, V7X_LANES), lambda b, h, i: (b, h)),
            pl.BlockSpec((1, 1), lambda b, h, i: (0, 0)),
            pl.BlockSpec((1, DA_V_DIM), lambda b, h, i: (0, 0)),
        ],
        out_specs=pl.BlockSpec((blk, DA_V_DIM), lambda b, h, i: (b * nq + i, h)),
        out_shape=jax.ShapeDtypeStruct((B * S, DA_V_W), jnp.bfloat16),
        compiler_params=pltpu.CompilerParams(
            dimension_semantics=("parallel", "parallel", "arbitrary"),
            vmem_limit_bytes=V7X_VMEM_LIMIT_BYTES),
        name="diff_attn",
    )(dq, dk, dv, lam, subln_w)


def _diff_attn_bounded_kernel(q_ref, k_ref, v_ref, lam_ref, sw_ref, o_ref,
                              qst_ref, p_ref, acc_ref, *, qb, kb):
    qi = pl.program_id(2)
    nq = 2 * qb
    tile = V7X_MXU_WIDTH
    n_tile = nq // tile
    n_diag = qb // kb
    f32 = jnp.float32
    assert kb == tile

    qt = q_ref[...].astype(f32).T
    dim = lax.broadcasted_iota(jnp.int32, qt.shape, 0)
    qst_ref[:, :qb] = _bf16(jnp.where(dim < DA_QK_DIM, qt, 0.0))
    qst_ref[:, qb:] = _bf16(jnp.where(dim >= DA_QK_DIM, qt, 0.0))
    acc_ref[...] = jnp.zeros(acc_ref.shape, f32)

    def probs(t, slot, lsum, tiles, diag_tile_token):
        start = pl.multiple_of(t * kb, kb)
        kblk = k_ref[pl.ds(start, kb), :]
        lsum = list(lsum)
        for ct in tiles:
            s = _dot(kblk, qst_ref[:, ct * tile:(ct + 1) * tile])
            if diag_tile_token is not None and (ct * tile) % qb == diag_tile_token:
                krow = lax.broadcasted_iota(jnp.int32, s.shape, 0)
                qcol = lax.broadcasted_iota(jnp.int32, s.shape, 1)
                s = jnp.where(krow <= qcol, s, MASK_VALUE)
            p = jnp.exp2(s)
            lsum[ct] = lsum[ct] + jnp.sum(p.reshape(kb // 8, 8, tile), axis=0)
            p_ref[slot, ct] = _bf16(p)
        return tuple(lsum)

    def accumulate(t, slot, tiles):
        start = pl.multiple_of(t * kb, kb)
        vblk = v_ref[pl.ds(start, kb), :]
        for ct in tiles:
            acc_ref[ct] += _dot_tn(vblk, p_ref[slot, ct])

    all_tiles = tuple(range(n_tile))

    def body(u, lsum):
        t = 2 * u
        lsum = probs(t, 0, lsum, all_tiles, None)
        lsum = probs(t + 1, 1, lsum, all_tiles, None)
        accumulate(t, 0, all_tiles)
        accumulate(t + 1, 1, all_tiles)
        return lsum

    lsum = tuple(jnp.zeros((8, tile), f32) for _ in range(n_tile))
    lsum = lax.fori_loop(0, (n_diag // 2) * qi, body, lsum)
    for d in range(n_diag):
        tiles = tuple(ct for ct in all_tiles if (ct * tile) % qb >= d * kb)
        lsum = probs(n_diag * qi + d, d % 2, lsum, tiles, d * kb)
        accumulate(n_diag * qi + d, d % 2, tiles)

    o = jnp.concatenate(
        [acc_ref[ct] / jnp.sum(lsum[ct], axis=0, keepdims=True) for ct in all_tiles], axis=1)
    o = o[:, :qb] - lam_ref[...] * o[:, qb:]
    o = o * lax.rsqrt(jnp.mean(o * o, axis=0, keepdims=True) + SUBLN_EPS)
    o_ref[...] = _bf16(o.T * sw_ref[...])


def _diff_attn_bounded(dq, dk, dv, lam, subln_w, B, S):
    blk = min(ATTN_BOUNDED_BLOCK, S)
    kb = V7X_MXU_WIDTH
    assert S % blk == 0 and blk % (2 * kb) == 0
    nq = S // blk
    n_tile = 2 * blk // V7X_MXU_WIDTH
    kernel = functools.partial(_diff_attn_bounded_kernel, qb=blk, kb=kb)
    return pl.pallas_call(
        kernel,
        grid=(B, DA_HEADS, nq),
        scratch_shapes=[
            pltpu.VMEM((DA_V_DIM, 2 * blk), jnp.bfloat16),
            pltpu.VMEM((2, n_tile, kb, V7X_MXU_WIDTH), jnp.bfloat16),
            pltpu.VMEM((n_tile, DA_V_DIM, V7X_MXU_WIDTH), jnp.float32),
        ],
        in_specs=[
            pl.BlockSpec((blk, V7X_LANES), lambda b, h, i: (b * nq + i, h)),
            pl.BlockSpec((S, V7X_LANES), lambda b, h, i: (b, h)),
            pl.BlockSpec((S, V7X_LANES), lambda b, h, i: (b, h)),
            pl.BlockSpec((1, 1), lambda b, h, i: (0, 0)),
            pl.BlockSpec((1, DA_V_DIM), lambda b, h, i: (0, 0)),
        ],
        out_specs=pl.BlockSpec((blk, DA_V_DIM), lambda b, h, i: (b * nq + i, h)),
        out_shape=jax.ShapeDtypeStruct((B * S, DA_V_W), jnp.bfloat16),
        compiler_params=pltpu.CompilerParams(
            dimension_semantics=("parallel", "parallel", "arbitrary"),
            vmem_limit_bytes=V7X_VMEM_LIMIT_BYTES),
        name="diff_attn_bounded",
    )(dq, dk, dv, lam, subln_w)


def _retention_kernel(q_ref, k_ref, v_ref, g_ref, lg_ref, gnw_ref, o_ref, *, chunk, n_chunks):
    lg = lg_ref[0, :, 0:1]
    row = lax.broadcasted_iota(jnp.int32, (chunk, chunk), 0)
    col = lax.broadcasted_iota(jnp.int32, (chunk, chunk), 1)
    rel = (row - col).astype(jnp.float32)
    decay = jnp.where(rel >= 0, jnp.exp(lg * jnp.maximum(rel, 0.0)), 0.0)
    idx = lax.broadcasted_iota(jnp.int32, (chunk, 1), 0).astype(jnp.float32)
    xi = jnp.exp(lg * (idx + 1.0))
    zeta = jnp.exp(lg * (chunk - 1.0 - idx))
    gamma_c = jnp.exp(lg * float(chunk))
    gnw = gnw_ref[...]

    def body(c, state):
        start = pl.multiple_of(c * chunk, chunk)
        qc = q_ref[pl.ds(start, chunk), :]
        kc = k_ref[pl.ds(start, chunk), :]
        vc = v_ref[pl.ds(start, chunk), :]
        scores = _dot_nt(qc, kc) * decay
        inner = _dot(_bf16(scores), vc)
        cross = _dot(qc, _bf16(state)) * xi
        kz = _bf16(kc.astype(jnp.float32) * zeta)
        new_state = gamma_c * state + _dot_tn(kz, vc)
        o = inner + cross
        mu = jnp.mean(o, axis=-1, keepdims=True)
        d = o - mu
        var = jnp.mean(d * d, axis=-1, keepdims=True)
        on = d * lax.rsqrt(var + GN_EPS) * gnw
        g = g_ref[pl.ds(start, chunk), :].astype(jnp.float32)
        o_ref[pl.ds(start, chunk), :] = _bf16(g * jax.nn.sigmoid(g) * on)
        return new_state

    lax.fori_loop(0, n_chunks, body, jnp.zeros((RET_QK_DIM, RET_V_DIM), jnp.float32))


def _retention(rq, rk, rv, rg, lg_rows, gn_w, B, S):
    chunk = min(RET_CHUNK, S)
    assert S % chunk == 0
    kernel = functools.partial(_retention_kernel, chunk=chunk, n_chunks=S // chunk)
    qk_spec = pl.BlockSpec((S, RET_QK_DIM), lambda b, h: (b, h))
    v_spec = pl.BlockSpec((S, RET_V_DIM), lambda b, h: (b, h))
    return pl.pallas_call(
        kernel,
        grid=(B, RET_HEADS),
        in_specs=[
            qk_spec, qk_spec, v_spec, v_spec,
            pl.BlockSpec((1, 1, V7X_LANES), lambda b, h: (h, 0, 0)),
            pl.BlockSpec((1, RET_V_DIM), lambda b, h: (0, h)),
        ],
        out_specs=v_spec,
        out_shape=jax.ShapeDtypeStruct((B * S, RET_V_W), jnp.bfloat16),
        compiler_params=pltpu.CompilerParams(
            dimension_semantics=("parallel", "parallel"),
            vmem_limit_bytes=V7X_VMEM_LIMIT_BYTES),
        name="retention",
    )(rq, rk, rv, rg, lg_rows, gn_w)


def _out_mlp_kernel(x_ref, oa_ref, or_ref, gl_ref, wa_ref, wr_ref, wo_ref, n2w_ref, w1_ref, w2_ref,
                    o_ref):
    ya = _dot(oa_ref[...], wa_ref[...])
    yr = _dot(or_ref[...], wr_ref[...])
    ga = jax.nn.sigmoid(gl_ref[:, :D_MODEL].astype(jnp.float32))
    gr = jax.nn.sigmoid(gl_ref[:, D_MODEL:].astype(jnp.float32))
    merged = _bf16(ga * ya + gr * yr)
    x1 = x_ref[...] + _dot(merged, wo_ref[...])
    h2 = x1 * lax.rsqrt(jnp.mean(x1 * x1, axis=-1, keepdims=True) + NORM_EPS) * n2w_ref[...]
    h2 = _bf16(h2)
    acc = x1
    for c in range(D_FF // MLP_FF_CHUNK):
        lo = c * MLP_FF_CHUNK
        hid = jnp.maximum(_dot(h2, w1_ref[:, lo:lo + MLP_FF_CHUNK]), 0.0)
        acc = acc + _dot(_bf16(hid * hid), w2_ref[lo:lo + MLP_FF_CHUNK, :])
    o_ref[...] = acc


def _out_mlp(x2, oa, orr, gl, wa, wr, wo, n2w, w1, w2):
    T = x2.shape[0]
    rows = min(MLP_ROWS, T)
    assert T % rows == 0
    row_spec = lambda width: pl.BlockSpec((rows, width), lambda i: (i, 0))
    weight_spec = lambda shape: pl.BlockSpec(shape, lambda i: (0, 0), pipeline_mode=pl.Buffered(1))
    return pl.pallas_call(
        _out_mlp_kernel,
        grid=(T // rows,),
        in_specs=[
            row_spec(D_MODEL), row_spec(DA_V_W), row_spec(RET_V_W), row_spec(GATE_W),
            weight_spec((DA_V_W, D_MODEL)), weight_spec((RET_V_W, D_MODEL)),
            weight_spec((D_MODEL, D_MODEL)),
            pl.BlockSpec((1, D_MODEL), lambda i: (0, 0)),
            weight_spec((D_MODEL, D_FF)), weight_spec((D_FF, D_MODEL)),
        ],
        out_specs=row_spec(D_MODEL),
        out_shape=jax.ShapeDtypeStruct((T, D_MODEL), jnp.float32),
        compiler_params=pltpu.CompilerParams(
            dimension_semantics=("parallel",), vmem_limit_bytes=V7X_VMEM_LIMIT_BYTES),
        name="out_mlp",
    )(x2, oa, orr, gl, wa, wr, wo, n2w, w1, w2)


def _rotary_tables(positions):
    pos = positions.reshape(-1, 1).astype(jnp.float32)
    half = DA_ROT_DIM // 2
    inv_da = ROPE_THETA ** (-jnp.arange(0, DA_ROT_DIM, 2, dtype=jnp.float32) / DA_ROT_DIM)
    ang = pos * inv_da
    cos, sin = jnp.cos(ang), jnp.sin(ang)
    pad1 = jnp.ones((pos.shape[0], DA_QK_DIM - DA_ROT_DIM), jnp.float32)
    pad0 = jnp.zeros_like(pad1)
    dac = jnp.concatenate([cos, cos, pad1] * (V7X_LANES // DA_QK_DIM), axis=1)
    das = jnp.concatenate([-sin, sin, pad0] * (V7X_LANES // DA_QK_DIM), axis=1)
    inv_rt = 1.0 / (RET_THETA_BASE ** jnp.linspace(0.0, 1.0, RET_QK_DIM // 2, dtype=jnp.float32))
    ang = pos * inv_rt
    cos, sin = jnp.cos(ang), jnp.sin(ang)
    rtc = jnp.concatenate([cos, cos], axis=1)
    rts = jnp.concatenate([-sin, sin], axis=1)
    del half
    return dac, das, rtc, rts


def kernel(x, positions, norm1_w, w_in, q_norm_w, k_norm_w, lambda_q1, lambda_k1, lambda_q2,
           lambda_k2, da_subln_w, w_da_branch, ret_gn_w, w_ret_branch, w_out, norm2_w, w_mlp_in,
           w_mlp_out):
    B, S, _ = x.shape
    T = B * S
    f32 = jnp.float32
    x2 = x.reshape(T, D_MODEL)

    dac, das, rtc, rts = _rotary_tables(positions)
    q_scale = (DA_QK_DIM ** -0.5) * LOG2_E
    qkw = jnp.stack([jnp.tile(q_norm_w[0].astype(f32) * q_scale, DA_QK_W // DA_QK_DIM),
                     jnp.tile(k_norm_w[0].astype(f32), DA_QK_W // DA_QK_DIM)])
    blk = np.arange(DA_QK_W) // DA_QK_DIM
    ones_bd = jnp.asarray(blk[:, None] == blk[None, :], jnp.bfloat16)

    dq, dk, dv, rq, rk, rv, rg, gl = _in_proj(
        x2, norm1_w[0].astype(f32).reshape(1, D_MODEL), _bf16(w_in[0]), qkw, dac, das, rtc, rts,
        ones_bd)

    lam = (jnp.exp(jnp.sum(lambda_q1[0].astype(f32) * lambda_k1[0].astype(f32)))
           - jnp.exp(jnp.sum(lambda_q2[0].astype(f32) * lambda_k2[0].astype(f32)))
           + LAMBDA_INIT).reshape(1, 1)
    subln = (da_subln_w[0].astype(f32) * (1.0 - LAMBDA_INIT)).reshape(1, DA_V_DIM)
    logit_bound = (DA_QK_DIM * jnp.max(jnp.abs(qkw[0])) * jnp.max(jnp.abs(qkw[1]))
                   * (1.0 + 2.0 ** -6))
    oa = lax.cond(logit_bound <= ATTN_BOUNDED_LOGIT,
                  functools.partial(_diff_attn_bounded, B=B, S=S),
                  functools.partial(_diff_attn, B=B, S=S),
                  dq, dk, dv, lam, subln)

    log_gamma = np.log1p(-np.exp2(-5.0 - np.arange(RET_HEADS, dtype=np.float64)))
    lg_rows = jnp.asarray(np.broadcast_to(log_gamma[:, None, None], (RET_HEADS, 1, V7X_LANES)), f32)
    orr = _retention(rq, rk, rv, rg, lg_rows, ret_gn_w[0].astype(f32).reshape(1, RET_V_W), B, S)

    out = _out_mlp(x2, oa, orr, gl, _bf16(w_da_branch[0]), _bf16(w_ret_branch[0]), _bf16(w_out[0]),
                   norm2_w[0].astype(f32).reshape(1, D_MODEL), _bf16(w_mlp_in[0]),
                   _bf16(w_mlp_out[0]))
    return out.reshape(B, S, D_MODEL)
```

```python
import functools
import math

import jax
import jax.numpy as jnp
import numpy as np
from jax import lax
from jax.experimental import pallas as pl
from jax.experimental.pallas import tpu as pltpu

D_MODEL = 1024
DA_HEADS = 4
DA_QK_DIM = 64
DA_V_DIM = 2 * DA_QK_DIM
DA_ROT_DIM = DA_QK_DIM // 4
ROPE_THETA = 500000.0
RET_HEADS = 4
RET_QK_DIM = 128
RET_V_DIM = 256
RET_THETA_BASE = 10000.0
D_FF = 4 * D_MODEL
NORM_EPS = 1e-6
SUBLN_EPS = 1e-5
GN_EPS = 1e-5
MASK_VALUE = -1e30
LAMBDA_INIT = 0.8 - 0.6 * math.exp(-0.3 * 0)

DA_QK_W = DA_HEADS * 2 * DA_QK_DIM
DA_V_W = DA_HEADS * DA_V_DIM
RET_QK_W = RET_HEADS * RET_QK_DIM
RET_V_W = RET_HEADS * RET_V_DIM
GATE_W = 2 * D_MODEL
SECTION_WIDTHS = (DA_QK_W, DA_QK_W, DA_V_W, RET_QK_W, RET_QK_W, RET_V_W, RET_V_W, GATE_W)
SECTION_STARTS = tuple(int(v) for v in np.cumsum((0,) + SECTION_WIDTHS[:-1]))
IN_W = sum(SECTION_WIDTHS)

V7X_LANES = 128
V7X_MXU_WIDTH = 256
V7X_VMEM_LIMIT_BYTES = 56 * 1024 * 1024

LOG2_E = math.log2(math.e)

PROJ_ROWS = 512
ATTN_BLOCK = 512
ATTN_BOUNDED_BLOCK = 1024
ATTN_BOUNDED_LOGIT = 60.0
RET_CHUNK = 256
MLP_ROWS = 512
MLP_FF_CHUNK = 1024


def _bf16(t):
    return t.astype(jnp.bfloat16)


def _dot(a, b):
    return jnp.dot(a, b, preferred_element_type=jnp.float32)


def _dot_nt(a, b):
    return lax.dot_general(a, b, (((1,), (1,)), ((), ())), preferred_element_type=jnp.float32)


def _dot_tn(a, b):
    return lax.dot_general(a, b, (((0,), (0,)), ((), ())), preferred_element_type=jnp.float32)


def _split_bf16(t):
    hi = _bf16(t)
    lo = _bf16(t - hi.astype(jnp.float32))
    return jnp.concatenate([hi, lo], axis=0)


def _in_proj_kernel(x_ref, pos_ref, n1w_ref, w_ref, qkw_ref, freq_ref, eda_ref, ert_ref, ones_ref,
                    dq_ref, dk_ref, dv_ref, rq_ref, rk_ref, rv_ref, rg_ref, gl_ref):
    x = x_ref[...]
    h = x * lax.rsqrt(jnp.mean(x * x, axis=-1, keepdims=True) + NORM_EPS) * n1w_ref[...]
    h = _bf16(h)

    def section(idx):
        lo = SECTION_STARTS[idx]
        return _dot(h, w_ref[:, lo:lo + SECTION_WIDTHS[idx]])

    rows = x.shape[0]
    n_da = DA_ROT_DIM // 2
    ang = jnp.concatenate([freq_ref[...]] * (rows // V7X_LANES), axis=1) * pos_ref[0]
    cos_t, sin_t = jnp.cos(ang), jnp.sin(ang)
    tab_da = _dot_tn(_split_bf16(jnp.concatenate([cos_t[:n_da], sin_t[:n_da]], axis=0)),
                     eda_ref[...])
    tab_rt = _dot_tn(_split_bf16(jnp.concatenate([cos_t[n_da:], sin_t[n_da:]], axis=0)),
                     ert_ref[...])
    lane1 = lax.broadcasted_iota(jnp.int32, (1, V7X_LANES), 1)
    unrotated = jnp.where(lane1 % DA_QK_DIM >= DA_ROT_DIM, 1.0, 0.0)

    cos_rt = tab_rt[:, :V7X_LANES]
    sin_rt = tab_rt[:, V7X_LANES:]
    for idx, out_ref, scale in ((3, rq_ref, 1.0), (4, rk_ref, RET_QK_DIM ** -0.5)):
        t = section(idx)
        for hh in range(RET_HEADS):
            th = t[:, hh * RET_QK_DIM:(hh + 1) * RET_QK_DIM]
            rot = th * cos_rt + pltpu.roll(th, RET_QK_DIM // 2, axis=1) * sin_rt
            if scale != 1.0:
                rot = rot * scale
            out_ref[:, hh * RET_QK_DIM:(hh + 1) * RET_QK_DIM] = _bf16(rot)

    reps = DA_QK_W // V7X_LANES
    cos_da = jnp.concatenate([tab_da[:, :V7X_LANES] + unrotated] * reps, axis=1)
    sin_da = jnp.concatenate([tab_da[:, V7X_LANES:]] * reps, axis=1)
    lane = lax.broadcasted_iota(jnp.int32, (rows, DA_QK_W), 1)
    first_half = (lane % DA_QK_DIM) < (DA_ROT_DIM // 2)
    ones_bd = ones_ref[...]
    for idx, out_ref in ((0, dq_ref), (1, dk_ref)):
        t = section(idx)
        ms = _dot(_bf16(t * t), ones_bd) * (1.0 / DA_QK_DIM)
        t = t * lax.rsqrt(ms + NORM_EPS) * qkw_ref[idx:idx + 1, :]
        partner = jnp.where(first_half,
                            pltpu.roll(t, DA_QK_W - DA_ROT_DIM // 2, axis=1),
                            pltpu.roll(t, DA_ROT_DIM // 2, axis=1))
        out_ref[...] = _bf16(t * cos_da + partner * sin_da)

    dv_ref[...] = _bf16(section(2))
    rv_ref[...] = _bf16(section(5))
    rg_ref[...] = _bf16(section(6))
    gl_ref[...] = _bf16(section(7))


def _rotary_constants():
    n_da, n_rt = DA_ROT_DIM // 2, RET_QK_DIM // 2
    inv_da = ROPE_THETA ** (-jnp.arange(0, DA_ROT_DIM, 2, dtype=jnp.float32) / DA_ROT_DIM)
    inv_rt = 1.0 / (RET_THETA_BASE ** jnp.linspace(0.0, 1.0, n_rt, dtype=jnp.float32))
    freq = jnp.broadcast_to(jnp.concatenate([inv_da, inv_rt])[:, None], (n_da + n_rt, V7X_LANES))

    def selection(n, period):
        e = np.zeros((4 * n, 2 * V7X_LANES), np.float32)
        for f in range(n):
            for base in range(0, V7X_LANES, period):
                for part in (0, 2 * n):
                    e[part + f, base + f] = e[part + f, base + n + f] = 1.0
                    e[part + n + f, V7X_LANES + base + f] = -1.0
                    e[part + n + f, V7X_LANES + base + n + f] = 1.0
        return jnp.asarray(e, jnp.bfloat16)

    return freq, selection(n_da, DA_QK_DIM), selection(n_rt, RET_QK_DIM)


def _in_proj(x2, pos, n1w, w_in, qkw, ones_bd):
    T = x2.shape[0]
    rows = min(PROJ_ROWS, T)
    assert T % rows == 0 and rows % V7X_LANES == 0
    freq, e_da, e_rt = _rotary_constants()
    pos = pos.reshape(T // rows, 1, rows)
    row_spec = lambda width: pl.BlockSpec((rows, width), lambda i: (i, 0))
    const_spec = lambda shape: pl.BlockSpec(shape, lambda i: (0, 0))
    out_widths = SECTION_WIDTHS
    return pl.pallas_call(
        _in_proj_kernel,
        grid=(T // rows,),
        in_specs=[
            row_spec(D_MODEL),
            pl.BlockSpec((1, 1, rows), lambda i: (i, 0, 0)),
            const_spec((1, D_MODEL)),
            pl.BlockSpec((D_MODEL, IN_W), lambda i: (0, 0), pipeline_mode=pl.Buffered(1)),
            const_spec((2, DA_QK_W)),
            const_spec(freq.shape), const_spec(e_da.shape), const_spec(e_rt.shape),
            const_spec((DA_QK_W, DA_QK_W)),
        ],
        out_specs=[row_spec(w) for w in out_widths],
        out_shape=[jax.ShapeDtypeStruct((T, w), jnp.bfloat16) for w in out_widths],
        compiler_params=pltpu.CompilerParams(
            dimension_semantics=("parallel",), vmem_limit_bytes=V7X_VMEM_LIMIT_BYTES),
        name="in_proj",
    )(x2, pos, n1w, w_in, qkw, freq, e_da, e_rt, ones_bd)


def _diff_attn_kernel(q_ref, k_ref, v_ref, lam_ref, sw_ref, o_ref,
                      qst_ref, s_ref, p_ref, acc_ref, *, qb, kb):
    qi = pl.program_id(2)
    nq = 2 * qb
    n_strip = nq // V7X_LANES
    f32 = jnp.float32

    qt = q_ref[...].astype(f32).T
    dim = lax.broadcasted_iota(jnp.int32, qt.shape, 0)
    qst_ref[:, :qb] = _bf16(jnp.where(dim < DA_QK_DIM, qt, 0.0))
    qst_ref[:, qb:] = _bf16(jnp.where(dim >= DA_QK_DIM, qt, 0.0))
    acc_ref[...] = jnp.zeros(acc_ref.shape, f32)
    p_ref[1] = jnp.zeros(p_ref.shape[1:], p_ref.dtype)

    def stage_a(t, slot):
        start = pl.multiple_of(t * kb, kb)
        s_ref[slot] = _dot(k_ref[pl.ds(start, kb), :], qst_ref[...])

    def stage_b(slot, m, l, key_offset):
        m_out, l_out, a_out = [], [], []
        for c in range(n_strip):
            cols = slice(c * V7X_LANES, (c + 1) * V7X_LANES)
            st = s_ref[slot, :, cols]
            if key_offset is not None:
                krow = lax.broadcasted_iota(jnp.int32, st.shape, 0) + key_offset
                qcol = lax.broadcasted_iota(jnp.int32, st.shape, 1) + (c * V7X_LANES) % qb
                st = jnp.where(krow <= qcol, st, MASK_VALUE)
            m_new = jnp.maximum(m[c], jnp.max(st, axis=0, keepdims=True))
            alpha = jnp.exp2(m[c] - m_new)
            p = jnp.exp2(st - m_new)
            l_out.append(alpha * l[c] + jnp.sum(p, axis=0, keepdims=True))
            p_ref[slot, :, cols] = _bf16(p)
            m_out.append(m_new)
            a_out.append(alpha)
        return tuple(m_out), tuple(l_out), tuple(a_out)

    def stage_c(t, slot, alpha):
        start = pl.multiple_of(jnp.maximum(t, 0) * kb, kb)
        vt = v_ref[pl.ds(start, kb), :]
        for c in range(0, n_strip, 2):
            cols = slice(c * V7X_LANES, (c + 2) * V7X_LANES)
            a = jnp.concatenate([alpha[c], alpha[c + 1]], axis=1)
            acc_ref[:, cols] = a * acc_ref[:, cols] + _dot_tn(vt, p_ref[slot, :, cols])

    def body(u, carry):
        m, l, alpha = carry
        t = 2 * u
        stage_c(t - 1, 1, alpha)
        stage_a(t + 1, 1)
        m, l, alpha = stage_b(0, m, l, None)
        stage_c(t, 0, alpha)
        stage_a(t + 2, 0)
        m, l, alpha = stage_b(1, m, l, None)
        return m, l, alpha

    row = lambda v: tuple(jnp.full((1, V7X_LANES), v, f32) for _ in range(n_strip))
    stage_a(0, 0)
    m, l, alpha = lax.fori_loop(0, qi, body, (row(MASK_VALUE), row(0.0), row(1.0)))
    t = 2 * qi
    stage_c(t - 1, 1, alpha)
    stage_a(t + 1, 1)
    m, l, alpha = stage_b(0, m, l, 0)
    stage_c(t, 0, alpha)
    m, l, alpha = stage_b(1, m, l, kb)
    stage_c(t + 1, 1, alpha)

    o = acc_ref[...] / jnp.concatenate(l, axis=1)
    o = o[:, :qb] - lam_ref[...] * o[:, qb:]
    o = o * lax.rsqrt(jnp.mean(o * o, axis=0, keepdims=True) + SUBLN_EPS)
    o_ref[...] = _bf16(o.T * sw_ref[...])


def _diff_attn(dq, dk, dv, lam, subln_w, B, S):
    blk = min(ATTN_BLOCK, S)
    kb = blk // 2
    assert S % blk == 0 and kb % V7X_LANES == 0
    nq = S // blk
    kernel = functools.partial(_diff_attn_kernel, qb=blk, kb=kb)
    return pl.pallas_call(
        kernel,
        grid=(B, DA_HEADS, nq),
        scratch_shapes=[
            pltpu.VMEM((DA_V_DIM, 2 * blk), jnp.bfloat16),
            pltpu.VMEM((2, kb, 2 * blk), jnp.float32),
            pltpu.VMEM((2, kb, 2 * blk), jnp.bfloat16),
            pltpu.VMEM((DA_V_DIM, 2 * blk), jnp.float32),
        ],
        in_specs=[
            pl.BlockSpec((blk, V7X_LANES), lambda b, h, i: (b * nq + i, h)),
            pl.BlockSpec((S, V7X_LANES), lambda b, h, i: (b, h)),
            pl.BlockSpec((S, V7X_LANES), lambda b, h, i: (b, h)),
            pl.BlockSpec((1, 1), lambda b, h, i: (0, 0)),
            pl.BlockSpec((1, DA_V_DIM), lambda b, h, i: (0, 0)),
        ],
        out_specs=pl.BlockSpec((blk, DA_V_DIM), lambda b, h, i: (b * nq + i, h)),
        out_shape=jax.ShapeDtypeStruct((B * S, DA_V_W), jnp.bfloat16),
        compiler_params=pltpu.CompilerParams(
            dimension_semantics=("parallel", "parallel", "arbitrary"),
            vmem_limit_bytes=V7X_VMEM_LIMIT_BYTES),
        name="diff_attn",
    )(dq, dk, dv, lam, subln_w)


def _diff_attn_bounded_kernel(q_ref, k_ref, v_ref, lam_ref, sw_ref, o_ref,
                              qst_ref, p_ref, acc_ref, *, qb, kb):
    qi = pl.program_id(2)
    nq = 2 * qb
    tile = V7X_MXU_WIDTH
    n_tile = nq // tile
    n_diag = qb // kb
    f32 = jnp.float32
    assert kb == tile

    qt = q_ref[...].astype(f32).T
    dim = lax.broadcasted_iota(jnp.int32, qt.shape, 0)
    qst_ref[:, :qb] = _bf16(jnp.where(dim < DA_QK_DIM, qt, 0.0))
    qst_ref[:, qb:] = _bf16(jnp.where(dim >= DA_QK_DIM, qt, 0.0))
    acc_ref[...] = jnp.zeros(acc_ref.shape, f32)

    def probs(t, slot, lsum, tiles, diag_tile_token):
        start = pl.multiple_of(t * kb, kb)
        kblk = k_ref[pl.ds(start, kb), :]
        lsum = list(lsum)
        for ct in tiles:
            s = _dot(kblk, qst_ref[:, ct * tile:(ct + 1) * tile])
            if diag_tile_token is not None and (ct * tile) % qb == diag_tile_token:
                krow = lax.broadcasted_iota(jnp.int32, s.shape, 0)
                qcol = lax.broadcasted_iota(jnp.int32, s.shape, 1)
                s = jnp.where(krow <= qcol, s, MASK_VALUE)
            p = jnp.exp2(s)
            lsum[ct] = lsum[ct] + jnp.sum(p.reshape(kb // 8, 8, tile), axis=0)
            p_ref[slot, ct] = _bf16(p)
        return tuple(lsum)

    def accumulate(t, slot, tiles):
        start = pl.multiple_of(t * kb, kb)
        vblk = v_ref[pl.ds(start, kb), :]
        for ct in tiles:
            acc_ref[ct] += _dot_tn(vblk, p_ref[slot, ct])

    all_tiles = tuple(range(n_tile))

    def body(u, lsum):
        t = 2 * u
        lsum = probs(t, 0, lsum, all_tiles, None)
        lsum = probs(t + 1, 1, lsum, all_tiles, None)
        accumulate(t, 0, all_tiles)
        accumulate(t + 1, 1, all_tiles)
        return lsum

    lsum = tuple(jnp.zeros((8, tile), f32) for _ in range(n_tile))
    lsum = lax.fori_loop(0, (n_diag // 2) * qi, body, lsum)
    for d in range(n_diag):
        tiles = tuple(ct for ct in all_tiles if (ct * tile) % qb >= d * kb)
        lsum = probs(n_diag * qi + d, d % 2, lsum, tiles, d * kb)
        accumulate(n_diag * qi + d, d % 2, tiles)

    o = jnp.concatenate(
        [acc_ref[ct] / jnp.sum(lsum[ct], axis=0, keepdims=True) for ct in all_tiles], axis=1)
    o = o[:, :qb] - lam_ref[...] * o[:, qb:]
    o = o * lax.rsqrt(jnp.mean(o * o, axis=0, keepdims=True) + SUBLN_EPS)
    o_ref[...] = _bf16(o.T * sw_ref[...])


def _diff_attn_bounded(dq, dk, dv, lam, subln_w, B, S):
    blk = min(ATTN_BOUNDED_BLOCK, S)
    kb = V7X_MXU_WIDTH
    assert S % blk == 0 and blk % (2 * kb) == 0
    nq = S // blk
    n_tile = 2 * blk // V7X_MXU_WIDTH
    kernel = functools.partial(_diff_attn_bounded_kernel, qb=blk, kb=kb)
    return pl.pallas_call(
        kernel,
        grid=(B, DA_HEADS, nq),
        scratch_shapes=[
            pltpu.VMEM((DA_V_DIM, 2 * blk), jnp.bfloat16),
            pltpu.VMEM((2, n_tile, kb, V7X_MXU_WIDTH), jnp.bfloat16),
            pltpu.VMEM((n_tile, DA_V_DIM, V7X_MXU_WIDTH), jnp.float32),
        ],
        in_specs=[
            pl.BlockSpec((blk, V7X_LANES), lambda b, h, i: (b * nq + i, h)),
            pl.BlockSpec((S, V7X_LANES), lambda b, h, i: (b, h)),
            pl.BlockSpec((S, V7X_LANES), lambda b, h, i: (b, h)),
            pl.BlockSpec((1, 1), lambda b, h, i: (0, 0)),
            pl.BlockSpec((1, DA_V_DIM), lambda b, h, i: (0, 0)),
        ],
        out_specs=pl.BlockSpec((blk, DA_V_DIM), lambda b, h, i: (b * nq + i, h)),
        out_shape=jax.ShapeDtypeStruct((B * S, DA_V_W), jnp.bfloat16),
        compiler_params=pltpu.CompilerParams(
            dimension_semantics=("parallel", "parallel", "arbitrary"),
            vmem_limit_bytes=V7X_VMEM_LIMIT_BYTES),
        name="diff_attn_bounded",
    )(dq, dk, dv, lam, subln_w)


def _retention_kernel(q_ref, k_ref, v_ref, g_ref, lg_ref, gnw_ref, o_ref, *, chunk, n_chunks):
    lg = lg_ref[0, :, 0:1]
    row = lax.broadcasted_iota(jnp.int32, (chunk, chunk), 0)
    col = lax.broadcasted_iota(jnp.int32, (chunk, chunk), 1)
    rel = (row - col).astype(jnp.float32)
    decay = jnp.where(rel >= 0, jnp.exp(lg * jnp.maximum(rel, 0.0)), 0.0)
    idx = lax.broadcasted_iota(jnp.int32, (chunk, 1), 0).astype(jnp.float32)
    xi = jnp.exp(lg * (idx + 1.0))
    zeta = jnp.exp(lg * (chunk - 1.0 - idx))
    gamma_c = jnp.exp(lg * float(chunk))
    gnw = gnw_ref[...]

    def body(c, state):
        start = pl.multiple_of(c * chunk, chunk)
        qc = q_ref[pl.ds(start, chunk), :]
        kc = k_ref[pl.ds(start, chunk), :]
        vc = v_ref[pl.ds(start, chunk), :]
        scores = _dot_nt(qc, kc) * decay
        inner = _dot(_bf16(scores), vc)
        cross = _dot(qc, _bf16(state)) * xi
        kz = _bf16(kc.astype(jnp.float32) * zeta)
        new_state = gamma_c * state + _dot_tn(kz, vc)
        o = inner + cross
        mu = jnp.mean(o, axis=-1, keepdims=True)
        d = o - mu
        var = jnp.mean(d * d, axis=-1, keepdims=True)
        on = d * lax.rsqrt(var + GN_EPS) * gnw
        g = g_ref[pl.ds(start, chunk), :].astype(jnp.float32)
        o_ref[pl.ds(start, chunk), :] = _bf16(g * jax.nn.sigmoid(g) * on)
        return new_state

    lax.fori_loop(0, n_chunks, body, jnp.zeros((RET_QK_DIM, RET_V_DIM), jnp.float32))


def _retention(rq, rk, rv, rg, lg_rows, gn_w, B, S):
    chunk = min(RET_CHUNK, S)
    assert S % chunk == 0
    kernel = functools.partial(_retention_kernel, chunk=chunk, n_chunks=S // chunk)
    qk_spec = pl.BlockSpec((S, RET_QK_DIM), lambda b, h: (b, h))
    v_spec = pl.BlockSpec((S, RET_V_DIM), lambda b, h: (b, h))
    return pl.pallas_call(
        kernel,
        grid=(B, RET_HEADS),
        in_specs=[
            qk_spec, qk_spec, v_spec, v_spec,
            pl.BlockSpec((1, 1, V7X_LANES), lambda b, h: (h, 0, 0)),
            pl.BlockSpec((1, RET_V_DIM), lambda b, h: (0, h)),
        ],
        out_specs=v_spec,
        out_shape=jax.ShapeDtypeStruct((B * S, RET_V_W), jnp.bfloat16),
        compiler_params=pltpu.CompilerParams(
            dimension_semantics=("parallel", "parallel"),
            vmem_limit_bytes=V7X_VMEM_LIMIT_BYTES),
        name="retention",
    )(rq, rk, rv, rg, lg_rows, gn_w)


def _out_mlp_kernel(x_ref, oa_ref, or_ref, gl_ref, wa_ref, wr_ref, wo_ref, n2w_ref, w1_ref, w2_ref,
                    o_ref):
    ya = _dot(oa_ref[...], wa_ref[...])
    yr = _dot(or_ref[...], wr_ref[...])
    ga = jax.nn.sigmoid(gl_ref[:, :D_MODEL].astype(jnp.float32))
    gr = jax.nn.sigmoid(gl_ref[:, D_MODEL:].astype(jnp.float32))
    merged = _bf16(ga * ya + gr * yr)
    x1 = x_ref[...] + _dot(merged, wo_ref[...])
    h2 = x1 * lax.rsqrt(jnp.mean(x1 * x1, axis=-1, keepdims=True) + NORM_EPS) * n2w_ref[...]
    h2 = _bf16(h2)
    acc = x1
    for c in range(D_FF // MLP_FF_CHUNK):
        lo = c * MLP_FF_CHUNK
        hid = jnp.maximum(_dot(h2, w1_ref[:, lo:lo + MLP_FF_CHUNK]), 0.0)
        acc = acc + _dot(_bf16(hid * hid), w2_ref[lo:lo + MLP_FF_CHUNK, :])
    o_ref[...] = acc


def _out_mlp(x2, oa, orr, gl, wa, wr, wo, n2w, w1, w2):
    T = x2.shape[0]
    rows = min(MLP_ROWS, T)
    assert T % rows == 0
    row_spec = lambda width: pl.BlockSpec((rows, width), lambda i: (i, 0))
    weight_spec = lambda shape: pl.BlockSpec(shape, lambda i: (0, 0), pipeline_mode=pl.Buffered(1))
    return pl.pallas_call(
        _out_mlp_kernel,
        grid=(T // rows,),
        in_specs=[
            row_spec(D_MODEL), row_spec(DA_V_W), row_spec(RET_V_W), row_spec(GATE_W),
            weight_spec((DA_V_W, D_MODEL)), weight_spec((RET_V_W, D_MODEL)),
            weight_spec((D_MODEL, D_MODEL)),
            pl.BlockSpec((1, D_MODEL), lambda i: (0, 0)),
            weight_spec((D_MODEL, D_FF)), weight_spec((D_FF, D_MODEL)),
        ],
        out_specs=row_spec(D_MODEL),
        out_shape=jax.ShapeDtypeStruct((T, D_MODEL), jnp.float32),
        compiler_params=pltpu.CompilerParams(
            dimension_semantics=("parallel",), vmem_limit_bytes=V7X_VMEM_LIMIT_BYTES),
        name="out_mlp",
    )(x2, oa, orr, gl, wa, wr, wo, n2w, w1, w2)


def kernel(x, positions, norm1_w, w_in, q_norm_w, k_norm_w, lambda_q1, lambda_k1, lambda_q2,
           lambda_k2, da_subln_w, w_da_branch, ret_gn_w, w_ret_branch, w_out, norm2_w, w_mlp_in,
           w_mlp_out):
    B, S, _ = x.shape
    T = B * S
    f32 = jnp.float32
    x2 = x.reshape(T, D_MODEL)

    q_scale = (DA_QK_DIM ** -0.5) * LOG2_E
    qkw = jnp.stack([jnp.tile(q_norm_w[0].astype(f32) * q_scale, DA_QK_W // DA_QK_DIM),
                     jnp.tile(k_norm_w[0].astype(f32), DA_QK_W // DA_QK_DIM)])
    blk = np.arange(DA_QK_W) // DA_QK_DIM
    ones_bd = jnp.asarray(blk[:, None] == blk[None, :], jnp.bfloat16)

    dq, dk, dv, rq, rk, rv, rg, gl = _in_proj(
        x2, positions.astype(f32), norm1_w[0].astype(f32).reshape(1, D_MODEL), _bf16(w_in[0]), qkw,
        ones_bd)

    lam = (jnp.exp(jnp.sum(lambda_q1[0].astype(f32) * lambda_k1[0].astype(f32)))
           - jnp.exp(jnp.sum(lambda_q2[0].astype(f32) * lambda_k2[0].astype(f32)))
           + LAMBDA_INIT).reshape(1, 1)
    subln = (da_subln_w[0].astype(f32) * (1.0 - LAMBDA_INIT)).reshape(1, DA_V_DIM)
    logit_bound = (DA_QK_DIM * jnp.max(jnp.abs(qkw[0])) * jnp.max(jnp.abs(qkw[1]))
                   * (1.0 + 2.0 ** -6))
    oa = lax.cond(logit_bound <= ATTN_BOUNDED_LOGIT,
                  functools.partial(_diff_attn_bounded, B=B, S=S),
                  functools.partial(_diff_attn, B=B, S=S),
                  dq, dk, dv, lam, subln)

    log_gamma = np.log1p(-np.exp2(-5.0 - np.arange(RET_HEADS, dtype=np.float64)))
    lg_rows = jnp.asarray(np.broadcast_to(log_gamma[:, None, None], (RET_HEADS, 1, V7X_LANES)), f32)
    orr = _retention(rq, rk, rv, rg, lg_rows, ret_gn_w[0].astype(f32).reshape(1, RET_V_W), B, S)

    out = _out_mlp(x2, oa, orr, gl, _bf16(w_da_branch[0]), _bf16(w_ret_branch[0]), _bf16(w_out[0]),
                   norm2_w[0].astype(f32).reshape(1, D_MODEL), _bf16(w_mlp_in[0]),
                   _bf16(w_mlp_out[0]))
    return out.reshape(B, S, D_MODEL)
```

```python
import functools
import math

import jax
import jax.numpy as jnp
import numpy as np
from jax import lax
from jax.experimental import pallas as pl
from jax.experimental.pallas import tpu as pltpu

D_MODEL = 1024
DA_HEADS = 4
DA_QK_DIM = 64
DA_V_DIM = 2 * DA_QK_DIM
DA_ROT_DIM = DA_QK_DIM // 4
ROPE_THETA = 500000.0
RET_HEADS = 4
RET_QK_DIM = 128
RET_V_DIM = 256
RET_THETA_BASE = 10000.0
D_FF = 4 * D_MODEL
NORM_EPS = 1e-6
SUBLN_EPS = 1e-5
GN_EPS = 1e-5
MASK_VALUE = -1e30
LAMBDA_INIT = 0.8 - 0.6 * math.exp(-0.3 * 0)

DA_QK_W = DA_HEADS * 2 * DA_QK_DIM
DA_V_W = DA_HEADS * DA_V_DIM
RET_QK_W = RET_HEADS * RET_QK_DIM
RET_V_W = RET_HEADS * RET_V_DIM
GATE_W = 2 * D_MODEL
SECTION_WIDTHS = (DA_QK_W, DA_QK_W, DA_V_W, RET_QK_W, RET_QK_W, RET_V_W, RET_V_W, GATE_W)
SECTION_STARTS = tuple(int(v) for v in np.cumsum((0,) + SECTION_WIDTHS[:-1]))
IN_W = sum(SECTION_WIDTHS)

V7X_LANES = 128
V7X_MXU_WIDTH = 256
V7X_VMEM_LIMIT_BYTES = 56 * 1024 * 1024

LOG2_E = math.log2(math.e)

PROJ_ROWS = 512
ATTN_BLOCK = 512
ATTN_BOUNDED_BLOCK = 1024
ATTN_BOUNDED_LOGIT = 60.0
RET_CHUNK = 256
RET_SEQ_BLOCK = 1024
MLP_ROWS = 512
MLP_FF_CHUNK = 1024


def _bf16(t):
    return t.astype(jnp.bfloat16)


def _dot(a, b):
    return jnp.dot(a, b, preferred_element_type=jnp.float32)


def _dot_nt(a, b):
    return lax.dot_general(a, b, (((1,), (1,)), ((), ())), preferred_element_type=jnp.float32)


def _dot_tn(a, b):
    return lax.dot_general(a, b, (((0,), (0,)), ((), ())), preferred_element_type=jnp.float32)


def _split_bf16(t):
    hi = _bf16(t)
    lo = _bf16(t - hi.astype(jnp.float32))
    return jnp.concatenate([hi, lo], axis=0)


def _in_proj_kernel(x_ref, pos_ref, n1w_ref, w_ref, qkw_ref, freq_ref, eda_ref, ert_ref, ones_ref,
                    dq_ref, dk_ref, dv_ref, rq_ref, rk_ref, rv_ref, rg_ref, gl_ref):
    x = x_ref[...]
    h = x * lax.rsqrt(jnp.mean(x * x, axis=-1, keepdims=True) + NORM_EPS) * n1w_ref[...]
    h = _bf16(h)

    def section(idx):
        lo = SECTION_STARTS[idx]
        return _dot(h, w_ref[:, lo:lo + SECTION_WIDTHS[idx]])

    rows = x.shape[0]
    n_da = DA_ROT_DIM // 2
    ang = jnp.concatenate([freq_ref[...]] * (rows // V7X_LANES), axis=1) * pos_ref[0]
    cos_t, sin_t = jnp.cos(ang), jnp.sin(ang)
    tab_da = _dot_tn(_split_bf16(jnp.concatenate([cos_t[:n_da], sin_t[:n_da]], axis=0)),
                     eda_ref[...])
    tab_rt = _dot_tn(_split_bf16(jnp.concatenate([cos_t[n_da:], sin_t[n_da:]], axis=0)),
                     ert_ref[...])
    lane1 = lax.broadcasted_iota(jnp.int32, (1, V7X_LANES), 1)
    unrotated = jnp.where(lane1 % DA_QK_DIM >= DA_ROT_DIM, 1.0, 0.0)

    cos_rt = tab_rt[:, :V7X_LANES]
    sin_rt = tab_rt[:, V7X_LANES:]
    for idx, out_ref, scale in ((3, rq_ref, 1.0), (4, rk_ref, RET_QK_DIM ** -0.5)):
        t = section(idx)
        for hh in range(RET_HEADS):
            th = t[:, hh * RET_QK_DIM:(hh + 1) * RET_QK_DIM]
            rot = th * cos_rt + pltpu.roll(th, RET_QK_DIM // 2, axis=1) * sin_rt
            if scale != 1.0:
                rot = rot * scale
            out_ref[:, hh * RET_QK_DIM:(hh + 1) * RET_QK_DIM] = _bf16(rot)

    reps = DA_QK_W // V7X_LANES
    cos_da = jnp.concatenate([tab_da[:, :V7X_LANES] + unrotated] * reps, axis=1)
    sin_da = jnp.concatenate([tab_da[:, V7X_LANES:]] * reps, axis=1)
    lane = lax.broadcasted_iota(jnp.int32, (rows, DA_QK_W), 1)
    first_half = (lane % DA_QK_DIM) < (DA_ROT_DIM // 2)
    ones_bd = ones_ref[...]
    for idx, out_ref in ((0, dq_ref), (1, dk_ref)):
        t = section(idx)
        ms = _dot(_bf16(t * t), ones_bd) * (1.0 / DA_QK_DIM)
        t = t * lax.rsqrt(ms + NORM_EPS) * qkw_ref[idx:idx + 1, :]
        partner = jnp.where(first_half,
                            pltpu.roll(t, DA_QK_W - DA_ROT_DIM // 2, axis=1),
                            pltpu.roll(t, DA_ROT_DIM // 2, axis=1))
        out_ref[...] = _bf16(t * cos_da + partner * sin_da)

    dv_ref[...] = _bf16(section(2))
    rv_ref[...] = _bf16(section(5))
    rg_ref[...] = _bf16(section(6))
    gl_ref[...] = _bf16(section(7))


def _rotary_constants():
    n_da, n_rt = DA_ROT_DIM // 2, RET_QK_DIM // 2
    inv_da = ROPE_THETA ** (-jnp.arange(0, DA_ROT_DIM, 2, dtype=jnp.float32) / DA_ROT_DIM)
    inv_rt = 1.0 / (RET_THETA_BASE ** jnp.linspace(0.0, 1.0, n_rt, dtype=jnp.float32))
    freq = jnp.broadcast_to(jnp.concatenate([inv_da, inv_rt])[:, None], (n_da + n_rt, V7X_LANES))

    def selection(n, period):
        e = np.zeros((4 * n, 2 * V7X_LANES), np.float32)
        for f in range(n):
            for base in range(0, V7X_LANES, period):
                for part in (0, 2 * n):
                    e[part + f, base + f] = e[part + f, base + n + f] = 1.0
                    e[part + n + f, V7X_LANES + base + f] = -1.0
                    e[part + n + f, V7X_LANES + base + n + f] = 1.0
        return jnp.asarray(e, jnp.bfloat16)

    return freq, selection(n_da, DA_QK_DIM), selection(n_rt, RET_QK_DIM)


def _in_proj(x2, pos, n1w, w_in, qkw, ones_bd):
    T = x2.shape[0]
    rows = min(PROJ_ROWS, T)
    assert T % rows == 0 and rows % V7X_LANES == 0
    freq, e_da, e_rt = _rotary_constants()
    pos = pos.reshape(T // rows, 1, rows)
    row_spec = lambda width: pl.BlockSpec((rows, width), lambda i: (i, 0))
    const_spec = lambda shape: pl.BlockSpec(shape, lambda i: (0, 0))
    out_widths = SECTION_WIDTHS
    return pl.pallas_call(
        _in_proj_kernel,
        grid=(T // rows,),
        in_specs=[
            row_spec(D_MODEL),
            pl.BlockSpec((1, 1, rows), lambda i: (i, 0, 0)),
            const_spec((1, D_MODEL)),
            pl.BlockSpec((D_MODEL, IN_W), lambda i: (0, 0), pipeline_mode=pl.Buffered(1)),
            const_spec((2, DA_QK_W)),
            const_spec(freq.shape), const_spec(e_da.shape), const_spec(e_rt.shape),
            const_spec((DA_QK_W, DA_QK_W)),
        ],
        out_specs=[row_spec(w) for w in out_widths],
        out_shape=[jax.ShapeDtypeStruct((T, w), jnp.bfloat16) for w in out_widths],
        compiler_params=pltpu.CompilerParams(
            dimension_semantics=("parallel",), vmem_limit_bytes=V7X_VMEM_LIMIT_BYTES),
        name="in_proj",
    )(x2, pos, n1w, w_in, qkw, freq, e_da, e_rt, ones_bd)


def _diff_attn_kernel(q_ref, k_ref, v_ref, lam_ref, sw_ref, o_ref,
                      qst_ref, s_ref, p_ref, acc_ref, *, qb, kb):
    qi = pl.program_id(2)
    nq = 2 * qb
    n_strip = nq // V7X_LANES
    f32 = jnp.float32

    qt = q_ref[...].astype(f32).T
    dim = lax.broadcasted_iota(jnp.int32, qt.shape, 0)
    qst_ref[:, :qb] = _bf16(jnp.where(dim < DA_QK_DIM, qt, 0.0))
    qst_ref[:, qb:] = _bf16(jnp.where(dim >= DA_QK_DIM, qt, 0.0))
    acc_ref[...] = jnp.zeros(acc_ref.shape, f32)
    p_ref[1] = jnp.zeros(p_ref.shape[1:], p_ref.dtype)

    def stage_a(t, slot):
        start = pl.multiple_of(t * kb, kb)
        s_ref[slot] = _dot(k_ref[pl.ds(start, kb), :], qst_ref[...])

    def stage_b(slot, m, l, key_offset):
        m_out, l_out, a_out = [], [], []
        for c in range(n_strip):
            cols = slice(c * V7X_LANES, (c + 1) * V7X_LANES)
            st = s_ref[slot, :, cols]
            if key_offset is not None:
                krow = lax.broadcasted_iota(jnp.int32, st.shape, 0) + key_offset
                qcol = lax.broadcasted_iota(jnp.int32, st.shape, 1) + (c * V7X_LANES) % qb
                st = jnp.where(krow <= qcol, st, MASK_VALUE)
            m_new = jnp.maximum(m[c], jnp.max(st, axis=0, keepdims=True))
            alpha = jnp.exp2(m[c] - m_new)
            p = jnp.exp2(st - m_new)
            l_out.append(alpha * l[c] + jnp.sum(p, axis=0, keepdims=True))
            p_ref[slot, :, cols] = _bf16(p)
            m_out.append(m_new)
            a_out.append(alpha)
        return tuple(m_out), tuple(l_out), tuple(a_out)

    def stage_c(t, slot, alpha):
        start = pl.multiple_of(jnp.maximum(t, 0) * kb, kb)
        vt = v_ref[pl.ds(start, kb), :]
        for c in range(0, n_strip, 2):
            cols = slice(c * V7X_LANES, (c + 2) * V7X_LANES)
            a = jnp.concatenate([alpha[c], alpha[c + 1]], axis=1)
            acc_ref[:, cols] = a * acc_ref[:, cols] + _dot_tn(vt, p_ref[slot, :, cols])

    def body(u, carry):
        m, l, alpha = carry
        t = 2 * u
        stage_c(t - 1, 1, alpha)
        stage_a(t + 1, 1)
        m, l, alpha = stage_b(0, m, l, None)
        stage_c(t, 0, alpha)
        stage_a(t + 2, 0)
        m, l, alpha = stage_b(1, m, l, None)
        return m, l, alpha

    row = lambda v: tuple(jnp.full((1, V7X_LANES), v, f32) for _ in range(n_strip))
    stage_a(0, 0)
    m, l, alpha = lax.fori_loop(0, qi, body, (row(MASK_VALUE), row(0.0), row(1.0)))
    t = 2 * qi
    stage_c(t - 1, 1, alpha)
    stage_a(t + 1, 1)
    m, l, alpha = stage_b(0, m, l, 0)
    stage_c(t, 0, alpha)
    m, l, alpha = stage_b(1, m, l, kb)
    stage_c(t + 1, 1, alpha)

    o = acc_ref[...] / jnp.concatenate(l, axis=1)
    o = o[:, :qb] - lam_ref[...] * o[:, qb:]
    o = o * lax.rsqrt(jnp.mean(o * o, axis=0, keepdims=True) + SUBLN_EPS)
    o_ref[...] = _bf16(o.T * sw_ref[...])


def _diff_attn(dq, dk, dv, lam, subln_w, B, S):
    blk = min(ATTN_BLOCK, S)
    kb = blk // 2
    assert S % blk == 0 and kb % V7X_LANES == 0
    nq = S // blk
    kernel = functools.partial(_diff_attn_kernel, qb=blk, kb=kb)
    return pl.pallas_call(
        kernel,
        grid=(B, DA_HEADS, nq),
        scratch_shapes=[
            pltpu.VMEM((DA_V_DIM, 2 * blk), jnp.bfloat16),
            pltpu.VMEM((2, kb, 2 * blk), jnp.float32),
            pltpu.VMEM((2, kb, 2 * blk), jnp.bfloat16),
            pltpu.VMEM((DA_V_DIM, 2 * blk), jnp.float32),
        ],
        in_specs=[
            pl.BlockSpec((blk, V7X_LANES), lambda b, h, i: (b * nq + i, h)),
            pl.BlockSpec((S, V7X_LANES), lambda b, h, i: (b, h)),
            pl.BlockSpec((S, V7X_LANES), lambda b, h, i: (b, h)),
            pl.BlockSpec((1, 1), lambda b, h, i: (0, 0)),
            pl.BlockSpec((1, DA_V_DIM), lambda b, h, i: (0, 0)),
        ],
        out_specs=pl.BlockSpec((blk, DA_V_DIM), lambda b, h, i: (b * nq + i, h)),
        out_shape=jax.ShapeDtypeStruct((B * S, DA_V_W), jnp.bfloat16),
        compiler_params=pltpu.CompilerParams(
            dimension_semantics=("parallel", "parallel", "arbitrary"),
            vmem_limit_bytes=V7X_VMEM_LIMIT_BYTES),
        name="diff_attn",
    )(dq, dk, dv, lam, subln_w)


def _diff_attn_bounded_kernel(q_ref, k_ref, v_ref, lam_ref, sw_ref, o_ref,
                              qst_ref, p_ref, acc_ref, *, qb, kb):
    qi = pl.program_id(2)
    nq = 2 * qb
    tile = V7X_MXU_WIDTH
    n_tile = nq // tile
    n_diag = qb // kb
    f32 = jnp.float32
    assert kb == tile

    qt = q_ref[...].astype(f32).T
    dim = lax.broadcasted_iota(jnp.int32, qt.shape, 0)
    qst_ref[:, :qb] = _bf16(jnp.where(dim < DA_QK_DIM, qt, 0.0))
    qst_ref[:, qb:] = _bf16(jnp.where(dim >= DA_QK_DIM, qt, 0.0))
    acc_ref[...] = jnp.zeros(acc_ref.shape, f32)

    def probs(t, slot, lsum, tiles, diag_tile_token):
        start = pl.multiple_of(t * kb, kb)
        kblk = k_ref[pl.ds(start, kb), :]
        lsum = list(lsum)
        for ct in tiles:
            s = _dot(kblk, qst_ref[:, ct * tile:(ct + 1) * tile])
            if diag_tile_token is not None and (ct * tile) % qb == diag_tile_token:
                krow = lax.broadcasted_iota(jnp.int32, s.shape, 0)
                qcol = lax.broadcasted_iota(jnp.int32, s.shape, 1)
                s = jnp.where(krow <= qcol, s, MASK_VALUE)
            p = jnp.exp2(s)
            lsum[ct] = lsum[ct] + jnp.sum(p.reshape(kb // 8, 8, tile), axis=0)
            p_ref[slot, ct] = _bf16(p)
        return tuple(lsum)

    def accumulate(t, slot, tiles):
        start = pl.multiple_of(t * kb, kb)
        vblk = v_ref[pl.ds(start, kb), :]
        for ct in tiles:
            acc_ref[ct] += _dot_tn(vblk, p_ref[slot, ct])

    all_tiles = tuple(range(n_tile))

    def body(u, lsum):
        t = 2 * u
        lsum = probs(t, 0, lsum, all_tiles, None)
        lsum = probs(t + 1, 1, lsum, all_tiles, None)
        accumulate(t, 0, all_tiles)
        accumulate(t + 1, 1, all_tiles)
        return lsum

    lsum = tuple(jnp.zeros((8, tile), f32) for _ in range(n_tile))
    lsum = lax.fori_loop(0, (n_diag // 2) * qi, body, lsum)
    for d in range(n_diag):
        tiles = tuple(ct for ct in all_tiles if (ct * tile) % qb >= d * kb)
        lsum = probs(n_diag * qi + d, d % 2, lsum, tiles, d * kb)
        accumulate(n_diag * qi + d, d % 2, tiles)

    o = jnp.concatenate(
        [acc_ref[ct] / jnp.sum(lsum[ct], axis=0, keepdims=True) for ct in all_tiles], axis=1)
    o = o[:, :qb] - lam_ref[...] * o[:, qb:]
    o = o * lax.rsqrt(jnp.mean(o * o, axis=0, keepdims=True) + SUBLN_EPS)
    o_ref[...] = _bf16(o.T * sw_ref[...])


def _diff_attn_bounded(dq, dk, dv, lam, subln_w, B, S):
    blk = min(ATTN_BOUNDED_BLOCK, S)
    kb = V7X_MXU_WIDTH
    assert S % blk == 0 and blk % (2 * kb) == 0
    nq = S // blk
    n_tile = 2 * blk // V7X_MXU_WIDTH
    kernel = functools.partial(_diff_attn_bounded_kernel, qb=blk, kb=kb)
    return pl.pallas_call(
        kernel,
        grid=(B, DA_HEADS, nq),
        scratch_shapes=[
            pltpu.VMEM((DA_V_DIM, 2 * blk), jnp.bfloat16),
            pltpu.VMEM((2, n_tile, kb, V7X_MXU_WIDTH), jnp.bfloat16),
            pltpu.VMEM((n_tile, DA_V_DIM, V7X_MXU_WIDTH), jnp.float32),
        ],
        in_specs=[
            pl.BlockSpec((blk, V7X_LANES), lambda b, h, i: (b * nq + i, h)),
            pl.BlockSpec((S, V7X_LANES), lambda b, h, i: (b, h)),
            pl.BlockSpec((S, V7X_LANES), lambda b, h, i: (b, h)),
            pl.BlockSpec((1, 1), lambda b, h, i: (0, 0)),
            pl.BlockSpec((1, DA_V_DIM), lambda b, h, i: (0, 0)),
        ],
        out_specs=pl.BlockSpec((blk, DA_V_DIM), lambda b, h, i: (b * nq + i, h)),
        out_shape=jax.ShapeDtypeStruct((B * S, DA_V_W), jnp.bfloat16),
        compiler_params=pltpu.CompilerParams(
            dimension_semantics=("parallel", "parallel", "arbitrary"),
            vmem_limit_bytes=V7X_VMEM_LIMIT_BYTES),
        name="diff_attn_bounded",
    )(dq, dk, dv, lam, subln_w)


def _retention_kernel(q_ref, k_ref, v_ref, g_ref, lg_ref, gnw_ref, o_ref, state_ref, decay_ref,
                      *, chunk, n_chunks):
    f32 = jnp.float32
    idx = lax.broadcasted_iota(jnp.int32, (chunk, 1), 0).astype(f32)

    @pl.when(pl.program_id(1) == 0)
    def _():
        state_ref[...] = jnp.zeros(state_ref.shape, f32)
        row = lax.broadcasted_iota(jnp.int32, (chunk, chunk), 0)
        col = lax.broadcasted_iota(jnp.int32, (chunk, chunk), 1)
        rel = (row - col).astype(f32)
        for h in range(RET_HEADS):
            lg = lg_ref[h, :, 0:1]
            decay_ref[h] = jnp.where(rel >= 0, jnp.exp(lg * jnp.maximum(rel, 0.0)), 0.0)

    def body(c, carry):
        start = pl.multiple_of(c * chunk, chunk)
        rows = pl.ds(start, chunk)
        heads = range(RET_HEADS)
        qk_cols = [slice(h * RET_QK_DIM, (h + 1) * RET_QK_DIM) for h in heads]
        v_cols = [slice(h * RET_V_DIM, (h + 1) * RET_V_DIM) for h in heads]
        lg = [lg_ref[h, :, 0:1] for h in heads]
        qc = [q_ref[rows, qk_cols[h]] for h in heads]
        kc = [k_ref[rows, qk_cols[h]] for h in heads]
        scores = [_bf16(_dot_nt(qc[h], kc[h]) * decay_ref[h]) for h in heads]
        cross = [_dot(qc[h], _bf16(state_ref[h])) * jnp.exp(lg[h] * (idx + 1.0)) for h in heads]
        o = [_dot(scores[h], v_ref[rows, v_cols[h]]) + cross[h] for h in heads]
        for h in heads:
            kz = _bf16(kc[h].astype(f32) * jnp.exp(lg[h] * (chunk - 1.0 - idx)))
            state_ref[h] = (jnp.exp(lg[h] * float(chunk)) * state_ref[h]
                            + _dot_tn(kz, v_ref[rows, v_cols[h]]))
        mu = [jnp.mean(o[h], axis=-1, keepdims=True) for h in heads]
        d = [o[h] - mu[h] for h in heads]
        var = [jnp.mean(d[h] * d[h], axis=-1, keepdims=True) for h in heads]
        for h in heads:
            on = d[h] * lax.rsqrt(var[h] + GN_EPS) * gnw_ref[:, v_cols[h]]
            g = g_ref[rows, v_cols[h]].astype(f32)
            o_ref[rows, v_cols[h]] = _bf16(g * jax.nn.sigmoid(g) * on)
        return carry

    lax.fori_loop(0, n_chunks, body, 0)


def _retention(rq, rk, rv, rg, lg_rows, gn_w, B, S):
    chunk = min(RET_CHUNK, S)
    sblk = min(RET_SEQ_BLOCK, S)
    assert S % sblk == 0 and sblk % chunk == 0
    nsb = S // sblk
    kernel = functools.partial(_retention_kernel, chunk=chunk, n_chunks=sblk // chunk)
    qk_spec = pl.BlockSpec((sblk, RET_QK_W), lambda b, s: (b * nsb + s, 0))
    v_spec = pl.BlockSpec((sblk, RET_V_W), lambda b, s: (b * nsb + s, 0))
    return pl.pallas_call(
        kernel,
        grid=(B, nsb),
        scratch_shapes=[
            pltpu.VMEM((RET_HEADS, RET_QK_DIM, RET_V_DIM), jnp.float32),
            pltpu.VMEM((RET_HEADS, chunk, chunk), jnp.float32),
        ],
        in_specs=[
            qk_spec, qk_spec, v_spec, v_spec,
            pl.BlockSpec((RET_HEADS, 1, V7X_LANES), lambda b, s: (0, 0, 0)),
            pl.BlockSpec((1, RET_V_W), lambda b, s: (0, 0)),
        ],
        out_specs=v_spec,
        out_shape=jax.ShapeDtypeStruct((B * S, RET_V_W), jnp.bfloat16),
        compiler_params=pltpu.CompilerParams(
            dimension_semantics=("parallel", "arbitrary"),
            vmem_limit_bytes=V7X_VMEM_LIMIT_BYTES),
        name="retention",
    )(rq, rk, rv, rg, lg_rows, gn_w)


def _out_mlp_kernel(x_ref, oa_ref, or_ref, gl_ref, wa_ref, wr_ref, wo_ref, n2w_ref, w1_ref, w2_ref,
                    o_ref):
    ya = _dot(oa_ref[...], wa_ref[...])
    yr = _dot(or_ref[...], wr_ref[...])
    ga = jax.nn.sigmoid(gl_ref[:, :D_MODEL].astype(jnp.float32))
    gr = jax.nn.sigmoid(gl_ref[:, D_MODEL:].astype(jnp.float32))
    merged = _bf16(ga * ya + gr * yr)
    x1 = x_ref[...] + _dot(merged, wo_ref[...])
    h2 = x1 * lax.rsqrt(jnp.mean(x1 * x1, axis=-1, keepdims=True) + NORM_EPS) * n2w_ref[...]
    h2 = _bf16(h2)
    acc = x1
    for c in range(D_FF // MLP_FF_CHUNK):
        lo = c * MLP_FF_CHUNK
        hid = jnp.maximum(_dot(h2, w1_ref[:, lo:lo + MLP_FF_CHUNK]), 0.0)
        acc = acc + _dot(_bf16(hid * hid), w2_ref[lo:lo + MLP_FF_CHUNK, :])
    o_ref[...] = acc


def _out_mlp(x2, oa, orr, gl, wa, wr, wo, n2w, w1, w2):
    T = x2.shape[0]
    rows = min(MLP_ROWS, T)
    assert T % rows == 0
    row_spec = lambda width: pl.BlockSpec((rows, width), lambda i: (i, 0))
    weight_spec = lambda shape: pl.BlockSpec(shape, lambda i: (0, 0), pipeline_mode=pl.Buffered(1))
    return pl.pallas_call(
        _out_mlp_kernel,
        grid=(T // rows,),
        in_specs=[
            row_spec(D_MODEL), row_spec(DA_V_W), row_spec(RET_V_W), row_spec(GATE_W),
            weight_spec((DA_V_W, D_MODEL)), weight_spec((RET_V_W, D_MODEL)),
            weight_spec((D_MODEL, D_MODEL)),
            pl.BlockSpec((1, D_MODEL), lambda i: (0, 0)),
            weight_spec((D_MODEL, D_FF)), weight_spec((D_FF, D_MODEL)),
        ],
        out_specs=row_spec(D_MODEL),
        out_shape=jax.ShapeDtypeStruct((T, D_MODEL), jnp.float32),
        compiler_params=pltpu.CompilerParams(
            dimension_semantics=("parallel",), vmem_limit_bytes=V7X_VMEM_LIMIT_BYTES),
        name="out_mlp",
    )(x2, oa, orr, gl, wa, wr, wo, n2w, w1, w2)


def kernel(x, positions, norm1_w, w_in, q_norm_w, k_norm_w, lambda_q1, lambda_k1, lambda_q2,
           lambda_k2, da_subln_w, w_da_branch, ret_gn_w, w_ret_branch, w_out, norm2_w, w_mlp_in,
           w_mlp_out):
    B, S, _ = x.shape
    T = B * S
    f32 = jnp.float32
    x2 = x.reshape(T, D_MODEL)

    q_scale = (DA_QK_DIM ** -0.5) * LOG2_E
    qkw = jnp.stack([jnp.tile(q_norm_w[0].astype(f32) * q_scale, DA_QK_W // DA_QK_DIM),
                     jnp.tile(k_norm_w[0].astype(f32), DA_QK_W // DA_QK_DIM)])
    blk = np.arange(DA_QK_W) // DA_QK_DIM
    ones_bd = jnp.asarray(blk[:, None] == blk[None, :], jnp.bfloat16)

    dq, dk, dv, rq, rk, rv, rg, gl = _in_proj(
        x2, positions.astype(f32), norm1_w[0].astype(f32).reshape(1, D_MODEL), _bf16(w_in[0]), qkw,
        ones_bd)

    lam = (jnp.exp(jnp.sum(lambda_q1[0].astype(f32) * lambda_k1[0].astype(f32)))
           - jnp.exp(jnp.sum(lambda_q2[0].astype(f32) * lambda_k2[0].astype(f32)))
           + LAMBDA_INIT).reshape(1, 1)
    subln = (da_subln_w[0].astype(f32) * (1.0 - LAMBDA_INIT)).reshape(1, DA_V_DIM)
    logit_bound = (DA_QK_DIM * jnp.max(jnp.abs(qkw[0])) * jnp.max(jnp.abs(qkw[1]))
                   * (1.0 + 2.0 ** -6))
    oa = lax.cond(logit_bound <= ATTN_BOUNDED_LOGIT,
                  functools.partial(_diff_attn_bounded, B=B, S=S),
                  functools.partial(_diff_attn, B=B, S=S),
                  dq, dk, dv, lam, subln)

    log_gamma = np.log1p(-np.exp2(-5.0 - np.arange(RET_HEADS, dtype=np.float64)))
    lg_rows = jnp.asarray(np.broadcast_to(log_gamma[:, None, None], (RET_HEADS, 1, V7X_LANES)), f32)
    orr = _retention(rq, rk, rv, rg, lg_rows, ret_gn_w[0].astype(f32).reshape(1, RET_V_W), B, S)

    out = _out_mlp(x2, oa, orr, gl, _bf16(w_da_branch[0]), _bf16(w_ret_branch[0]), _bf16(w_out[0]),
                   norm2_w[0].astype(f32).reshape(1, D_MODEL), _bf16(w_mlp_in[0]),
                   _bf16(w_mlp_out[0]))
    return out.reshape(B, S, D_MODEL)
```

```python
import functools
import math

import jax
import jax.numpy as jnp
import numpy as np
from jax import lax
from jax.experimental import pallas as pl
from jax.experimental.pallas import tpu as pltpu

D_MODEL = 1024
DA_HEADS = 4
DA_QK_DIM = 64
DA_V_DIM = 2 * DA_QK_DIM
DA_ROT_DIM = DA_QK_DIM // 4
ROPE_THETA = 500000.0
RET_HEADS = 4
RET_QK_DIM = 128
RET_V_DIM = 256
RET_THETA_BASE = 10000.0
D_FF = 4 * D_MODEL
NORM_EPS = 1e-6
SUBLN_EPS = 1e-5
GN_EPS = 1e-5
MASK_VALUE = -1e30
LAMBDA_INIT = 0.8 - 0.6 * math.exp(-0.3 * 0)

DA_QK_W = DA_HEADS * 2 * DA_QK_DIM
DA_V_W = DA_HEADS * DA_V_DIM
RET_QK_W = RET_HEADS * RET_QK_DIM
RET_V_W = RET_HEADS * RET_V_DIM
GATE_W = 2 * D_MODEL
SECTION_WIDTHS = (DA_QK_W, DA_QK_W, DA_V_W, RET_QK_W, RET_QK_W, RET_V_W, RET_V_W, GATE_W)
SECTION_STARTS = tuple(int(v) for v in np.cumsum((0,) + SECTION_WIDTHS[:-1]))
IN_W = sum(SECTION_WIDTHS)

V7X_LANES = 128
V7X_MXU_WIDTH = 256
V7X_VMEM_LIMIT_BYTES = 56 * 1024 * 1024

LOG2_E = math.log2(math.e)

PROJ_ROWS = 512
ATTN_BLOCK = 512
ATTN_BOUNDED_BLOCK = 1024
ATTN_BOUNDED_LOGIT = 60.0
ATTN_PV_LAG = 2
RET_CHUNK = 256
RET_SEQ_BLOCK = 1024
MLP_ROWS = 512
MLP_FF_CHUNK = 1024


def _bf16(t):
    return t.astype(jnp.bfloat16)


def _dot(a, b):
    return jnp.dot(a, b, preferred_element_type=jnp.float32)


def _dot_nt(a, b):
    return lax.dot_general(a, b, (((1,), (1,)), ((), ())), preferred_element_type=jnp.float32)


def _dot_tn(a, b):
    return lax.dot_general(a, b, (((0,), (0,)), ((), ())), preferred_element_type=jnp.float32)


def _split_bf16(t):
    hi = _bf16(t)
    lo = _bf16(t - hi.astype(jnp.float32))
    return jnp.concatenate([hi, lo], axis=0)


def _in_proj_kernel(x_ref, pos_ref, n1w_ref, w_ref, qkw_ref, freq_ref, eda_ref, ert_ref, ones_ref,
                    dq_ref, dk_ref, dv_ref, rq_ref, rk_ref, rv_ref, rg_ref, gl_ref):
    x = x_ref[...]
    h = x * lax.rsqrt(jnp.mean(x * x, axis=-1, keepdims=True) + NORM_EPS) * n1w_ref[...]
    h = _bf16(h)

    def section(idx):
        lo = SECTION_STARTS[idx]
        return _dot(h, w_ref[:, lo:lo + SECTION_WIDTHS[idx]])

    rows = x.shape[0]
    n_da = DA_ROT_DIM // 2
    ang = jnp.concatenate([freq_ref[...]] * (rows // V7X_LANES), axis=1) * pos_ref[0]
    cos_t, sin_t = jnp.cos(ang), jnp.sin(ang)
    tab_da = _dot_tn(_split_bf16(jnp.concatenate([cos_t[:n_da], sin_t[:n_da]], axis=0)),
                     eda_ref[...])
    tab_rt = _dot_tn(_split_bf16(jnp.concatenate([cos_t[n_da:], sin_t[n_da:]], axis=0)),
                     ert_ref[...])
    lane1 = lax.broadcasted_iota(jnp.int32, (1, V7X_LANES), 1)
    unrotated = jnp.where(lane1 % DA_QK_DIM >= DA_ROT_DIM, 1.0, 0.0)

    cos_rt = tab_rt[:, :V7X_LANES]
    sin_rt = tab_rt[:, V7X_LANES:]
    for idx, out_ref, scale in ((3, rq_ref, 1.0), (4, rk_ref, RET_QK_DIM ** -0.5)):
        t = section(idx)
        for hh in range(RET_HEADS):
            th = t[:, hh * RET_QK_DIM:(hh + 1) * RET_QK_DIM]
            rot = th * cos_rt + pltpu.roll(th, RET_QK_DIM // 2, axis=1) * sin_rt
            if scale != 1.0:
                rot = rot * scale
            out_ref[:, hh * RET_QK_DIM:(hh + 1) * RET_QK_DIM] = _bf16(rot)

    reps = DA_QK_W // V7X_LANES
    cos_da = jnp.concatenate([tab_da[:, :V7X_LANES] + unrotated] * reps, axis=1)
    sin_da = jnp.concatenate([tab_da[:, V7X_LANES:]] * reps, axis=1)
    lane = lax.broadcasted_iota(jnp.int32, (rows, DA_QK_W), 1)
    first_half = (lane % DA_QK_DIM) < (DA_ROT_DIM // 2)
    ones_bd = ones_ref[...]
    for idx, out_ref in ((0, dq_ref), (1, dk_ref)):
        t = section(idx)
        ms = _dot(_bf16(t * t), ones_bd) * (1.0 / DA_QK_DIM)
        t = t * lax.rsqrt(ms + NORM_EPS) * qkw_ref[idx:idx + 1, :]
        partner = jnp.where(first_half,
                            pltpu.roll(t, DA_QK_W - DA_ROT_DIM // 2, axis=1),
                            pltpu.roll(t, DA_ROT_DIM // 2, axis=1))
        rot = t * cos_da + partner * sin_da
        out_ref[...] = _bf16(rot.T if idx == 0 else rot)

    dv_ref[...] = _bf16(section(2))
    rv_ref[...] = _bf16(section(5))
    rg_ref[...] = _bf16(section(6))
    gl_ref[...] = _bf16(section(7))


def _rotary_constants():
    n_da, n_rt = DA_ROT_DIM // 2, RET_QK_DIM // 2
    inv_da = ROPE_THETA ** (-jnp.arange(0, DA_ROT_DIM, 2, dtype=jnp.float32) / DA_ROT_DIM)
    inv_rt = 1.0 / (RET_THETA_BASE ** jnp.linspace(0.0, 1.0, n_rt, dtype=jnp.float32))
    freq = jnp.broadcast_to(jnp.concatenate([inv_da, inv_rt])[:, None], (n_da + n_rt, V7X_LANES))

    def selection(n, period):
        e = np.zeros((4 * n, 2 * V7X_LANES), np.float32)
        for f in range(n):
            for base in range(0, V7X_LANES, period):
                for part in (0, 2 * n):
                    e[part + f, base + f] = e[part + f, base + n + f] = 1.0
                    e[part + n + f, V7X_LANES + base + f] = -1.0
                    e[part + n + f, V7X_LANES + base + n + f] = 1.0
        return jnp.asarray(e, jnp.bfloat16)

    return freq, selection(n_da, DA_QK_DIM), selection(n_rt, RET_QK_DIM)


def _in_proj(x2, pos, n1w, w_in, qkw, ones_bd):
    T = x2.shape[0]
    rows = min(PROJ_ROWS, T)
    assert T % rows == 0 and rows % V7X_LANES == 0
    freq, e_da, e_rt = _rotary_constants()
    pos = pos.reshape(T // rows, 1, rows)
    row_spec = lambda width: pl.BlockSpec((rows, width), lambda i: (i, 0))
    const_spec = lambda shape: pl.BlockSpec(shape, lambda i: (0, 0))
    out_widths = SECTION_WIDTHS
    return pl.pallas_call(
        _in_proj_kernel,
        grid=(T // rows,),
        in_specs=[
            row_spec(D_MODEL),
            pl.BlockSpec((1, 1, rows), lambda i: (i, 0, 0)),
            const_spec((1, D_MODEL)),
            pl.BlockSpec((D_MODEL, IN_W), lambda i: (0, 0), pipeline_mode=pl.Buffered(1)),
            const_spec((2, DA_QK_W)),
            const_spec(freq.shape), const_spec(e_da.shape), const_spec(e_rt.shape),
            const_spec((DA_QK_W, DA_QK_W)),
        ],
        out_specs=[pl.BlockSpec((DA_QK_W, rows), lambda i: (0, i))] + [row_spec(w) for w in out_widths[1:]],
        out_shape=[jax.ShapeDtypeStruct((DA_QK_W, T), jnp.bfloat16)]
        + [jax.ShapeDtypeStruct((T, w), jnp.bfloat16) for w in out_widths[1:]],
        compiler_params=pltpu.CompilerParams(
            dimension_semantics=("parallel",), vmem_limit_bytes=V7X_VMEM_LIMIT_BYTES),
        name="in_proj",
    )(x2, pos, n1w, w_in, qkw, freq, e_da, e_rt, ones_bd)


def _diff_attn_kernel(q_ref, k_ref, v_ref, lam_ref, sw_ref, o_ref,
                      qst_ref, s_ref, p_ref, acc_ref, *, qb, kb):
    qi = pl.program_id(2)
    nq = 2 * qb
    n_strip = nq // V7X_LANES
    f32 = jnp.float32

    qt = q_ref[...]
    dim = lax.broadcasted_iota(jnp.int32, qt.shape, 0)
    qst_ref[:, :qb] = jnp.where(dim < DA_QK_DIM, qt, jnp.zeros_like(qt))
    qst_ref[:, qb:] = jnp.where(dim >= DA_QK_DIM, qt, jnp.zeros_like(qt))
    acc_ref[...] = jnp.zeros(acc_ref.shape, f32)
    p_ref[1] = jnp.zeros(p_ref.shape[1:], p_ref.dtype)

    def stage_a(t, slot):
        start = pl.multiple_of(t * kb, kb)
        s_ref[slot] = _dot(k_ref[pl.ds(start, kb), :], qst_ref[...])

    def stage_b(slot, m, l, key_offset):
        m_out, l_out, a_out = [], [], []
        for c in range(n_strip):
            cols = slice(c * V7X_LANES, (c + 1) * V7X_LANES)
            st = s_ref[slot, :, cols]
            if key_offset is not None:
                krow = lax.broadcasted_iota(jnp.int32, st.shape, 0) + key_offset
                qcol = lax.broadcasted_iota(jnp.int32, st.shape, 1) + (c * V7X_LANES) % qb
                st = jnp.where(krow <= qcol, st, MASK_VALUE)
            m_new = jnp.maximum(m[c], jnp.max(st, axis=0, keepdims=True))
            alpha = jnp.exp2(m[c] - m_new)
            p = jnp.exp2(st - m_new)
            l_out.append(alpha * l[c] + jnp.sum(p, axis=0, keepdims=True))
            p_ref[slot, :, cols] = _bf16(p)
            m_out.append(m_new)
            a_out.append(alpha)
        return tuple(m_out), tuple(l_out), tuple(a_out)

    def stage_c(t, slot, alpha):
        start = pl.multiple_of(jnp.maximum(t, 0) * kb, kb)
        vt = v_ref[pl.ds(start, kb), :]
        for c in range(0, n_strip, 2):
            cols = slice(c * V7X_LANES, (c + 2) * V7X_LANES)
            a = jnp.concatenate([alpha[c], alpha[c + 1]], axis=1)
            acc_ref[:, cols] = a * acc_ref[:, cols] + _dot_tn(vt, p_ref[slot, :, cols])

    def body(u, carry):
        m, l, alpha = carry
        t = 2 * u
        stage_c(t - 1, 1, alpha)
        stage_a(t + 1, 1)
        m, l, alpha = stage_b(0, m, l, None)
        stage_c(t, 0, alpha)
        stage_a(t + 2, 0)
        m, l, alpha = stage_b(1, m, l, None)
        return m, l, alpha

    row = lambda v: tuple(jnp.full((1, V7X_LANES), v, f32) for _ in range(n_strip))
    stage_a(0, 0)
    m, l, alpha = lax.fori_loop(0, qi, body, (row(MASK_VALUE), row(0.0), row(1.0)))
    t = 2 * qi
    stage_c(t - 1, 1, alpha)
    stage_a(t + 1, 1)
    m, l, alpha = stage_b(0, m, l, 0)
    stage_c(t, 0, alpha)
    m, l, alpha = stage_b(1, m, l, kb)
    stage_c(t + 1, 1, alpha)

    o = acc_ref[...] / jnp.concatenate(l, axis=1)
    o = o[:, :qb] - lam_ref[...] * o[:, qb:]
    o = o * lax.rsqrt(jnp.mean(o * o, axis=0, keepdims=True) + SUBLN_EPS)
    o_ref[...] = _bf16(o * jnp.concatenate([sw_ref[...]] * (qb // V7X_LANES), axis=1))


def _diff_attn(dq, dk, dv, lam, subln_w, B, S):
    blk = min(ATTN_BLOCK, S)
    kb = blk // 2
    assert S % blk == 0 and kb % V7X_LANES == 0
    nq = S // blk
    kernel = functools.partial(_diff_attn_kernel, qb=blk, kb=kb)
    return pl.pallas_call(
        kernel,
        grid=(B, DA_HEADS, nq),
        scratch_shapes=[
            pltpu.VMEM((DA_V_DIM, 2 * blk), jnp.bfloat16),
            pltpu.VMEM((2, kb, 2 * blk), jnp.float32),
            pltpu.VMEM((2, kb, 2 * blk), jnp.bfloat16),
            pltpu.VMEM((DA_V_DIM, 2 * blk), jnp.float32),
        ],
        in_specs=[
            pl.BlockSpec((2 * DA_QK_DIM, blk), lambda b, h, i: (h, b * nq + i)),
            pl.BlockSpec((S, V7X_LANES), lambda b, h, i: (b, h)),
            pl.BlockSpec((S, V7X_LANES), lambda b, h, i: (b, h)),
            pl.BlockSpec((1, 1), lambda b, h, i: (0, 0)),
            pl.BlockSpec((DA_V_DIM, V7X_LANES), lambda b, h, i: (0, 0)),
        ],
        out_specs=pl.BlockSpec((DA_V_DIM, blk), lambda b, h, i: (h, b * nq + i)),
        out_shape=jax.ShapeDtypeStruct((DA_V_W, B * S), jnp.bfloat16),
        compiler_params=pltpu.CompilerParams(
            dimension_semantics=("parallel", "parallel", "arbitrary"),
            vmem_limit_bytes=V7X_VMEM_LIMIT_BYTES),
        name="diff_attn",
    )(dq, dk, dv, lam, subln_w)


def _diff_attn_bounded_kernel(q_ref, k_ref, v_ref, lam_ref, sw_ref, o_ref,
                              qst_ref, p_ref, acc_ref, *, qb):
    qi = pl.program_id(2)
    nq = 2 * qb
    tile = V7X_MXU_WIDTH
    n_tile = nq // tile
    f32 = jnp.float32

    qt = q_ref[...]
    dim = lax.broadcasted_iota(jnp.int32, qt.shape, 0)
    qst_ref[:, :qb] = jnp.where(dim < DA_QK_DIM, qt, jnp.zeros_like(qt))
    qst_ref[:, qb:] = jnp.where(dim >= DA_QK_DIM, qt, jnp.zeros_like(qt))
    acc_ref[...] = jnp.zeros(acc_ref.shape, f32)

    def probs(start, ct, lsum, keys, diag):
        s = _dot(k_ref[pl.ds(start, keys), :], qst_ref[:, ct * tile:(ct + 1) * tile])
        if diag:
            krow = lax.broadcasted_iota(jnp.int32, s.shape, 0) - (keys - tile)
            qcol = lax.broadcasted_iota(jnp.int32, s.shape, 1)
            s = jnp.where(krow <= qcol, s, MASK_VALUE)
        p = jnp.exp2(s)
        p_ref[ct, :keys] = _bf16(p)
        return lsum + jnp.sum(p.reshape(keys // 8, 8, tile), axis=0)

    def accumulate(start, ct, keys):
        acc_ref[ct] += _dot_tn(v_ref[pl.ds(start, keys), :], p_ref[ct, :keys])

    def span(start, lsum, keys_of_tile, diag):
        lsum = list(lsum)
        for ct in range(n_tile + ATTN_PV_LAG):
            if ct < n_tile:
                lsum[ct] = probs(start, ct, lsum[ct], keys_of_tile[ct], diag)
            if ct >= ATTN_PV_LAG:
                accumulate(start, ct - ATTN_PV_LAG, keys_of_tile[ct - ATTN_PV_LAG])
        return tuple(lsum)

    all_tiles = tuple(range(n_tile))
    lsum = tuple(jnp.zeros((8, tile), f32) for _ in all_tiles)
    lsum = lax.fori_loop(
        0, qi, lambda u, ls: span(pl.multiple_of(u * qb, qb), ls, (qb,) * n_tile, False), lsum)
    lsum = span(pl.multiple_of(qi * qb, qb), lsum,
                tuple((ct * tile) % qb + tile for ct in all_tiles), True)

    o = jnp.concatenate(
        [acc_ref[ct] * (1.0 / jnp.sum(lsum[ct], axis=0, keepdims=True)) for ct in all_tiles], axis=1)
    o = o[:, :qb] - lam_ref[...] * o[:, qb:]
    o = o * lax.rsqrt(jnp.mean(o * o, axis=0, keepdims=True) + SUBLN_EPS)
    o_ref[...] = _bf16(o * jnp.concatenate([sw_ref[...]] * (qb // V7X_LANES), axis=1))


def _diff_attn_bounded(dq, dk, dv, lam, subln_w, B, S):
    blk = min(ATTN_BOUNDED_BLOCK, S)
    assert S % blk == 0 and blk % V7X_MXU_WIDTH == 0
    nq = S // blk
    n_tile = 2 * blk // V7X_MXU_WIDTH
    kernel = functools.partial(_diff_attn_bounded_kernel, qb=blk)
    return pl.pallas_call(
        kernel,
        grid=(B, DA_HEADS, nq),
        scratch_shapes=[
            pltpu.VMEM((DA_V_DIM, 2 * blk), jnp.bfloat16),
            pltpu.VMEM((n_tile, blk, V7X_MXU_WIDTH), jnp.bfloat16),
            pltpu.VMEM((n_tile, DA_V_DIM, V7X_MXU_WIDTH), jnp.float32),
        ],
        in_specs=[
            pl.BlockSpec((2 * DA_QK_DIM, blk), lambda b, h, i: (h, b * nq + i)),
            pl.BlockSpec((S, V7X_LANES), lambda b, h, i: (b, h)),
            pl.BlockSpec((S, V7X_LANES), lambda b, h, i: (b, h)),
            pl.BlockSpec((1, 1), lambda b, h, i: (0, 0)),
            pl.BlockSpec((DA_V_DIM, V7X_LANES), lambda b, h, i: (0, 0)),
        ],
        out_specs=pl.BlockSpec((DA_V_DIM, blk), lambda b, h, i: (h, b * nq + i)),
        out_shape=jax.ShapeDtypeStruct((DA_V_W, B * S), jnp.bfloat16),
        compiler_params=pltpu.CompilerParams(
            dimension_semantics=("parallel", "parallel", "arbitrary"),
            vmem_limit_bytes=V7X_VMEM_LIMIT_BYTES),
        name="diff_attn_bounded",
    )(dq, dk, dv, lam, subln_w)


def _retention_kernel(q_ref, k_ref, v_ref, g_ref, lg_ref, gnw_ref, o_ref, state_ref, decay_ref,
                      *, chunk, n_chunks):
    f32 = jnp.float32
    idx = lax.broadcasted_iota(jnp.int32, (chunk, 1), 0).astype(f32)

    @pl.when(pl.program_id(1) == 0)
    def _():
        state_ref[...] = jnp.zeros(state_ref.shape, f32)
        row = lax.broadcasted_iota(jnp.int32, (chunk, chunk), 0)
        col = lax.broadcasted_iota(jnp.int32, (chunk, chunk), 1)
        rel = (row - col).astype(f32)
        for h in range(RET_HEADS):
            lg = lg_ref[h, :, 0:1]
            decay_ref[h] = jnp.where(rel >= 0, jnp.exp(lg * jnp.maximum(rel, 0.0)), 0.0)

    def body(c, carry):
        start = pl.multiple_of(c * chunk, chunk)
        rows = pl.ds(start, chunk)
        heads = range(RET_HEADS)
        qk_cols = [slice(h * RET_QK_DIM, (h + 1) * RET_QK_DIM) for h in heads]
        v_cols = [slice(h * RET_V_DIM, (h + 1) * RET_V_DIM) for h in heads]
        lg = [lg_ref[h, :, 0:1] for h in heads]
        qc = [q_ref[rows, qk_cols[h]] for h in heads]
        kc = [k_ref[rows, qk_cols[h]] for h in heads]
        scores = [_bf16(_dot_nt(qc[h], kc[h]) * decay_ref[h]) for h in heads]
        cross = [_dot(qc[h], _bf16(state_ref[h])) * jnp.exp(lg[h] * (idx + 1.0)) for h in heads]
        o = [_dot(scores[h], v_ref[rows, v_cols[h]]) + cross[h] for h in heads]
        for h in heads:
            kz = _bf16(kc[h].astype(f32) * jnp.exp(lg[h] * (chunk - 1.0 - idx)))
            state_ref[h] = (jnp.exp(lg[h] * float(chunk)) * state_ref[h]
                            + _dot_tn(kz, v_ref[rows, v_cols[h]]))
        mu = [jnp.mean(o[h], axis=-1, keepdims=True) for h in heads]
        d = [o[h] - mu[h] for h in heads]
        var = [jnp.mean(d[h] * d[h], axis=-1, keepdims=True) for h in heads]
        for h in heads:
            on = d[h] * lax.rsqrt(var[h] + GN_EPS) * gnw_ref[:, v_cols[h]]
            g = g_ref[rows, v_cols[h]].astype(f32)
            o_ref[rows, v_cols[h]] = _bf16(g * jax.nn.sigmoid(g) * on)
        return carry

    lax.fori_loop(0, n_chunks, body, 0)


def _retention(rq, rk, rv, rg, lg_rows, gn_w, B, S):
    chunk = min(RET_CHUNK, S)
    sblk = min(RET_SEQ_BLOCK, S)
    assert S % sblk == 0 and sblk % chunk == 0
    nsb = S // sblk
    kernel = functools.partial(_retention_kernel, chunk=chunk, n_chunks=sblk // chunk)
    qk_spec = pl.BlockSpec((sblk, RET_QK_W), lambda b, s: (b * nsb + s, 0))
    v_spec = pl.BlockSpec((sblk, RET_V_W), lambda b, s: (b * nsb + s, 0))
    return pl.pallas_call(
        kernel,
        grid=(B, nsb),
        scratch_shapes=[
            pltpu.VMEM((RET_HEADS, RET_QK_DIM, RET_V_DIM), jnp.float32),
            pltpu.VMEM((RET_HEADS, chunk, chunk), jnp.float32),
        ],
        in_specs=[
            qk_spec, qk_spec, v_spec, v_spec,
            pl.BlockSpec((RET_HEADS, 1, V7X_LANES), lambda b, s: (0, 0, 0)),
            pl.BlockSpec((1, RET_V_W), lambda b, s: (0, 0)),
        ],
        out_specs=v_spec,
        out_shape=jax.ShapeDtypeStruct((B * S, RET_V_W), jnp.bfloat16),
        compiler_params=pltpu.CompilerParams(
            dimension_semantics=("parallel", "arbitrary"),
            vmem_limit_bytes=V7X_VMEM_LIMIT_BYTES),
        name="retention",
    )(rq, rk, rv, rg, lg_rows, gn_w)


def _out_mlp_kernel(x_ref, oa_ref, or_ref, gl_ref, wa_ref, wr_ref, wo_ref, n2w_ref, w1_ref, w2_ref,
                    o_ref):
    ya = _dot_tn(oa_ref[...], wa_ref[...])
    yr = _dot(or_ref[...], wr_ref[...])
    ga = jax.nn.sigmoid(gl_ref[:, :D_MODEL].astype(jnp.float32))
    gr = jax.nn.sigmoid(gl_ref[:, D_MODEL:].astype(jnp.float32))
    merged = _bf16(ga * ya + gr * yr)
    x1 = x_ref[...] + _dot(merged, wo_ref[...])
    h2 = x1 * lax.rsqrt(jnp.mean(x1 * x1, axis=-1, keepdims=True) + NORM_EPS) * n2w_ref[...]
    h2 = _bf16(h2)
    acc = x1
    for c in range(D_FF // MLP_FF_CHUNK):
        lo = c * MLP_FF_CHUNK
        hid = jnp.maximum(_dot(h2, w1_ref[:, lo:lo + MLP_FF_CHUNK]), 0.0)
        acc = acc + _dot(_bf16(hid * hid), w2_ref[lo:lo + MLP_FF_CHUNK, :])
    o_ref[...] = acc


def _out_mlp(x2, oa, orr, gl, wa, wr, wo, n2w, w1, w2):
    T = x2.shape[0]
    rows = min(MLP_ROWS, T)
    assert T % rows == 0
    row_spec = lambda width: pl.BlockSpec((rows, width), lambda i: (i, 0))
    weight_spec = lambda shape: pl.BlockSpec(shape, lambda i: (0, 0), pipeline_mode=pl.Buffered(1))
    return pl.pallas_call(
        _out_mlp_kernel,
        grid=(T // rows,),
        in_specs=[
            row_spec(D_MODEL), pl.BlockSpec((DA_V_W, rows), lambda i: (0, i)),
            row_spec(RET_V_W), row_spec(GATE_W),
            weight_spec((DA_V_W, D_MODEL)), weight_spec((RET_V_W, D_MODEL)),
            weight_spec((D_MODEL, D_MODEL)),
            pl.BlockSpec((1, D_MODEL), lambda i: (0, 0)),
            weight_spec((D_MODEL, D_FF)), weight_spec((D_FF, D_MODEL)),
        ],
        out_specs=row_spec(D_MODEL),
        out_shape=jax.ShapeDtypeStruct((T, D_MODEL), jnp.float32),
        compiler_params=pltpu.CompilerParams(
            dimension_semantics=("parallel",), vmem_limit_bytes=V7X_VMEM_LIMIT_BYTES),
        name="out_mlp",
    )(x2, oa, orr, gl, wa, wr, wo, n2w, w1, w2)


def kernel(x, positions, norm1_w, w_in, q_norm_w, k_norm_w, lambda_q1, lambda_k1, lambda_q2,
           lambda_k2, da_subln_w, w_da_branch, ret_gn_w, w_ret_branch, w_out, norm2_w, w_mlp_in,
           w_mlp_out):
    B, S, _ = x.shape
    T = B * S
    f32 = jnp.float32
    x2 = x.reshape(T, D_MODEL)

    q_scale = (DA_QK_DIM ** -0.5) * LOG2_E
    qkw = jnp.stack([jnp.tile(q_norm_w[0].astype(f32) * q_scale, DA_QK_W // DA_QK_DIM),
                     jnp.tile(k_norm_w[0].astype(f32), DA_QK_W // DA_QK_DIM)])
    blk = np.arange(DA_QK_W) // DA_QK_DIM
    ones_bd = jnp.asarray(blk[:, None] == blk[None, :], jnp.bfloat16)

    dq, dk, dv, rq, rk, rv, rg, gl = _in_proj(
        x2, positions.astype(f32), norm1_w[0].astype(f32).reshape(1, D_MODEL), _bf16(w_in[0]), qkw,
        ones_bd)

    lam = (jnp.exp(jnp.sum(lambda_q1[0].astype(f32) * lambda_k1[0].astype(f32)))
           - jnp.exp(jnp.sum(lambda_q2[0].astype(f32) * lambda_k2[0].astype(f32)))
           + LAMBDA_INIT).reshape(1, 1)
    subln = jnp.broadcast_to((da_subln_w[0].astype(f32) * (1.0 - LAMBDA_INIT))[:, None],
                             (DA_V_DIM, V7X_LANES))
    logit_bound = (DA_QK_DIM * jnp.max(jnp.abs(qkw[0])) * jnp.max(jnp.abs(qkw[1]))
                   * (1.0 + 2.0 ** -6))
    oa = lax.cond(logit_bound <= ATTN_BOUNDED_LOGIT,
                  functools.partial(_diff_attn_bounded, B=B, S=S),
                  functools.partial(_diff_attn, B=B, S=S),
                  dq, dk, dv, lam, subln)

    log_gamma = np.log1p(-np.exp2(-5.0 - np.arange(RET_HEADS, dtype=np.float64)))
    lg_rows = jnp.asarray(np.broadcast_to(log_gamma[:, None, None], (RET_HEADS, 1, V7X_LANES)), f32)
    orr = _retention(rq, rk, rv, rg, lg_rows, ret_gn_w[0].astype(f32).reshape(1, RET_V_W), B, S)

    out = _out_mlp(x2, oa, orr, gl, _bf16(w_da_branch[0]), _bf16(w_ret_branch[0]), _bf16(w_out[0]),
                   norm2_w[0].astype(f32).reshape(1, D_MODEL), _bf16(w_mlp_in[0]),
                   _bf16(w_mlp_out[0]))
    return out.reshape(B, S, D_MODEL)
```

```python
import functools
import math

import jax
import jax.numpy as jnp
import numpy as np
from jax import lax
from jax.experimental import pallas as pl
from jax.experimental.pallas import tpu as pltpu

D_MODEL = 1024
DA_HEADS = 4
DA_QK_DIM = 64
DA_V_DIM = 2 * DA_QK_DIM
DA_ROT_DIM = DA_QK_DIM // 4
ROPE_THETA = 500000.0
RET_HEADS = 4
RET_QK_DIM = 128
RET_V_DIM = 256
RET_THETA_BASE = 10000.0
D_FF = 4 * D_MODEL
NORM_EPS = 1e-6
SUBLN_EPS = 1e-5
GN_EPS = 1e-5
MASK_VALUE = -1e30
LAMBDA_INIT = 0.8 - 0.6 * math.exp(-0.3 * 0)

DA_QK_W = DA_HEADS * 2 * DA_QK_DIM
DA_V_W = DA_HEADS * DA_V_DIM
RET_QK_W = RET_HEADS * RET_QK_DIM
RET_V_W = RET_HEADS * RET_V_DIM
GATE_W = 2 * D_MODEL
SECTION_WIDTHS = (DA_QK_W, DA_QK_W, DA_V_W, RET_QK_W, RET_QK_W, RET_V_W, RET_V_W, GATE_W)
SECTION_STARTS = tuple(int(v) for v in np.cumsum((0,) + SECTION_WIDTHS[:-1]))
IN_W = sum(SECTION_WIDTHS)

V7X_LANES = 128
V7X_MXU_WIDTH = 256
V7X_VMEM_LIMIT_BYTES = 56 * 1024 * 1024

LOG2_E = math.log2(math.e)

PROJ_ROWS = 512
ATTN_BLOCK = 512
ATTN_BOUNDED_LOGIT = 60.0
ATTN_PV_LAG = 2
RET_CHUNK = 256
RET_SEQ_BLOCK = 1024
MLP_ROWS = 512
MLP_FF_CHUNK = 1024


def _bf16(t):
    return t.astype(jnp.bfloat16)


def _dot(a, b):
    return jnp.dot(a, b, preferred_element_type=jnp.float32)


def _dot_nt(a, b):
    return lax.dot_general(a, b, (((1,), (1,)), ((), ())), preferred_element_type=jnp.float32)


def _dot_tn(a, b):
    return lax.dot_general(a, b, (((0,), (0,)), ((), ())), preferred_element_type=jnp.float32)


def _split_bf16(t):
    hi = _bf16(t)
    lo = _bf16(t - hi.astype(jnp.float32))
    return jnp.concatenate([hi, lo], axis=0)


def _in_proj_kernel(x_ref, pos_ref, n1w_ref, w_ref, qkw_ref, freq_ref, eda_ref, ert_ref, ones_ref,
                    dq_ref, dk_ref, dv_ref, rq_ref, rk_ref, rv_ref, rg_ref, gl_ref):
    x = x_ref[...]
    h = x * lax.rsqrt(jnp.mean(x * x, axis=-1, keepdims=True) + NORM_EPS) * n1w_ref[...]
    h = _bf16(h)

    def section(idx):
        lo = SECTION_STARTS[idx]
        return _dot(h, w_ref[:, lo:lo + SECTION_WIDTHS[idx]])

    rows = x.shape[0]
    n_da = DA_ROT_DIM // 2
    ang = jnp.concatenate([freq_ref[...]] * (rows // V7X_LANES), axis=1) * pos_ref[0]
    cos_t, sin_t = jnp.cos(ang), jnp.sin(ang)
    tab_da = _dot_tn(_split_bf16(jnp.concatenate([cos_t[:n_da], sin_t[:n_da]], axis=0)),
                     eda_ref[...])
    tab_rt = _dot_tn(_split_bf16(jnp.concatenate([cos_t[n_da:], sin_t[n_da:]], axis=0)),
                     ert_ref[...])
    lane1 = lax.broadcasted_iota(jnp.int32, (1, V7X_LANES), 1)
    unrotated = jnp.where(lane1 % DA_QK_DIM >= DA_ROT_DIM, 1.0, 0.0)

    cos_rt = tab_rt[:, :V7X_LANES]
    sin_rt = tab_rt[:, V7X_LANES:]
    for idx, out_ref, scale in ((3, rq_ref, 1.0), (4, rk_ref, RET_QK_DIM ** -0.5)):
        t = section(idx)
        for hh in range(RET_HEADS):
            th = t[:, hh * RET_QK_DIM:(hh + 1) * RET_QK_DIM]
            rot = th * cos_rt + pltpu.roll(th, RET_QK_DIM // 2, axis=1) * sin_rt
            if scale != 1.0:
                rot = rot * scale
            out_ref[:, hh * RET_QK_DIM:(hh + 1) * RET_QK_DIM] = _bf16(rot)

    reps = DA_QK_W // V7X_LANES
    cos_da = jnp.concatenate([tab_da[:, :V7X_LANES] + unrotated] * reps, axis=1)
    sin_da = jnp.concatenate([tab_da[:, V7X_LANES:]] * reps, axis=1)
    lane = lax.broadcasted_iota(jnp.int32, (rows, DA_QK_W), 1)
    first_half = (lane % DA_QK_DIM) < (DA_ROT_DIM // 2)
    ones_bd = ones_ref[...]
    for idx, out_ref in ((0, dq_ref), (1, dk_ref)):
        t = section(idx)
        ms = _dot(_bf16(t * t), ones_bd) * (1.0 / DA_QK_DIM)
        t = t * lax.rsqrt(ms + NORM_EPS) * qkw_ref[idx:idx + 1, :]
        partner = jnp.where(first_half,
                            pltpu.roll(t, DA_QK_W - DA_ROT_DIM // 2, axis=1),
                            pltpu.roll(t, DA_ROT_DIM // 2, axis=1))
        rot = t * cos_da + partner * sin_da
        out_ref[...] = _bf16(rot.T if idx == 0 else rot)

    dv_ref[...] = _bf16(section(2))
    rv_ref[...] = _bf16(section(5))
    rg = section(6)
    rg_ref[...] = _bf16(rg * jax.nn.sigmoid(rg))
    gl_ref[...] = _bf16(section(7))


def _rotary_constants():
    n_da, n_rt = DA_ROT_DIM // 2, RET_QK_DIM // 2
    inv_da = ROPE_THETA ** (-jnp.arange(0, DA_ROT_DIM, 2, dtype=jnp.float32) / DA_ROT_DIM)
    inv_rt = 1.0 / (RET_THETA_BASE ** jnp.linspace(0.0, 1.0, n_rt, dtype=jnp.float32))
    freq = jnp.broadcast_to(jnp.concatenate([inv_da, inv_rt])[:, None], (n_da + n_rt, V7X_LANES))

    def selection(n, period):
        e = np.zeros((4 * n, 2 * V7X_LANES), np.float32)
        for f in range(n):
            for base in range(0, V7X_LANES, period):
                for part in (0, 2 * n):
                    e[part + f, base + f] = e[part + f, base + n + f] = 1.0
                    e[part + n + f, V7X_LANES + base + f] = -1.0
                    e[part + n + f, V7X_LANES + base + n + f] = 1.0
        return jnp.asarray(e, jnp.bfloat16)

    return freq, selection(n_da, DA_QK_DIM), selection(n_rt, RET_QK_DIM)


def _in_proj(x2, pos, n1w, w_in, qkw, ones_bd):
    T = x2.shape[0]
    rows = min(PROJ_ROWS, T)
    assert T % rows == 0 and rows % V7X_LANES == 0
    freq, e_da, e_rt = _rotary_constants()
    pos = pos.reshape(T // rows, 1, rows)
    row_spec = lambda width: pl.BlockSpec((rows, width), lambda i: (i, 0))
    const_spec = lambda shape: pl.BlockSpec(shape, lambda i: (0, 0))
    out_widths = SECTION_WIDTHS
    return pl.pallas_call(
        _in_proj_kernel,
        grid=(T // rows,),
        in_specs=[
            row_spec(D_MODEL),
            pl.BlockSpec((1, 1, rows), lambda i: (i, 0, 0)),
            const_spec((1, D_MODEL)),
            pl.BlockSpec((D_MODEL, IN_W), lambda i: (0, 0), pipeline_mode=pl.Buffered(1)),
            const_spec((2, DA_QK_W)),
            const_spec(freq.shape), const_spec(e_da.shape), const_spec(e_rt.shape),
            const_spec((DA_QK_W, DA_QK_W)),
        ],
        out_specs=[pl.BlockSpec((DA_QK_W, rows), lambda i: (0, i))] + [row_spec(w) for w in out_widths[1:]],
        out_shape=[jax.ShapeDtypeStruct((DA_QK_W, T), jnp.bfloat16)]
        + [jax.ShapeDtypeStruct((T, w), jnp.bfloat16) for w in out_widths[1:]],
        compiler_params=pltpu.CompilerParams(
            dimension_semantics=("parallel",), vmem_limit_bytes=V7X_VMEM_LIMIT_BYTES),
        name="in_proj",
    )(x2, pos, n1w, w_in, qkw, freq, e_da, e_rt, ones_bd)


def _diff_attn_kernel(q_ref, k_ref, v_ref, lam_ref, sw_ref, o_ref,
                      qst_ref, s_ref, p_ref, acc_ref, *, qb, kb):
    qi = pl.program_id(2)
    nq = 2 * qb
    n_strip = nq // V7X_LANES
    f32 = jnp.float32

    qt = q_ref[...]
    dim = lax.broadcasted_iota(jnp.int32, qt.shape, 0)
    qst_ref[:, :qb] = jnp.where(dim < DA_QK_DIM, qt, jnp.zeros_like(qt))
    qst_ref[:, qb:] = jnp.where(dim >= DA_QK_DIM, qt, jnp.zeros_like(qt))
    acc_ref[...] = jnp.zeros(acc_ref.shape, f32)
    p_ref[1] = jnp.zeros(p_ref.shape[1:], p_ref.dtype)

    def stage_a(t, slot):
        start = pl.multiple_of(t * kb, kb)
        s_ref[slot] = _dot(k_ref[pl.ds(start, kb), :], qst_ref[...])

    def stage_b(slot, m, l, key_offset):
        m_out, l_out, a_out = [], [], []
        for c in range(n_strip):
            cols = slice(c * V7X_LANES, (c + 1) * V7X_LANES)
            st = s_ref[slot, :, cols]
            if key_offset is not None:
                krow = lax.broadcasted_iota(jnp.int32, st.shape, 0) + key_offset
                qcol = lax.broadcasted_iota(jnp.int32, st.shape, 1) + (c * V7X_LANES) % qb
                st = jnp.where(krow <= qcol, st, MASK_VALUE)
            m_new = jnp.maximum(m[c], jnp.max(st, axis=0, keepdims=True))
            alpha = jnp.exp2(m[c] - m_new)
            p = jnp.exp2(st - m_new)
            l_out.append(alpha * l[c] + jnp.sum(p, axis=0, keepdims=True))
            p_ref[slot, :, cols] = _bf16(p)
            m_out.append(m_new)
            a_out.append(alpha)
        return tuple(m_out), tuple(l_out), tuple(a_out)

    def stage_c(t, slot, alpha):
        start = pl.multiple_of(jnp.maximum(t, 0) * kb, kb)
        vt = v_ref[pl.ds(start, kb), :]
        for c in range(0, n_strip, 2):
            cols = slice(c * V7X_LANES, (c + 2) * V7X_LANES)
            a = jnp.concatenate([alpha[c], alpha[c + 1]], axis=1)
            acc_ref[:, cols] = a * acc_ref[:, cols] + _dot_tn(vt, p_ref[slot, :, cols])

    def body(u, carry):
        m, l, alpha = carry
        t = 2 * u
        stage_c(t - 1, 1, alpha)
        stage_a(t + 1, 1)
        m, l, alpha = stage_b(0, m, l, None)
        stage_c(t, 0, alpha)
        stage_a(t + 2, 0)
        m, l, alpha = stage_b(1, m, l, None)
        return m, l, alpha

    row = lambda v: tuple(jnp.full((1, V7X_LANES), v, f32) for _ in range(n_strip))
    stage_a(0, 0)
    m, l, alpha = lax.fori_loop(0, qi, body, (row(MASK_VALUE), row(0.0), row(1.0)))
    t = 2 * qi
    stage_c(t - 1, 1, alpha)
    stage_a(t + 1, 1)
    m, l, alpha = stage_b(0, m, l, 0)
    stage_c(t, 0, alpha)
    m, l, alpha = stage_b(1, m, l, kb)
    stage_c(t + 1, 1, alpha)

    o = acc_ref[...] / jnp.concatenate(l, axis=1)
    o = o[:, :qb] - lam_ref[...] * o[:, qb:]
    o = o * lax.rsqrt(jnp.mean(o * o, axis=0, keepdims=True) + SUBLN_EPS)
    o_ref[...] = _bf16(o * jnp.concatenate([sw_ref[...]] * (qb // V7X_LANES), axis=1))


def _diff_attn(dq, dk, dv, lam, subln_w, B, S):
    blk = min(ATTN_BLOCK, S)
    kb = blk // 2
    assert S % blk == 0 and kb % V7X_LANES == 0
    nq = S // blk
    kernel = functools.partial(_diff_attn_kernel, qb=blk, kb=kb)
    return pl.pallas_call(
        kernel,
        grid=(B, DA_HEADS, nq),
        scratch_shapes=[
            pltpu.VMEM((DA_V_DIM, 2 * blk), jnp.bfloat16),
            pltpu.VMEM((2, kb, 2 * blk), jnp.float32),
            pltpu.VMEM((2, kb, 2 * blk), jnp.bfloat16),
            pltpu.VMEM((DA_V_DIM, 2 * blk), jnp.float32),
        ],
        in_specs=[
            pl.BlockSpec((2 * DA_QK_DIM, blk), lambda b, h, i: (h, b * nq + i)),
            pl.BlockSpec((S, V7X_LANES), lambda b, h, i: (b, h)),
            pl.BlockSpec((S, V7X_LANES), lambda b, h, i: (b, h)),
            pl.BlockSpec((1, 1), lambda b, h, i: (0, 0)),
            pl.BlockSpec((DA_V_DIM, V7X_LANES), lambda b, h, i: (0, 0)),
        ],
        out_specs=pl.BlockSpec((DA_V_DIM, blk), lambda b, h, i: (h, b * nq + i)),
        out_shape=jax.ShapeDtypeStruct((DA_V_W, B * S), jnp.bfloat16),
        compiler_params=pltpu.CompilerParams(
            dimension_semantics=("parallel", "parallel", "arbitrary"),
            vmem_limit_bytes=V7X_VMEM_LIMIT_BYTES),
        name="diff_attn",
    )(dq, dk, dv, lam, subln_w)


def _diff_attn_bounded_kernel(q_ref, k_ref, v_ref, lam_ref, sw_ref, o_ref, qst_ref, p_ref, *, seq):
    tile = V7X_MXU_WIDTH
    n_tok_tile = seq // tile

    qt = q_ref[...]
    dim = lax.broadcasted_iota(jnp.int32, qt.shape, 0)
    qst_ref[0] = jnp.where(dim < DA_QK_DIM, qt, jnp.zeros_like(qt))
    qst_ref[1] = jnp.where(dim >= DA_QK_DIM, qt, jnp.zeros_like(qt))
    sw = jnp.concatenate([sw_ref[...]] * (tile // V7X_LANES), axis=1)

    items = [(j, m) for j in range(n_tok_tile) for m in range(2)]
    slots = p_ref.shape[0]

    def probs(n):
        j, m = items[n]
        keys = (j + 1) * tile
        s = _dot(k_ref[:keys, :], qst_ref[m, :, j * tile:(j + 1) * tile])
        krow = lax.broadcasted_iota(jnp.int32, (tile, tile), 0)
        qcol = lax.broadcasted_iota(jnp.int32, (tile, tile), 1)
        p_diag = jnp.exp2(jnp.where(krow <= qcol, s[keys - tile:], MASK_VALUE))
        p_ref[n % slots, keys - tile:keys] = _bf16(p_diag)
        part = jnp.sum(p_diag.reshape(tile // 8, 8, tile), axis=0)
        if keys > tile:
            p_full = jnp.exp2(s[:keys - tile])
            p_ref[n % slots, :keys - tile] = _bf16(p_full)
            part = part + jnp.sum(p_full.reshape((keys - tile) // 8, 8, tile), axis=0)
        return jnp.sum(part, axis=0, keepdims=True)

    def weighted_values(n):
        j, _ = items[n]
        keys = (j + 1) * tile
        return _dot_tn(v_ref[:keys, :], p_ref[n % slots, :keys])

    lsum, out = {}, {}
    for n in range(len(items) + ATTN_PV_LAG):
        if n < len(items):
            lsum[n] = probs(n)
        done = n - ATTN_PV_LAG
        if done >= 0:
            out[done] = weighted_values(done) * (1.0 / lsum.pop(done))
            j, m = items[done]
            if m == 1:
                o = out.pop(done - 1) - lam_ref[...] * out.pop(done)
                o = o * lax.rsqrt(jnp.mean(o * o, axis=0, keepdims=True) + SUBLN_EPS)
                o_ref[:, j * tile:(j + 1) * tile] = _bf16(o * sw)


def _diff_attn_bounded(dq, dk, dv, lam, subln_w, B, S):
    assert S % V7X_MXU_WIDTH == 0
    kernel = functools.partial(_diff_attn_bounded_kernel, seq=S)
    return pl.pallas_call(
        kernel,
        grid=(B, DA_HEADS),
        scratch_shapes=[
            pltpu.VMEM((2, DA_V_DIM, S), jnp.bfloat16),
            pltpu.VMEM((2 * ATTN_PV_LAG + 2, S, V7X_MXU_WIDTH), jnp.bfloat16),
        ],
        in_specs=[
            pl.BlockSpec((2 * DA_QK_DIM, S), lambda b, h: (h, b)),
            pl.BlockSpec((S, V7X_LANES), lambda b, h: (b, h)),
            pl.BlockSpec((S, V7X_LANES), lambda b, h: (b, h)),
            pl.BlockSpec((1, 1), lambda b, h: (0, 0)),
            pl.BlockSpec((DA_V_DIM, V7X_LANES), lambda b, h: (0, 0)),
        ],
        out_specs=pl.BlockSpec((DA_V_DIM, S), lambda b, h: (h, b)),
        out_shape=jax.ShapeDtypeStruct((DA_V_W, B * S), jnp.bfloat16),
        compiler_params=pltpu.CompilerParams(
            dimension_semantics=("parallel", "parallel"),
            vmem_limit_bytes=V7X_VMEM_LIMIT_BYTES),
        name="diff_attn_bounded",
    )(dq, dk, dv, lam, subln_w)


def _retention_kernel(q_ref, k_ref, v_ref, g_ref, lg_ref, gnw_ref, o_ref, state_ref, decay_ref,
                      *, chunk, n_chunks):
    f32 = jnp.float32
    idx = lax.broadcasted_iota(jnp.int32, (chunk, 1), 0).astype(f32)

    @pl.when(pl.program_id(1) == 0)
    def _():
        state_ref[...] = jnp.zeros(state_ref.shape, f32)
        row = lax.broadcasted_iota(jnp.int32, (chunk, chunk), 0)
        col = lax.broadcasted_iota(jnp.int32, (chunk, chunk), 1)
        rel = (row - col).astype(f32)
        for h in range(RET_HEADS):
            lg = lg_ref[h, :, 0:1]
            decay_ref[h] = jnp.where(rel >= 0, jnp.exp(lg * jnp.maximum(rel, 0.0)), 0.0)

    def body(c, carry):
        start = pl.multiple_of(c * chunk, chunk)
        rows = pl.ds(start, chunk)
        heads = range(RET_HEADS)
        qk_cols = [slice(h * RET_QK_DIM, (h + 1) * RET_QK_DIM) for h in heads]
        v_cols = [slice(h * RET_V_DIM, (h + 1) * RET_V_DIM) for h in heads]
        lg = [lg_ref[h, :, 0:1] for h in heads]
        qc = [q_ref[rows, qk_cols[h]] for h in heads]
        kc = [k_ref[rows, qk_cols[h]] for h in heads]
        scores = [_bf16(_dot_nt(qc[h], kc[h]) * decay_ref[h]) for h in heads]
        cross = [_dot(qc[h], _bf16(state_ref[h])) * jnp.exp(lg[h] * (idx + 1.0)) for h in heads]
        o = [_dot(scores[h], v_ref[rows, v_cols[h]]) + cross[h] for h in heads]
        for h in heads:
            kz = _bf16(kc[h].astype(f32) * jnp.exp(lg[h] * (chunk - 1.0 - idx)))
            state_ref[h] = (jnp.exp(lg[h] * float(chunk)) * state_ref[h]
                            + _dot_tn(kz, v_ref[rows, v_cols[h]]))
        mu = [jnp.mean(o[h], axis=-1, keepdims=True) for h in heads]
        d = [o[h] - mu[h] for h in heads]
        var = [jnp.mean(d[h] * d[h], axis=-1, keepdims=True) for h in heads]
        for h in heads:
            on = d[h] * lax.rsqrt(var[h] + GN_EPS) * gnw_ref[:, v_cols[h]]
            o_ref[rows, v_cols[h]] = _bf16(g_ref[rows, v_cols[h]].astype(f32) * on)
        return carry

    lax.fori_loop(0, n_chunks, body, 0)


def _retention(rq, rk, rv, rg, lg_rows, gn_w, B, S):
    chunk = min(RET_CHUNK, S)
    sblk = min(RET_SEQ_BLOCK, S)
    assert S % sblk == 0 and sblk % chunk == 0
    nsb = S // sblk
    kernel = functools.partial(_retention_kernel, chunk=chunk, n_chunks=sblk // chunk)
    qk_spec = pl.BlockSpec((sblk, RET_QK_W), lambda b, s: (b * nsb + s, 0))
    v_spec = pl.BlockSpec((sblk, RET_V_W), lambda b, s: (b * nsb + s, 0))
    return pl.pallas_call(
        kernel,
        grid=(B, nsb),
        scratch_shapes=[
            pltpu.VMEM((RET_HEADS, RET_QK_DIM, RET_V_DIM), jnp.float32),
            pltpu.VMEM((RET_HEADS, chunk, chunk), jnp.float32),
        ],
        in_specs=[
            qk_spec, qk_spec, v_spec, v_spec,
            pl.BlockSpec((RET_HEADS, 1, V7X_LANES), lambda b, s: (0, 0, 0)),
            pl.BlockSpec((1, RET_V_W), lambda b, s: (0, 0)),
        ],
        out_specs=v_spec,
        out_shape=jax.ShapeDtypeStruct((B * S, RET_V_W), jnp.bfloat16),
        compiler_params=pltpu.CompilerParams(
            dimension_semantics=("parallel", "arbitrary"),
            vmem_limit_bytes=V7X_VMEM_LIMIT_BYTES),
        name="retention",
    )(rq, rk, rv, rg, lg_rows, gn_w)


def _out_mlp_kernel(x_ref, oa_ref, or_ref, gl_ref, wa_ref, wr_ref, wo_ref, n2w_ref, w1_ref, w2_ref,
                    o_ref):
    ya = _dot_tn(oa_ref[...], wa_ref[...])
    yr = _dot(or_ref[...], wr_ref[...])
    ga = jax.nn.sigmoid(gl_ref[:, :D_MODEL].astype(jnp.float32))
    gr = jax.nn.sigmoid(gl_ref[:, D_MODEL:].astype(jnp.float32))
    merged = _bf16(ga * ya + gr * yr)
    x1 = x_ref[...] + _dot(merged, wo_ref[...])
    h2 = x1 * lax.rsqrt(jnp.mean(x1 * x1, axis=-1, keepdims=True) + NORM_EPS) * n2w_ref[...]
    h2 = _bf16(h2)
    acc = x1
    for c in range(D_FF // MLP_FF_CHUNK):
        lo = c * MLP_FF_CHUNK
        hid = jnp.maximum(_dot(h2, w1_ref[:, lo:lo + MLP_FF_CHUNK]), 0.0)
        acc = acc + _dot(_bf16(hid * hid), w2_ref[lo:lo + MLP_FF_CHUNK, :])
    o_ref[...] = acc


def _out_mlp(x2, oa, orr, gl, wa, wr, wo, n2w, w1, w2):
    T = x2.shape[0]
    rows = min(MLP_ROWS, T)
    assert T % rows == 0
    row_spec = lambda width: pl.BlockSpec((rows, width), lambda i: (i, 0))
    weight_spec = lambda shape: pl.BlockSpec(shape, lambda i: (0, 0), pipeline_mode=pl.Buffered(1))
    return pl.pallas_call(
        _out_mlp_kernel,
        grid=(T // rows,),
        in_specs=[
            row_spec(D_MODEL), pl.BlockSpec((DA_V_W, rows), lambda i: (0, i)),
            row_spec(RET_V_W), row_spec(GATE_W),
            weight_spec((DA_V_W, D_MODEL)), weight_spec((RET_V_W, D_MODEL)),
            weight_spec((D_MODEL, D_MODEL)),
            pl.BlockSpec((1, D_MODEL), lambda i: (0, 0)),
            weight_spec((D_MODEL, D_FF)), weight_spec((D_FF, D_MODEL)),
        ],
        out_specs=row_spec(D_MODEL),
        out_shape=jax.ShapeDtypeStruct((T, D_MODEL), jnp.float32),
        compiler_params=pltpu.CompilerParams(
            dimension_semantics=("parallel",), vmem_limit_bytes=V7X_VMEM_LIMIT_BYTES),
        name="out_mlp",
    )(x2, oa, orr, gl, wa, wr, wo, n2w, w1, w2)


def kernel(x, positions, norm1_w, w_in, q_norm_w, k_norm_w, lambda_q1, lambda_k1, lambda_q2,
           lambda_k2, da_subln_w, w_da_branch, ret_gn_w, w_ret_branch, w_out, norm2_w, w_mlp_in,
           w_mlp_out):
    B, S, _ = x.shape
    T = B * S
    f32 = jnp.float32
    x2 = x.reshape(T, D_MODEL)

    q_scale = (DA_QK_DIM ** -0.5) * LOG2_E
    qkw = jnp.stack([jnp.tile(q_norm_w[0].astype(f32) * q_scale, DA_QK_W // DA_QK_DIM),
                     jnp.tile(k_norm_w[0].astype(f32), DA_QK_W // DA_QK_DIM)])
    blk = np.arange(DA_QK_W) // DA_QK_DIM
    ones_bd = jnp.asarray(blk[:, None] == blk[None, :], jnp.bfloat16)

    dq, dk, dv, rq, rk, rv, rg, gl = _in_proj(
        x2, positions.astype(f32), norm1_w[0].astype(f32).reshape(1, D_MODEL), _bf16(w_in[0]), qkw,
        ones_bd)

    lam = (jnp.exp(jnp.sum(lambda_q1[0].astype(f32) * lambda_k1[0].astype(f32)))
           - jnp.exp(jnp.sum(lambda_q2[0].astype(f32) * lambda_k2[0].astype(f32)))
           + LAMBDA_INIT).reshape(1, 1)
    subln = jnp.broadcast_to((da_subln_w[0].astype(f32) * (1.0 - LAMBDA_INIT))[:, None],
                             (DA_V_DIM, V7X_LANES))
    logit_bound = (DA_QK_DIM * jnp.max(jnp.abs(qkw[0])) * jnp.max(jnp.abs(qkw[1]))
                   * (1.0 + 2.0 ** -6))
    oa = lax.cond(logit_bound <= ATTN_BOUNDED_LOGIT,
                  functools.partial(_diff_attn_bounded, B=B, S=S),
                  functools.partial(_diff_attn, B=B, S=S),
                  dq, dk, dv, lam, subln)

    log_gamma = np.log1p(-np.exp2(-5.0 - np.arange(RET_HEADS, dtype=np.float64)))
    lg_rows = jnp.asarray(np.broadcast_to(log_gamma[:, None, None], (RET_HEADS, 1, V7X_LANES)), f32)
    orr = _retention(rq, rk, rv, rg, lg_rows, ret_gn_w[0].astype(f32).reshape(1, RET_V_W), B, S)

    out = _out_mlp(x2, oa, orr, gl, _bf16(w_da_branch[0]), _bf16(w_ret_branch[0]), _bf16(w_out[0]),
                   norm2_w[0].astype(f32).reshape(1, D_MODEL), _bf16(w_mlp_in[0]),
                   _bf16(w_mlp_out[0]))
    return out.reshape(B, S, D_MODEL)
```

```python
import functools
import math

import jax
import jax.numpy as jnp
import numpy as np
from jax import lax
from jax.experimental import pallas as pl
from jax.experimental.pallas import tpu as pltpu

D_MODEL = 1024
DA_HEADS = 4
DA_QK_DIM = 64
DA_V_DIM = 2 * DA_QK_DIM
DA_ROT_DIM = DA_QK_DIM // 4
ROPE_THETA = 500000.0
RET_HEADS = 4
RET_QK_DIM = 128
RET_V_DIM = 256
RET_THETA_BASE = 10000.0
D_FF = 4 * D_MODEL
NORM_EPS = 1e-6
SUBLN_EPS = 1e-5
GN_EPS = 1e-5
MASK_VALUE = -1e30
LAMBDA_INIT = 0.8 - 0.6 * math.exp(-0.3 * 0)

DA_QK_W = DA_HEADS * 2 * DA_QK_DIM
DA_V_W = DA_HEADS * DA_V_DIM
RET_QK_W = RET_HEADS * RET_QK_DIM
RET_V_W = RET_HEADS * RET_V_DIM
GATE_W = 2 * D_MODEL
SECTION_WIDTHS = (DA_QK_W, DA_QK_W, DA_V_W, RET_QK_W, RET_QK_W, RET_V_W, RET_V_W, GATE_W)
SECTION_STARTS = tuple(int(v) for v in np.cumsum((0,) + SECTION_WIDTHS[:-1]))
IN_W = sum(SECTION_WIDTHS)

V7X_LANES = 128
V7X_MXU_WIDTH = 256
V7X_VMEM_LIMIT_BYTES = 56 * 1024 * 1024

LOG2_E = math.log2(math.e)

PROJ_ROWS = 512
ATTN_BLOCK = 512
ATTN_BOUNDED_LOGIT = 60.0
ATTN_PV_LAG = 2
RET_CHUNK = 256
RET_SEQ_BLOCK = 1024
MLP_ROWS = 512
MLP_FF_CHUNK = 1024


def _bf16(t):
    return t.astype(jnp.bfloat16)


def _dot(a, b):
    return jnp.dot(a, b, preferred_element_type=jnp.float32)


def _dot_nt(a, b):
    return lax.dot_general(a, b, (((1,), (1,)), ((), ())), preferred_element_type=jnp.float32)


def _dot_tn(a, b):
    return lax.dot_general(a, b, (((0,), (0,)), ((), ())), preferred_element_type=jnp.float32)


def _split_bf16(t):
    hi = _bf16(t)
    lo = _bf16(t - hi.astype(jnp.float32))
    return jnp.concatenate([hi, lo], axis=0)


def _in_proj_kernel(x_ref, pos_ref, n1w_ref, w_ref, qkw_ref, freq_ref, eda_ref, ert_ref, ones_ref,
                    dq_ref, dk_ref, dv_ref, rq_ref, rk_ref, rv_ref, rg_ref, gl_ref):
    x = x_ref[...]
    h = x * lax.rsqrt(jnp.mean(x * x, axis=-1, keepdims=True) + NORM_EPS) * n1w_ref[...]
    h = _bf16(h)

    def section(idx):
        lo = SECTION_STARTS[idx]
        return _dot(h, w_ref[:, lo:lo + SECTION_WIDTHS[idx]])

    rows = x.shape[0]
    n_da = DA_ROT_DIM // 2
    ang = jnp.concatenate([freq_ref[...]] * (rows // V7X_LANES), axis=1) * pos_ref[0]
    cos_t, sin_t = jnp.cos(ang), jnp.sin(ang)
    tab_da = _dot_tn(_split_bf16(jnp.concatenate([cos_t[:n_da], sin_t[:n_da]], axis=0)),
                     eda_ref[...])
    tab_rt = _dot_tn(_split_bf16(jnp.concatenate([cos_t[n_da:], sin_t[n_da:]], axis=0)),
                     ert_ref[...])
    lane1 = lax.broadcasted_iota(jnp.int32, (1, V7X_LANES), 1)
    unrotated = jnp.where(lane1 % DA_QK_DIM >= DA_ROT_DIM, 1.0, 0.0)

    cos_rt = tab_rt[:, :V7X_LANES]
    sin_rt = tab_rt[:, V7X_LANES:]
    for idx, out_ref, scale in ((3, rq_ref, 1.0), (4, rk_ref, RET_QK_DIM ** -0.5)):
        t = section(idx)
        for hh in range(RET_HEADS):
            th = t[:, hh * RET_QK_DIM:(hh + 1) * RET_QK_DIM]
            rot = th * cos_rt + pltpu.roll(th, RET_QK_DIM // 2, axis=1) * sin_rt
            if scale != 1.0:
                rot = rot * scale
            out_ref[:, hh * RET_QK_DIM:(hh + 1) * RET_QK_DIM] = _bf16(rot)

    reps = DA_QK_W // V7X_LANES
    cos_da = jnp.concatenate([tab_da[:, :V7X_LANES] + unrotated] * reps, axis=1)
    sin_da = jnp.concatenate([tab_da[:, V7X_LANES:]] * reps, axis=1)
    lane = lax.broadcasted_iota(jnp.int32, (rows, DA_QK_W), 1)
    first_half = (lane % DA_QK_DIM) < (DA_ROT_DIM // 2)
    ones_bd = ones_ref[...]
    for idx, out_ref in ((0, dq_ref), (1, dk_ref)):
        t = section(idx)
        tt = _bf16(t * t)
        ms = jnp.concatenate(
            [_dot(tt[:, g:g + V7X_MXU_WIDTH], ones_bd) for g in range(0, DA_QK_W, V7X_MXU_WIDTH)],
            axis=1) * (1.0 / DA_QK_DIM)
        t = t * lax.rsqrt(ms + NORM_EPS) * qkw_ref[idx:idx + 1, :]
        partner = jnp.where(first_half,
                            pltpu.roll(t, DA_QK_W - DA_ROT_DIM // 2, axis=1),
                            pltpu.roll(t, DA_ROT_DIM // 2, axis=1))
        rot = t * cos_da + partner * sin_da
        out_ref[...] = _bf16(rot.T if idx == 0 else rot)

    dv_ref[...] = _bf16(section(2))
    rv_ref[...] = _bf16(section(5))
    rg = section(6)
    rg_ref[...] = _bf16(rg * jax.nn.sigmoid(rg))
    gl_ref[...] = _bf16(section(7))


def _rotary_constants():
    n_da, n_rt = DA_ROT_DIM // 2, RET_QK_DIM // 2
    inv_da = ROPE_THETA ** (-jnp.arange(0, DA_ROT_DIM, 2, dtype=jnp.float32) / DA_ROT_DIM)
    inv_rt = 1.0 / (RET_THETA_BASE ** jnp.linspace(0.0, 1.0, n_rt, dtype=jnp.float32))
    freq = jnp.broadcast_to(jnp.concatenate([inv_da, inv_rt])[:, None], (n_da + n_rt, V7X_LANES))

    def selection(n, period):
        e = np.zeros((4 * n, 2 * V7X_LANES), np.float32)
        for f in range(n):
            for base in range(0, V7X_LANES, period):
                for part in (0, 2 * n):
                    e[part + f, base + f] = e[part + f, base + n + f] = 1.0
                    e[part + n + f, V7X_LANES + base + f] = -1.0
                    e[part + n + f, V7X_LANES + base + n + f] = 1.0
        return jnp.asarray(e, jnp.bfloat16)

    return freq, selection(n_da, DA_QK_DIM), selection(n_rt, RET_QK_DIM)


def _in_proj(x2, pos, n1w, w_in, qkw, ones_bd):
    T = x2.shape[0]
    rows = min(PROJ_ROWS, T)
    assert T % rows == 0 and rows % V7X_LANES == 0
    freq, e_da, e_rt = _rotary_constants()
    pos = pos.reshape(T // rows, 1, rows)
    row_spec = lambda width: pl.BlockSpec((rows, width), lambda i: (i, 0))
    const_spec = lambda shape: pl.BlockSpec(shape, lambda i: (0, 0))
    out_widths = SECTION_WIDTHS
    return pl.pallas_call(
        _in_proj_kernel,
        grid=(T // rows,),
        in_specs=[
            row_spec(D_MODEL),
            pl.BlockSpec((1, 1, rows), lambda i: (i, 0, 0)),
            const_spec((1, D_MODEL)),
            pl.BlockSpec((D_MODEL, IN_W), lambda i: (0, 0), pipeline_mode=pl.Buffered(1)),
            const_spec((2, DA_QK_W)),
            const_spec(freq.shape), const_spec(e_da.shape), const_spec(e_rt.shape),
            const_spec((V7X_MXU_WIDTH, V7X_MXU_WIDTH)),
        ],
        out_specs=[pl.BlockSpec((DA_QK_W, rows), lambda i: (0, i))] + [row_spec(w) for w in out_widths[1:]],
        out_shape=[jax.ShapeDtypeStruct((DA_QK_W, T), jnp.bfloat16)]
        + [jax.ShapeDtypeStruct((T, w), jnp.bfloat16) for w in out_widths[1:]],
        compiler_params=pltpu.CompilerParams(
            dimension_semantics=("parallel",), vmem_limit_bytes=V7X_VMEM_LIMIT_BYTES),
        name="in_proj",
    )(x2, pos, n1w, w_in, qkw, freq, e_da, e_rt, ones_bd)


def _diff_attn_kernel(q_ref, k_ref, v_ref, lam_ref, sw_ref, o_ref,
                      qst_ref, s_ref, p_ref, acc_ref, *, qb, kb):
    qi = pl.program_id(2)
    nq = 2 * qb
    n_strip = nq // V7X_LANES
    f32 = jnp.float32

    qt = q_ref[...]
    dim = lax.broadcasted_iota(jnp.int32, qt.shape, 0)
    qst_ref[:, :qb] = jnp.where(dim < DA_QK_DIM, qt, jnp.zeros_like(qt))
    qst_ref[:, qb:] = jnp.where(dim >= DA_QK_DIM, qt, jnp.zeros_like(qt))
    acc_ref[...] = jnp.zeros(acc_ref.shape, f32)
    p_ref[1] = jnp.zeros(p_ref.shape[1:], p_ref.dtype)

    def stage_a(t, slot):
        start = pl.multiple_of(t * kb, kb)
        s_ref[slot] = _dot(k_ref[pl.ds(start, kb), :], qst_ref[...])

    def stage_b(slot, m, l, key_offset):
        m_out, l_out, a_out = [], [], []
        for c in range(n_strip):
            cols = slice(c * V7X_LANES, (c + 1) * V7X_LANES)
            st = s_ref[slot, :, cols]
            if key_offset is not None:
                krow = lax.broadcasted_iota(jnp.int32, st.shape, 0) + key_offset
                qcol = lax.broadcasted_iota(jnp.int32, st.shape, 1) + (c * V7X_LANES) % qb
                st = jnp.where(krow <= qcol, st, MASK_VALUE)
            m_new = jnp.maximum(m[c], jnp.max(st, axis=0, keepdims=True))
            alpha = jnp.exp2(m[c] - m_new)
            p = jnp.exp2(st - m_new)
            l_out.append(alpha * l[c] + jnp.sum(p, axis=0, keepdims=True))
            p_ref[slot, :, cols] = _bf16(p)
            m_out.append(m_new)
            a_out.append(alpha)
        return tuple(m_out), tuple(l_out), tuple(a_out)

    def stage_c(t, slot, alpha):
        start = pl.multiple_of(jnp.maximum(t, 0) * kb, kb)
        vt = v_ref[pl.ds(start, kb), :]
        for c in range(0, n_strip, 2):
            cols = slice(c * V7X_LANES, (c + 2) * V7X_LANES)
            a = jnp.concatenate([alpha[c], alpha[c + 1]], axis=1)
            acc_ref[:, cols] = a * acc_ref[:, cols] + _dot_tn(vt, p_ref[slot, :, cols])

    def body(u, carry):
        m, l, alpha = carry
        t = 2 * u
        stage_c(t - 1, 1, alpha)
        stage_a(t + 1, 1)
        m, l, alpha = stage_b(0, m, l, None)
        stage_c(t, 0, alpha)
        stage_a(t + 2, 0)
        m, l, alpha = stage_b(1, m, l, None)
        return m, l, alpha

    row = lambda v: tuple(jnp.full((1, V7X_LANES), v, f32) for _ in range(n_strip))
    stage_a(0, 0)
    m, l, alpha = lax.fori_loop(0, qi, body, (row(MASK_VALUE), row(0.0), row(1.0)))
    t = 2 * qi
    stage_c(t - 1, 1, alpha)
    stage_a(t + 1, 1)
    m, l, alpha = stage_b(0, m, l, 0)
    stage_c(t, 0, alpha)
    m, l, alpha = stage_b(1, m, l, kb)
    stage_c(t + 1, 1, alpha)

    o = acc_ref[...] / jnp.concatenate(l, axis=1)
    o = o[:, :qb] - lam_ref[...] * o[:, qb:]
    o = o * lax.rsqrt(jnp.mean(o * o, axis=0, keepdims=True) + SUBLN_EPS)
    o_ref[...] = _bf16(o * jnp.concatenate([sw_ref[...]] * (qb // V7X_LANES), axis=1))


def _diff_attn(dq, dk, dv, lam, subln_w, B, S):
    blk = min(ATTN_BLOCK, S)
    kb = blk // 2
    assert S % blk == 0 and kb % V7X_LANES == 0
    nq = S // blk
    kernel = functools.partial(_diff_attn_kernel, qb=blk, kb=kb)
    return pl.pallas_call(
        kernel,
        grid=(B, DA_HEADS, nq),
        scratch_shapes=[
            pltpu.VMEM((DA_V_DIM, 2 * blk), jnp.bfloat16),
            pltpu.VMEM((2, kb, 2 * blk), jnp.float32),
            pltpu.VMEM((2, kb, 2 * blk), jnp.bfloat16),
            pltpu.VMEM((DA_V_DIM, 2 * blk), jnp.float32),
        ],
        in_specs=[
            pl.BlockSpec((2 * DA_QK_DIM, blk), lambda b, h, i: (h, b * nq + i)),
            pl.BlockSpec((S, V7X_LANES), lambda b, h, i: (b, h)),
            pl.BlockSpec((S, V7X_LANES), lambda b, h, i: (b, h)),
            pl.BlockSpec((1, 1), lambda b, h, i: (0, 0)),
            pl.BlockSpec((DA_V_DIM, V7X_LANES), lambda b, h, i: (0, 0)),
        ],
        out_specs=pl.BlockSpec((DA_V_DIM, blk), lambda b, h, i: (h, b * nq + i)),
        out_shape=jax.ShapeDtypeStruct((DA_V_W, B * S), jnp.bfloat16),
        compiler_params=pltpu.CompilerParams(
            dimension_semantics=("parallel", "parallel", "arbitrary"),
            vmem_limit_bytes=V7X_VMEM_LIMIT_BYTES),
        name="diff_attn",
    )(dq, dk, dv, lam, subln_w)


def _diff_attn_bounded_kernel(q_ref, k_ref, v_ref, lam_ref, sw_ref, o_ref, qst_ref, p_ref, *, seq):
    tile = V7X_MXU_WIDTH
    n_tok_tile = seq // tile

    qt = q_ref[...]
    dim = lax.broadcasted_iota(jnp.int32, qt.shape, 0)
    qst_ref[0] = jnp.where(dim < DA_QK_DIM, qt, jnp.zeros_like(qt))
    qst_ref[1] = jnp.where(dim >= DA_QK_DIM, qt, jnp.zeros_like(qt))
    sw = jnp.concatenate([sw_ref[...]] * (tile // V7X_LANES), axis=1)

    items = [(j, m) for j in reversed(range(n_tok_tile)) for m in range(2)]
    slots = p_ref.shape[0]

    def probs(n):
        j, m = items[n]
        keys = (j + 1) * tile
        s = _dot(k_ref[:keys, :], qst_ref[m, :, j * tile:(j + 1) * tile])
        krow = lax.broadcasted_iota(jnp.int32, (tile, tile), 0)
        qcol = lax.broadcasted_iota(jnp.int32, (tile, tile), 1)
        p_diag = jnp.exp2(jnp.where(krow <= qcol, s[keys - tile:], MASK_VALUE))
        p_ref[n % slots, keys - tile:keys] = _bf16(p_diag)
        part = jnp.sum(p_diag.reshape(tile // 8, 8, tile), axis=0)
        if keys > tile:
            p_full = jnp.exp2(s[:keys - tile])
            p_ref[n % slots, :keys - tile] = _bf16(p_full)
            part = part + jnp.sum(p_full.reshape((keys - tile) // 8, 8, tile), axis=0)
        return jnp.sum(part, axis=0, keepdims=True)

    def weighted_values(n):
        j, _ = items[n]
        keys = (j + 1) * tile
        return _dot_tn(v_ref[:keys, :], p_ref[n % slots, :keys])

    lsum, out = {}, {}
    for n in range(len(items) + ATTN_PV_LAG):
        if n < len(items):
            lsum[n] = probs(n)
        done = n - ATTN_PV_LAG
        if done >= 0:
            out[done] = weighted_values(done) * (1.0 / lsum.pop(done))
            j, m = items[done]
            if m == 1:
                o = out.pop(done - 1) - lam_ref[...] * out.pop(done)
                o = o * lax.rsqrt(jnp.mean(o * o, axis=0, keepdims=True) + SUBLN_EPS)
                o_ref[:, j * tile:(j + 1) * tile] = _bf16(o * sw)


def _diff_attn_bounded(dq, dk, dv, lam, subln_w, B, S):
    assert S % V7X_MXU_WIDTH == 0
    kernel = functools.partial(_diff_attn_bounded_kernel, seq=S)
    return pl.pallas_call(
        kernel,
        grid=(B, DA_HEADS),
        scratch_shapes=[
            pltpu.VMEM((2, DA_V_DIM, S), jnp.bfloat16),
            pltpu.VMEM((2 * ATTN_PV_LAG + 2, S, V7X_MXU_WIDTH), jnp.bfloat16),
        ],
        in_specs=[
            pl.BlockSpec((2 * DA_QK_DIM, S), lambda b, h: (h, b)),
            pl.BlockSpec((S, V7X_LANES), lambda b, h: (b, h)),
            pl.BlockSpec((S, V7X_LANES), lambda b, h: (b, h)),
            pl.BlockSpec((1, 1), lambda b, h: (0, 0)),
            pl.BlockSpec((DA_V_DIM, V7X_LANES), lambda b, h: (0, 0)),
        ],
        out_specs=pl.BlockSpec((DA_V_DIM, S), lambda b, h: (h, b)),
        out_shape=jax.ShapeDtypeStruct((DA_V_W, B * S), jnp.bfloat16),
        compiler_params=pltpu.CompilerParams(
            dimension_semantics=("parallel", "parallel"),
            vmem_limit_bytes=V7X_VMEM_LIMIT_BYTES),
        name="diff_attn_bounded",
    )(dq, dk, dv, lam, subln_w)


def _retention_kernel(q_ref, k_ref, v_ref, g_ref, lg_ref, gnw_ref, o_ref, state_ref, decay_ref,
                      *, chunk, n_chunks):
    f32 = jnp.float32
    idx = lax.broadcasted_iota(jnp.int32, (chunk, 1), 0).astype(f32)

    @pl.when(pl.program_id(1) == 0)
    def _():
        state_ref[...] = jnp.zeros(state_ref.shape, f32)
        row = lax.broadcasted_iota(jnp.int32, (chunk, chunk), 0)
        col = lax.broadcasted_iota(jnp.int32, (chunk, chunk), 1)
        rel = (row - col).astype(f32)
        for h in range(RET_HEADS):
            lg = lg_ref[h, :, 0:1]
            decay_ref[h] = jnp.where(rel >= 0, jnp.exp(lg * jnp.maximum(rel, 0.0)), 0.0)

    def body(c, carry):
        start = pl.multiple_of(c * chunk, chunk)
        rows = pl.ds(start, chunk)
        heads = range(RET_HEADS)
        qk_cols = [slice(h * RET_QK_DIM, (h + 1) * RET_QK_DIM) for h in heads]
        v_cols = [slice(h * RET_V_DIM, (h + 1) * RET_V_DIM) for h in heads]
        lg = [lg_ref[h, :, 0:1] for h in heads]
        qc = [q_ref[rows, qk_cols[h]] for h in heads]
        kc = [k_ref[rows, qk_cols[h]] for h in heads]
        scores = [_bf16(_dot_nt(qc[h], kc[h]) * decay_ref[h]) for h in heads]
        cross = [_dot(qc[h], _bf16(state_ref[h])) * jnp.exp(lg[h] * (idx + 1.0)) for h in heads]
        o = [_dot(scores[h], v_ref[rows, v_cols[h]]) + cross[h] for h in heads]
        for h in heads:
            kz = _bf16(kc[h].astype(f32) * jnp.exp(lg[h] * (chunk - 1.0 - idx)))
            state_ref[h] = (jnp.exp(lg[h] * float(chunk)) * state_ref[h]
                            + _dot_tn(kz, v_ref[rows, v_cols[h]]))
        mu = [jnp.mean(o[h], axis=-1, keepdims=True) for h in heads]
        d = [o[h] - mu[h] for h in heads]
        var = [jnp.mean(d[h] * d[h], axis=-1, keepdims=True) for h in heads]
        for h in heads:
            on = d[h] * lax.rsqrt(var[h] + GN_EPS) * gnw_ref[:, v_cols[h]]
            o_ref[rows, v_cols[h]] = _bf16(g_ref[rows, v_cols[h]].astype(f32) * on)
        return carry

    lax.fori_loop(0, n_chunks, body, 0)


def _retention(rq, rk, rv, rg, lg_rows, gn_w, B, S):
    chunk = min(RET_CHUNK, S)
    sblk = min(RET_SEQ_BLOCK, S)
    assert S % sblk == 0 and sblk % chunk == 0
    nsb = S // sblk
    kernel = functools.partial(_retention_kernel, chunk=chunk, n_chunks=sblk // chunk)
    qk_spec = pl.BlockSpec((sblk, RET_QK_W), lambda b, s: (b * nsb + s, 0))
    v_spec = pl.BlockSpec((sblk, RET_V_W), lambda b, s: (b * nsb + s, 0))
    return pl.pallas_call(
        kernel,
        grid=(B, nsb),
        scratch_shapes=[
            pltpu.VMEM((RET_HEADS, RET_QK_DIM, RET_V_DIM), jnp.float32),
            pltpu.VMEM((RET_HEADS, chunk, chunk), jnp.float32),
        ],
        in_specs=[
            qk_spec, qk_spec, v_spec, v_spec,
            pl.BlockSpec((RET_HEADS, 1, V7X_LANES), lambda b, s: (0, 0, 0)),
            pl.BlockSpec((1, RET_V_W), lambda b, s: (0, 0)),
        ],
        out_specs=v_spec,
        out_shape=jax.ShapeDtypeStruct((B * S, RET_V_W), jnp.bfloat16),
        compiler_params=pltpu.CompilerParams(
            dimension_semantics=("parallel", "arbitrary"),
            vmem_limit_bytes=V7X_VMEM_LIMIT_BYTES),
        name="retention",
    )(rq, rk, rv, rg, lg_rows, gn_w)


def _out_mlp_kernel(x_ref, oa_ref, or_ref, gl_ref, wa_ref, wr_ref, wo_ref, n2w_ref, w1_ref, w2_ref,
                    o_ref):
    ya = _dot_tn(oa_ref[...], wa_ref[...])
    yr = _dot(or_ref[...], wr_ref[...])
    ga = jax.nn.sigmoid(gl_ref[:, :D_MODEL].astype(jnp.float32))
    gr = jax.nn.sigmoid(gl_ref[:, D_MODEL:].astype(jnp.float32))
    merged = _bf16(ga * ya + gr * yr)
    x1 = x_ref[...] + _dot(merged, wo_ref[...])
    h2 = x1 * lax.rsqrt(jnp.mean(x1 * x1, axis=-1, keepdims=True) + NORM_EPS) * n2w_ref[...]
    h2 = _bf16(h2)
    acc = x1
    for c in range(D_FF // MLP_FF_CHUNK):
        lo = c * MLP_FF_CHUNK
        hid = jnp.maximum(_dot(h2, w1_ref[:, lo:lo + MLP_FF_CHUNK]), 0.0)
        acc = acc + _dot(_bf16(hid * hid), w2_ref[lo:lo + MLP_FF_CHUNK, :])
    o_ref[...] = acc


def _out_mlp(x2, oa, orr, gl, wa, wr, wo, n2w, w1, w2):
    T = x2.shape[0]
    rows = min(MLP_ROWS, T)
    assert T % rows == 0
    row_spec = lambda width: pl.BlockSpec((rows, width), lambda i: (i, 0))
    weight_spec = lambda shape: pl.BlockSpec(shape, lambda i: (0, 0), pipeline_mode=pl.Buffered(1))
    return pl.pallas_call(
        _out_mlp_kernel,
        grid=(T // rows,),
        in_specs=[
            row_spec(D_MODEL), pl.BlockSpec((DA_V_W, rows), lambda i: (0, i)),
            row_spec(RET_V_W), row_spec(GATE_W),
            weight_spec((DA_V_W, D_MODEL)), weight_spec((RET_V_W, D_MODEL)),
            weight_spec((D_MODEL, D_MODEL)),
            pl.BlockSpec((1, D_MODEL), lambda i: (0, 0)),
            weight_spec((D_MODEL, D_FF)), weight_spec((D_FF, D_MODEL)),
        ],
        out_specs=row_spec(D_MODEL),
        out_shape=jax.ShapeDtypeStruct((T, D_MODEL), jnp.float32),
        compiler_params=pltpu.CompilerParams(
            dimension_semantics=("parallel",), vmem_limit_bytes=V7X_VMEM_LIMIT_BYTES),
        name="out_mlp",
    )(x2, oa, orr, gl, wa, wr, wo, n2w, w1, w2)


def kernel(x, positions, norm1_w, w_in, q_norm_w, k_norm_w, lambda_q1, lambda_k1, lambda_q2,
           lambda_k2, da_subln_w, w_da_branch, ret_gn_w, w_ret_branch, w_out, norm2_w, w_mlp_in,
           w_mlp_out):
    B, S, _ = x.shape
    T = B * S
    f32 = jnp.float32
    x2 = x.reshape(T, D_MODEL)

    q_scale = (DA_QK_DIM ** -0.5) * LOG2_E
    qkw = jnp.stack([jnp.tile(q_norm_w[0].astype(f32) * q_scale, DA_QK_W // DA_QK_DIM),
                     jnp.tile(k_norm_w[0].astype(f32), DA_QK_W // DA_QK_DIM)])
    blk = np.arange(V7X_MXU_WIDTH) // DA_QK_DIM
    ones_bd = jnp.asarray(blk[:, None] == blk[None, :], jnp.bfloat16)

    dq, dk, dv, rq, rk, rv, rg, gl = _in_proj(
        x2, positions.astype(f32), norm1_w[0].astype(f32).reshape(1, D_MODEL), _bf16(w_in[0]), qkw,
        ones_bd)

    lam = (jnp.exp(jnp.sum(lambda_q1[0].astype(f32) * lambda_k1[0].astype(f32)))
           - jnp.exp(jnp.sum(lambda_q2[0].astype(f32) * lambda_k2[0].astype(f32)))
           + LAMBDA_INIT).reshape(1, 1)
    subln = jnp.broadcast_to((da_subln_w[0].astype(f32) * (1.0 - LAMBDA_INIT))[:, None],
                             (DA_V_DIM, V7X_LANES))
    logit_bound = (DA_QK_DIM * jnp.max(jnp.abs(qkw[0])) * jnp.max(jnp.abs(qkw[1]))
                   * (1.0 + 2.0 ** -6))
    oa = lax.cond(logit_bound <= ATTN_BOUNDED_LOGIT,
                  functools.partial(_diff_attn_bounded, B=B, S=S),
                  functools.partial(_diff_attn, B=B, S=S),
                  dq, dk, dv, lam, subln)

    log_gamma = np.log1p(-np.exp2(-5.0 - np.arange(RET_HEADS, dtype=np.float64)))
    lg_rows = jnp.asarray(np.broadcast_to(log_gamma[:, None, None], (RET_HEADS, 1, V7X_LANES)), f32)
    orr = _retention(rq, rk, rv, rg, lg_rows, ret_gn_w[0].astype(f32).reshape(1, RET_V_W), B, S)

    out = _out_mlp(x2, oa, orr, gl, _bf16(w_da_branch[0]), _bf16(w_ret_branch[0]), _bf16(w_out[0]),
                   norm2_w[0].astype(f32).reshape(1, D_MODEL), _bf16(w_mlp_in[0]),
                   _bf16(w_mlp_out[0]))
    return out.reshape(B, S, D_MODEL)
```

```python
import functools
import math

import jax
import jax.numpy as jnp
import numpy as np
from jax import lax
from jax.experimental import pallas as pl
from jax.experimental.pallas import tpu as pltpu

D_MODEL = 1024
DA_HEADS = 4
DA_QK_DIM = 64
DA_V_DIM = 2 * DA_QK_DIM
DA_ROT_DIM = DA_QK_DIM // 4
ROPE_THETA = 500000.0
RET_HEADS = 4
RET_QK_DIM = 128
RET_V_DIM = 256
RET_THETA_BASE = 10000.0
D_FF = 4 * D_MODEL
NORM_EPS = 1e-6
SUBLN_EPS = 1e-5
GN_EPS = 1e-5
MASK_VALUE = -1e30
LAMBDA_INIT = 0.8 - 0.6 * math.exp(-0.3 * 0)

DA_QK_W = DA_HEADS * 2 * DA_QK_DIM
DA_V_W = DA_HEADS * DA_V_DIM
RET_QK_W = RET_HEADS * RET_QK_DIM
RET_V_W = RET_HEADS * RET_V_DIM
GATE_W = 2 * D_MODEL
SECTION_WIDTHS = (DA_QK_W, DA_QK_W, DA_V_W, RET_QK_W, RET_QK_W, RET_V_W, RET_V_W, GATE_W)
SECTION_STARTS = tuple(int(v) for v in np.cumsum((0,) + SECTION_WIDTHS[:-1]))
IN_W = sum(SECTION_WIDTHS)

V7X_LANES = 128
V7X_MXU_WIDTH = 256
V7X_VMEM_LIMIT_BYTES = 56 * 1024 * 1024

LOG2_E = math.log2(math.e)

PROJ_ROWS = 512
ATTN_BLOCK = 512
ATTN_BOUNDED_LOGIT = 60.0
ATTN_PV_LAG = 2
RET_CHUNK = 256
RET_SEQ_BLOCK = 1024
MLP_ROWS = 512
MLP_FF_CHUNK = 1024


def _bf16(t):
    return t.astype(jnp.bfloat16)


def _dot(a, b):
    return jnp.dot(a, b, preferred_element_type=jnp.float32)


def _dot_nt(a, b):
    return lax.dot_general(a, b, (((1,), (1,)), ((), ())), preferred_element_type=jnp.float32)


def _dot_tn(a, b):
    return lax.dot_general(a, b, (((0,), (0,)), ((), ())), preferred_element_type=jnp.float32)


def _split_bf16(t):
    hi = _bf16(t)
    lo = _bf16(t - hi.astype(jnp.float32))
    return jnp.concatenate([hi, lo], axis=0)


def _in_proj_kernel(x_ref, pos_ref, n1w_ref, w_ref, qkw_ref, freq_ref, eda_ref, ert_ref, ones_ref,
                    dq_ref, dk_ref, dv_ref, rq_ref, rk_ref, rv_ref, rg_ref, gl_ref):
    x = x_ref[...]
    h = x * lax.rsqrt(jnp.mean(x * x, axis=-1, keepdims=True) + NORM_EPS) * n1w_ref[...]
    h = _bf16(h)

    def section(idx):
        lo = SECTION_STARTS[idx]
        return _dot(h, w_ref[:, lo:lo + SECTION_WIDTHS[idx]])

    rows = x.shape[0]
    n_da = DA_ROT_DIM // 2
    ang = jnp.concatenate([freq_ref[...]] * (rows // V7X_LANES), axis=1) * pos_ref[0]
    cos_t, sin_t = jnp.cos(ang), jnp.sin(ang)
    tab_da = _dot_tn(_split_bf16(jnp.concatenate([cos_t[:n_da], sin_t[:n_da]], axis=0)),
                     eda_ref[...])
    tab_rt = _dot_tn(_split_bf16(jnp.concatenate([cos_t[n_da:], sin_t[n_da:]], axis=0)),
                     ert_ref[...])
    lane1 = lax.broadcasted_iota(jnp.int32, (1, V7X_LANES), 1)
    unrotated = jnp.where(lane1 % DA_QK_DIM >= DA_ROT_DIM, 1.0, 0.0)

    cos_rt = tab_rt[:, :V7X_LANES]
    sin_rt = tab_rt[:, V7X_LANES:]
    for idx, out_ref, scale in ((3, rq_ref, 1.0), (4, rk_ref, RET_QK_DIM ** -0.5)):
        t = section(idx)
        for hh in range(RET_HEADS):
            th = t[:, hh * RET_QK_DIM:(hh + 1) * RET_QK_DIM]
            rot = th * cos_rt + pltpu.roll(th, RET_QK_DIM // 2, axis=1) * sin_rt
            if scale != 1.0:
                rot = rot * scale
            out_ref[:, hh * RET_QK_DIM:(hh + 1) * RET_QK_DIM] = _bf16(rot)

    reps = DA_QK_W // V7X_LANES
    cos_da = jnp.concatenate([tab_da[:, :V7X_LANES] + unrotated] * reps, axis=1)
    sin_da = jnp.concatenate([tab_da[:, V7X_LANES:]] * reps, axis=1)
    lane = lax.broadcasted_iota(jnp.int32, (rows, DA_QK_W), 1)
    first_half = (lane % DA_QK_DIM) < (DA_ROT_DIM // 2)
    ones_bd = ones_ref[...]
    for idx, out_ref in ((0, dq_ref), (1, dk_ref)):
        t = section(idx)
        tt = _bf16(t * t)
        ms = jnp.concatenate(
            [_dot(tt[:, g:g + V7X_MXU_WIDTH], ones_bd) for g in range(0, DA_QK_W, V7X_MXU_WIDTH)],
            axis=1) * (1.0 / DA_QK_DIM)
        t = t * lax.rsqrt(ms + NORM_EPS) * qkw_ref[idx:idx + 1, :]
        partner = jnp.where(first_half,
                            pltpu.roll(t, DA_QK_W - DA_ROT_DIM // 2, axis=1),
                            pltpu.roll(t, DA_ROT_DIM // 2, axis=1))
        rot = t * cos_da + partner * sin_da
        out_ref[...] = _bf16(rot.T if idx == 0 else rot)

    dv_ref[...] = _bf16(section(2))
    rv_ref[...] = _bf16(section(5))
    rg = section(6)
    rg_ref[...] = _bf16(rg * jax.nn.sigmoid(rg))
    gl_ref[...] = _bf16(section(7))


def _rotary_constants():
    n_da, n_rt = DA_ROT_DIM // 2, RET_QK_DIM // 2
    inv_da = ROPE_THETA ** (-jnp.arange(0, DA_ROT_DIM, 2, dtype=jnp.float32) / DA_ROT_DIM)
    inv_rt = 1.0 / (RET_THETA_BASE ** jnp.linspace(0.0, 1.0, n_rt, dtype=jnp.float32))
    freq = jnp.broadcast_to(jnp.concatenate([inv_da, inv_rt])[:, None], (n_da + n_rt, V7X_LANES))

    def selection(n, period):
        e = np.zeros((4 * n, 2 * V7X_LANES), np.float32)
        for f in range(n):
            for base in range(0, V7X_LANES, period):
                for part in (0, 2 * n):
                    e[part + f, base + f] = e[part + f, base + n + f] = 1.0
                    e[part + n + f, V7X_LANES + base + f] = -1.0
                    e[part + n + f, V7X_LANES + base + n + f] = 1.0
        return jnp.asarray(e, jnp.bfloat16)

    return freq, selection(n_da, DA_QK_DIM), selection(n_rt, RET_QK_DIM)


def _in_proj(x2, pos, n1w, w_in, qkw, ones_bd):
    T = x2.shape[0]
    rows = min(PROJ_ROWS, T)
    assert T % rows == 0 and rows % V7X_LANES == 0
    freq, e_da, e_rt = _rotary_constants()
    pos = pos.reshape(T // rows, 1, rows)
    row_spec = lambda width: pl.BlockSpec((rows, width), lambda i: (i, 0))
    const_spec = lambda shape: pl.BlockSpec(shape, lambda i: (0, 0))
    out_widths = SECTION_WIDTHS
    return pl.pallas_call(
        _in_proj_kernel,
        grid=(T // rows,),
        in_specs=[
            row_spec(D_MODEL),
            pl.BlockSpec((1, 1, rows), lambda i: (i, 0, 0)),
            const_spec((1, D_MODEL)),
            pl.BlockSpec((D_MODEL, IN_W), lambda i: (0, 0), pipeline_mode=pl.Buffered(1)),
            const_spec((2, DA_QK_W)),
            const_spec(freq.shape), const_spec(e_da.shape), const_spec(e_rt.shape),
            const_spec((V7X_MXU_WIDTH, V7X_MXU_WIDTH)),
        ],
        out_specs=[pl.BlockSpec((DA_QK_W, rows), lambda i: (0, i))] + [row_spec(w) for w in out_widths[1:]],
        out_shape=[jax.ShapeDtypeStruct((DA_QK_W, T), jnp.bfloat16)]
        + [jax.ShapeDtypeStruct((T, w), jnp.bfloat16) for w in out_widths[1:]],
        compiler_params=pltpu.CompilerParams(
            dimension_semantics=("parallel",), vmem_limit_bytes=V7X_VMEM_LIMIT_BYTES),
        name="in_proj",
    )(x2, pos, n1w, w_in, qkw, freq, e_da, e_rt, ones_bd)


def _diff_attn_kernel(q_ref, k_ref, v_ref, lam_ref, sw_ref, o_ref,
                      qst_ref, s_ref, p_ref, acc_ref, *, qb, kb):
    qi = pl.program_id(2)
    nq = 2 * qb
    n_strip = nq // V7X_LANES
    f32 = jnp.float32

    qt = q_ref[...]
    dim = lax.broadcasted_iota(jnp.int32, qt.shape, 0)
    qst_ref[:, :qb] = jnp.where(dim < DA_QK_DIM, qt, jnp.zeros_like(qt))
    qst_ref[:, qb:] = jnp.where(dim >= DA_QK_DIM, qt, jnp.zeros_like(qt))
    acc_ref[...] = jnp.zeros(acc_ref.shape, f32)
    p_ref[1] = jnp.zeros(p_ref.shape[1:], p_ref.dtype)

    def stage_a(t, slot):
        start = pl.multiple_of(t * kb, kb)
        s_ref[slot] = _dot(k_ref[pl.ds(start, kb), :], qst_ref[...])

    def stage_b(slot, m, l, key_offset):
        m_out, l_out, a_out = [], [], []
        for c in range(n_strip):
            cols = slice(c * V7X_LANES, (c + 1) * V7X_LANES)
            st = s_ref[slot, :, cols]
            if key_offset is not None:
                krow = lax.broadcasted_iota(jnp.int32, st.shape, 0) + key_offset
                qcol = lax.broadcasted_iota(jnp.int32, st.shape, 1) + (c * V7X_LANES) % qb
                st = jnp.where(krow <= qcol, st, MASK_VALUE)
            m_new = jnp.maximum(m[c], jnp.max(st, axis=0, keepdims=True))
            alpha = jnp.exp2(m[c] - m_new)
            p = jnp.exp2(st - m_new)
            l_out.append(alpha * l[c] + jnp.sum(p, axis=0, keepdims=True))
            p_ref[slot, :, cols] = _bf16(p)
            m_out.append(m_new)
            a_out.append(alpha)
        return tuple(m_out), tuple(l_out), tuple(a_out)

    def stage_c(t, slot, alpha):
        start = pl.multiple_of(jnp.maximum(t, 0) * kb, kb)
        vt = v_ref[pl.ds(start, kb), :]
        for c in range(0, n_strip, 2):
            cols = slice(c * V7X_LANES, (c + 2) * V7X_LANES)
            a = jnp.concatenate([alpha[c], alpha[c + 1]], axis=1)
            acc_ref[:, cols] = a * acc_ref[:, cols] + _dot_tn(vt, p_ref[slot, :, cols])

    def body(u, carry):
        m, l, alpha = carry
        t = 2 * u
        stage_c(t - 1, 1, alpha)
        stage_a(t + 1, 1)
        m, l, alpha = stage_b(0, m, l, None)
        stage_c(t, 0, alpha)
        stage_a(t + 2, 0)
        m, l, alpha = stage_b(1, m, l, None)
        return m, l, alpha

    row = lambda v: tuple(jnp.full((1, V7X_LANES), v, f32) for _ in range(n_strip))
    stage_a(0, 0)
    m, l, alpha = lax.fori_loop(0, qi, body, (row(MASK_VALUE), row(0.0), row(1.0)))
    t = 2 * qi
    stage_c(t - 1, 1, alpha)
    stage_a(t + 1, 1)
    m, l, alpha = stage_b(0, m, l, 0)
    stage_c(t, 0, alpha)
    m, l, alpha = stage_b(1, m, l, kb)
    stage_c(t + 1, 1, alpha)

    o = acc_ref[...] / jnp.concatenate(l, axis=1)
    o = o[:, :qb] - lam_ref[...] * o[:, qb:]
    o = o * lax.rsqrt(jnp.mean(o * o, axis=0, keepdims=True) + SUBLN_EPS)
    o_ref[...] = _bf16(o * jnp.concatenate([sw_ref[...]] * (qb // V7X_LANES), axis=1))


def _diff_attn(dq, dk, dv, lam, subln_w, B, S):
    blk = min(ATTN_BLOCK, S)
    kb = blk // 2
    assert S % blk == 0 and kb % V7X_LANES == 0
    nq = S // blk
    kernel = functools.partial(_diff_attn_kernel, qb=blk, kb=kb)
    return pl.pallas_call(
        kernel,
        grid=(B, DA_HEADS, nq),
        scratch_shapes=[
            pltpu.VMEM((DA_V_DIM, 2 * blk), jnp.bfloat16),
            pltpu.VMEM((2, kb, 2 * blk), jnp.float32),
            pltpu.VMEM((2, kb, 2 * blk), jnp.bfloat16),
            pltpu.VMEM((DA_V_DIM, 2 * blk), jnp.float32),
        ],
        in_specs=[
            pl.BlockSpec((2 * DA_QK_DIM, blk), lambda b, h, i: (h, b * nq + i)),
            pl.BlockSpec((S, V7X_LANES), lambda b, h, i: (b, h)),
            pl.BlockSpec((S, V7X_LANES), lambda b, h, i: (b, h)),
            pl.BlockSpec((1, 1), lambda b, h, i: (0, 0)),
            pl.BlockSpec((DA_V_DIM, V7X_LANES), lambda b, h, i: (0, 0)),
        ],
        out_specs=pl.BlockSpec((DA_V_DIM, blk), lambda b, h, i: (h, b * nq + i)),
        out_shape=jax.ShapeDtypeStruct((DA_V_W, B * S), jnp.bfloat16),
        compiler_params=pltpu.CompilerParams(
            dimension_semantics=("parallel", "parallel", "arbitrary"),
            vmem_limit_bytes=V7X_VMEM_LIMIT_BYTES),
        name="diff_attn",
    )(dq, dk, dv, lam, subln_w)


def _diff_attn_bounded_kernel(q_ref, k_ref, v_ref, lam_ref, sw_ref, o_ref, qst_ref, p_ref, *, seq):
    tile = V7X_MXU_WIDTH
    n_tok_tile = seq // tile

    qt = q_ref[...]
    dim = lax.broadcasted_iota(jnp.int32, qt.shape, 0)
    qst_ref[0] = jnp.where(dim < DA_QK_DIM, qt, jnp.zeros_like(qt))
    qst_ref[1] = jnp.where(dim >= DA_QK_DIM, qt, jnp.zeros_like(qt))
    sw = jnp.concatenate([sw_ref[...]] * (tile // V7X_LANES), axis=1)

    items = [(j, m) for j in reversed(range(n_tok_tile)) for m in range(2)]
    slots = p_ref.shape[0]

    def probs(n):
        j, m = items[n]
        keys = (j + 1) * tile
        s = _dot(k_ref[:keys, :], qst_ref[m, :, j * tile:(j + 1) * tile])
        krow = lax.broadcasted_iota(jnp.int32, (tile, tile), 0)
        qcol = lax.broadcasted_iota(jnp.int32, (tile, tile), 1)
        p_diag = jnp.exp2(jnp.where(krow <= qcol, s[keys - tile:], MASK_VALUE))
        p_ref[n % slots, keys - tile:keys] = _bf16(p_diag)
        part = jnp.sum(p_diag.reshape(tile // 8, 8, tile), axis=0)
        if keys > tile:
            p_full = jnp.exp2(s[:keys - tile])
            p_ref[n % slots, :keys - tile] = _bf16(p_full)
            part = part + jnp.sum(p_full.reshape((keys - tile) // 8, 8, tile), axis=0)
        return jnp.sum(part, axis=0, keepdims=True)

    def weighted_values(n):
        j, _ = items[n]
        keys = (j + 1) * tile
        return _dot_tn(v_ref[:keys, :], p_ref[n % slots, :keys])

    lsum, out = {}, {}
    for n in range(len(items) + ATTN_PV_LAG):
        if n < len(items):
            lsum[n] = probs(n)
        done = n - ATTN_PV_LAG
        if done >= 0:
            out[done] = weighted_values(done) * (1.0 / lsum.pop(done))
            j, m = items[done]
            if m == 1:
                o = out.pop(done - 1) - lam_ref[...] * out.pop(done)
                o = o * lax.rsqrt(jnp.mean(o * o, axis=0, keepdims=True) + SUBLN_EPS)
                o_ref[:, j * tile:(j + 1) * tile] = _bf16(o * sw)


def _diff_attn_bounded(dq, dk, dv, lam, subln_w, B, S):
    assert S % V7X_MXU_WIDTH == 0
    kernel = functools.partial(_diff_attn_bounded_kernel, seq=S)
    return pl.pallas_call(
        kernel,
        grid=(B, DA_HEADS),
        scratch_shapes=[
            pltpu.VMEM((2, DA_V_DIM, S), jnp.bfloat16),
            pltpu.VMEM((2 * ATTN_PV_LAG + 2, S, V7X_MXU_WIDTH), jnp.bfloat16),
        ],
        in_specs=[
            pl.BlockSpec((2 * DA_QK_DIM, S), lambda b, h: (h, b)),
            pl.BlockSpec((S, V7X_LANES), lambda b, h: (b, h)),
            pl.BlockSpec((S, V7X_LANES), lambda b, h: (b, h)),
            pl.BlockSpec((1, 1), lambda b, h: (0, 0)),
            pl.BlockSpec((DA_V_DIM, V7X_LANES), lambda b, h: (0, 0)),
        ],
        out_specs=pl.BlockSpec((DA_V_DIM, S), lambda b, h: (h, b)),
        out_shape=jax.ShapeDtypeStruct((DA_V_W, B * S), jnp.bfloat16),
        compiler_params=pltpu.CompilerParams(
            dimension_semantics=("parallel", "parallel"),
            vmem_limit_bytes=V7X_VMEM_LIMIT_BYTES),
        name="diff_attn_bounded",
    )(dq, dk, dv, lam, subln_w)


def _retention_kernel(q_ref, k_ref, v_ref, g_ref, lg_ref, o_ref, state_ref, decay_ref, xi_ref, zeta_ref,
                      *, chunk, n_chunks):
    f32 = jnp.float32

    @pl.when(pl.program_id(1) == 0)
    def _():
        state_ref[...] = jnp.zeros(state_ref.shape, f32)
        row = lax.broadcasted_iota(jnp.int32, (chunk, chunk), 0)
        col = lax.broadcasted_iota(jnp.int32, (chunk, chunk), 1)
        rel = (row - col).astype(f32)
        idx = lax.broadcasted_iota(jnp.int32, (chunk, V7X_LANES), 0).astype(f32)
        for h in range(RET_HEADS):
            lg = lg_ref[h, :, 0:1]
            decay_ref[h] = jnp.where(rel >= 0, jnp.exp(lg * jnp.maximum(rel, 0.0)), 0.0)
            xi_ref[h] = jnp.exp(lg * (idx + 1.0))
            zeta_ref[h] = jnp.exp(lg * (chunk - 1.0 - idx))

    def body(c, carry):
        start = pl.multiple_of(c * chunk, chunk)
        rows = pl.ds(start, chunk)
        heads = range(RET_HEADS)
        qk_cols = [slice(h * RET_QK_DIM, (h + 1) * RET_QK_DIM) for h in heads]
        v_cols = [slice(h * RET_V_DIM, (h + 1) * RET_V_DIM) for h in heads]
        lg = [lg_ref[h, :, 0:1] for h in heads]
        qc = [q_ref[rows, qk_cols[h]] for h in heads]
        kc = [k_ref[rows, qk_cols[h]] for h in heads]
        scores = [_bf16(_dot_nt(qc[h], kc[h]) * decay_ref[h]) for h in heads]
        xi = [jnp.concatenate([xi_ref[h]] * (RET_V_DIM // V7X_LANES), axis=1) for h in heads]
        cross = [_dot(qc[h], _bf16(state_ref[h])) * xi[h] for h in heads]
        o = [_dot(scores[h], v_ref[rows, v_cols[h]]) + cross[h] for h in heads]
        for h in heads:
            kz = _bf16(kc[h].astype(f32) * zeta_ref[h])
            state_ref[h] = (jnp.exp(lg[h] * float(chunk)) * state_ref[h]
                            + _dot_tn(kz, v_ref[rows, v_cols[h]]))
        mu = [jnp.mean(o[h], axis=-1, keepdims=True) for h in heads]
        d = [o[h] - mu[h] for h in heads]
        var = [jnp.mean(d[h] * d[h], axis=-1, keepdims=True) for h in heads]
        for h in heads:
            on = d[h] * lax.rsqrt(var[h] + GN_EPS)
            o_ref[rows, v_cols[h]] = _bf16(g_ref[rows, v_cols[h]].astype(f32) * on)
        return carry

    lax.fori_loop(0, n_chunks, body, 0)


def _retention(rq, rk, rv, rg, lg_rows, B, S):
    chunk = min(RET_CHUNK, S)
    sblk = min(RET_SEQ_BLOCK, S)
    assert S % sblk == 0 and sblk % chunk == 0
    nsb = S // sblk
    kernel = functools.partial(_retention_kernel, chunk=chunk, n_chunks=sblk // chunk)
    qk_spec = pl.BlockSpec((sblk, RET_QK_W), lambda b, s: (b * nsb + s, 0))
    v_spec = pl.BlockSpec((sblk, RET_V_W), lambda b, s: (b * nsb + s, 0))
    return pl.pallas_call(
        kernel,
        grid=(B, nsb),
        scratch_shapes=[
            pltpu.VMEM((RET_HEADS, RET_QK_DIM, RET_V_DIM), jnp.float32),
            pltpu.VMEM((RET_HEADS, chunk, chunk), jnp.float32),
            pltpu.VMEM((RET_HEADS, chunk, V7X_LANES), jnp.float32),
            pltpu.VMEM((RET_HEADS, chunk, V7X_LANES), jnp.float32),
        ],
        in_specs=[
            qk_spec, qk_spec, v_spec, v_spec,
            pl.BlockSpec((RET_HEADS, 1, V7X_LANES), lambda b, s: (0, 0, 0)),
        ],
        out_specs=v_spec,
        out_shape=jax.ShapeDtypeStruct((B * S, RET_V_W), jnp.bfloat16),
        compiler_params=pltpu.CompilerParams(
            dimension_semantics=("parallel", "arbitrary"),
            vmem_limit_bytes=V7X_VMEM_LIMIT_BYTES),
        name="retention",
    )(rq, rk, rv, rg, lg_rows)


def _out_mlp_kernel(x_ref, oa_ref, or_ref, gl_ref, wa_ref, wr_ref, wo_ref, n2w_ref, w1_ref, w2_ref,
                    o_ref):
    ya = _dot_tn(oa_ref[...], wa_ref[...])
    yr = _dot(or_ref[...], wr_ref[...])
    ga = jax.nn.sigmoid(gl_ref[:, :D_MODEL].astype(jnp.float32))
    gr = jax.nn.sigmoid(gl_ref[:, D_MODEL:].astype(jnp.float32))
    merged = _bf16(ga * ya + gr * yr)
    x1 = x_ref[...] + _dot(merged, wo_ref[...])
    h2 = x1 * lax.rsqrt(jnp.mean(x1 * x1, axis=-1, keepdims=True) + NORM_EPS) * n2w_ref[...]
    h2 = _bf16(h2)
    acc = x1
    for c in range(D_FF // MLP_FF_CHUNK):
        lo = c * MLP_FF_CHUNK
        hid = jnp.maximum(_dot(h2, w1_ref[:, lo:lo + MLP_FF_CHUNK]), 0.0)
        acc = acc + _dot(_bf16(hid * hid), w2_ref[lo:lo + MLP_FF_CHUNK, :])
    o_ref[...] = acc


def _out_mlp(x2, oa, orr, gl, wa, wr, wo, n2w, w1, w2):
    T = x2.shape[0]
    rows = min(MLP_ROWS, T)
    assert T % rows == 0
    row_spec = lambda width: pl.BlockSpec((rows, width), lambda i: (i, 0))
    weight_spec = lambda shape: pl.BlockSpec(shape, lambda i: (0, 0), pipeline_mode=pl.Buffered(1))
    return pl.pallas_call(
        _out_mlp_kernel,
        grid=(T // rows,),
        in_specs=[
            row_spec(D_MODEL), pl.BlockSpec((DA_V_W, rows), lambda i: (0, i)),
            row_spec(RET_V_W), row_spec(GATE_W),
            weight_spec((DA_V_W, D_MODEL)), weight_spec((RET_V_W, D_MODEL)),
            weight_spec((D_MODEL, D_MODEL)),
            pl.BlockSpec((1, D_MODEL), lambda i: (0, 0)),
            weight_spec((D_MODEL, D_FF)), weight_spec((D_FF, D_MODEL)),
        ],
        out_specs=row_spec(D_MODEL),
        out_shape=jax.ShapeDtypeStruct((T, D_MODEL), jnp.float32),
        compiler_params=pltpu.CompilerParams(
            dimension_semantics=("parallel",), vmem_limit_bytes=V7X_VMEM_LIMIT_BYTES),
        name="out_mlp",
    )(x2, oa, orr, gl, wa, wr, wo, n2w, w1, w2)


def kernel(x, positions, norm1_w, w_in, q_norm_w, k_norm_w, lambda_q1, lambda_k1, lambda_q2,
           lambda_k2, da_subln_w, w_da_branch, ret_gn_w, w_ret_branch, w_out, norm2_w, w_mlp_in,
           w_mlp_out):
    B, S, _ = x.shape
    T = B * S
    f32 = jnp.float32
    x2 = x.reshape(T, D_MODEL)

    q_scale = (DA_QK_DIM ** -0.5) * LOG2_E
    qkw = jnp.stack([jnp.tile(q_norm_w[0].astype(f32) * q_scale, DA_QK_W // DA_QK_DIM),
                     jnp.tile(k_norm_w[0].astype(f32), DA_QK_W // DA_QK_DIM)])
    blk = np.arange(V7X_MXU_WIDTH) // DA_QK_DIM
    ones_bd = jnp.asarray(blk[:, None] == blk[None, :], jnp.bfloat16)

    dq, dk, dv, rq, rk, rv, rg, gl = _in_proj(
        x2, positions.astype(f32), norm1_w[0].astype(f32).reshape(1, D_MODEL), _bf16(w_in[0]), qkw,
        ones_bd)

    lam = (jnp.exp(jnp.sum(lambda_q1[0].astype(f32) * lambda_k1[0].astype(f32)))
           - jnp.exp(jnp.sum(lambda_q2[0].astype(f32) * lambda_k2[0].astype(f32)))
           + LAMBDA_INIT).reshape(1, 1)
    subln = jnp.broadcast_to((da_subln_w[0].astype(f32) * (1.0 - LAMBDA_INIT))[:, None],
                             (DA_V_DIM, V7X_LANES))
    logit_bound = (DA_QK_DIM * jnp.max(jnp.abs(qkw[0])) * jnp.max(jnp.abs(qkw[1]))
                   * (1.0 + 2.0 ** -6))
    oa = lax.cond(logit_bound <= ATTN_BOUNDED_LOGIT,
                  functools.partial(_diff_attn_bounded, B=B, S=S),
                  functools.partial(_diff_attn, B=B, S=S),
                  dq, dk, dv, lam, subln)

    log_gamma = np.log1p(-np.exp2(-5.0 - np.arange(RET_HEADS, dtype=np.float64)))
    lg_rows = jnp.asarray(np.broadcast_to(log_gamma[:, None, None], (RET_HEADS, 1, V7X_LANES)), f32)
    orr = _retention(rq, rk, rv, rg, lg_rows, B, S)

    w_ret = _bf16(ret_gn_w[0].astype(f32)[:, None] * w_ret_branch[0].astype(f32))
    out = _out_mlp(x2, oa, orr, gl, _bf16(w_da_branch[0]), w_ret, _bf16(w_out[0]),
                   norm2_w[0].astype(f32).reshape(1, D_MODEL), _bf16(w_mlp_in[0]),
                   _bf16(w_mlp_out[0]))
    return out.reshape(B, S, D_MODEL)
```

```python
import functools
import math

import jax
import jax.numpy as jnp
import numpy as np
from jax import lax
from jax.experimental import pallas as pl
from jax.experimental.pallas import tpu as pltpu

D_MODEL = 1024
DA_HEADS = 4
DA_QK_DIM = 64
DA_V_DIM = 2 * DA_QK_DIM
DA_ROT_DIM = DA_QK_DIM // 4
ROPE_THETA = 500000.0
RET_HEADS = 4
RET_QK_DIM = 128
RET_V_DIM = 256
RET_THETA_BASE = 10000.0
D_FF = 4 * D_MODEL
NORM_EPS = 1e-6
SUBLN_EPS = 1e-5
GN_EPS = 1e-5
MASK_VALUE = -1e30
LAMBDA_INIT = 0.8 - 0.6 * math.exp(-0.3 * 0)

DA_QK_W = DA_HEADS * 2 * DA_QK_DIM
DA_V_W = DA_HEADS * DA_V_DIM
RET_QK_W = RET_HEADS * RET_QK_DIM
RET_V_W = RET_HEADS * RET_V_DIM
GATE_W = 2 * D_MODEL
SECTION_WIDTHS = (DA_QK_W, DA_QK_W, DA_V_W, RET_QK_W, RET_QK_W, RET_V_W, RET_V_W, GATE_W)
SECTION_STARTS = tuple(int(v) for v in np.cumsum((0,) + SECTION_WIDTHS[:-1]))
IN_W = sum(SECTION_WIDTHS)

V7X_LANES = 128
V7X_MXU_WIDTH = 256
V7X_VMEM_LIMIT_BYTES = 56 * 1024 * 1024

LOG2_E = math.log2(math.e)

PROJ_ROWS = 512
ROT_FINE = 64
ROT_COARSE = 128
ATTN_BLOCK = 512
ATTN_BOUNDED_LOGIT = 60.0
ATTN_PV_LAG = 2
RET_CHUNK = 256
RET_SEQ_BLOCK = 1024
MLP_ROWS = 512
MLP_FF_CHUNK = 1024


def _bf16(t):
    return t.astype(jnp.bfloat16)


def _dot(a, b):
    return jnp.dot(a, b, preferred_element_type=jnp.float32)


def _dot_nt(a, b):
    return lax.dot_general(a, b, (((1,), (1,)), ((), ())), preferred_element_type=jnp.float32)


def _dot_tn(a, b):
    return lax.dot_general(a, b, (((0,), (0,)), ((), ())), preferred_element_type=jnp.float32)


def _split_bf16(t):
    hi = _bf16(t)
    lo = _bf16(t - hi.astype(jnp.float32))
    return jnp.concatenate([hi, lo], axis=0)


def _in_proj_kernel(x_ref, pos_ref, n1w_ref, w_ref, qkw_ref, freq_ref, coarse_ref, fine_ref, eda_ref,
                    ert_ref, ones_ref,
                    dq_ref, dk_ref, dv_ref, rq_ref, rk_ref, rv_ref, rg_ref, gl_ref, *, table_trig):
    x = x_ref[...]
    h = x * lax.rsqrt(jnp.mean(x * x, axis=-1, keepdims=True) + NORM_EPS) * n1w_ref[...]
    h = _bf16(h)

    def section(idx):
        lo = SECTION_STARTS[idx]
        return _dot(h, w_ref[:, lo:lo + SECTION_WIDTHS[idx]])

    rows = x.shape[0]
    n_da = DA_ROT_DIM // 2
    n_freq = freq_ref.shape[0]
    pos = pos_ref[0]
    if table_trig:
        a = jnp.floor(pos * (1.0 / ROT_FINE))
        b = pos - ROT_FINE * a
        pick_a = jnp.where(lax.broadcasted_iota(jnp.int32, (ROT_COARSE, rows), 0).astype(jnp.float32) == a,
                           1.0, 0.0).astype(jnp.bfloat16)
        pick_b = jnp.where(lax.broadcasted_iota(jnp.int32, (ROT_FINE, rows), 0).astype(jnp.float32) == b,
                           1.0, 0.0).astype(jnp.bfloat16)
        cs_a = _dot(coarse_ref[...], jnp.concatenate([pick_a, pick_a], axis=0))
        cs_b = _dot(fine_ref[...], jnp.concatenate([pick_b, pick_b], axis=0))
        cos_t = cs_a[:n_freq] * cs_b[:n_freq] - cs_a[n_freq:] * cs_b[n_freq:]
        sin_t = cs_a[n_freq:] * cs_b[:n_freq] + cs_a[:n_freq] * cs_b[n_freq:]
    else:
        ang = jnp.concatenate([freq_ref[...]] * (rows // V7X_LANES), axis=1) * pos
        cos_t, sin_t = jnp.cos(ang), jnp.sin(ang)
    tab_da = _dot_tn(_split_bf16(jnp.concatenate([cos_t[:n_da], sin_t[:n_da]], axis=0)),
                     eda_ref[...])
    tab_rt = _dot_tn(_split_bf16(jnp.concatenate([cos_t[n_da:], sin_t[n_da:]], axis=0)),
                     ert_ref[...])
    lane1 = lax.broadcasted_iota(jnp.int32, (1, V7X_LANES), 1)
    unrotated = jnp.where(lane1 % DA_QK_DIM >= DA_ROT_DIM, 1.0, 0.0)

    cos_rt = tab_rt[:, :V7X_LANES]
    sin_rt = tab_rt[:, V7X_LANES:]
    for idx, out_ref, scale in ((3, rq_ref, 1.0), (4, rk_ref, RET_QK_DIM ** -0.5)):
        t = section(idx)
        for hh in range(RET_HEADS):
            th = t[:, hh * RET_QK_DIM:(hh + 1) * RET_QK_DIM]
            rot = th * cos_rt + pltpu.roll(th, RET_QK_DIM // 2, axis=1) * sin_rt
            if scale != 1.0:
                rot = rot * scale
            out_ref[:, hh * RET_QK_DIM:(hh + 1) * RET_QK_DIM] = _bf16(rot)

    reps = DA_QK_W // V7X_LANES
    cos_da = jnp.concatenate([tab_da[:, :V7X_LANES] + unrotated] * reps, axis=1)
    sin_da = jnp.concatenate([tab_da[:, V7X_LANES:]] * reps, axis=1)
    lane = lax.broadcasted_iota(jnp.int32, (rows, DA_QK_W), 1)
    first_half = (lane % DA_QK_DIM) < (DA_ROT_DIM // 2)
    ones_bd = ones_ref[...]
    for idx, out_ref in ((0, dq_ref), (1, dk_ref)):
        t = section(idx)
        tt = _bf16(t * t)
        ms = jnp.concatenate(
            [_dot(tt[:, g:g + V7X_MXU_WIDTH], ones_bd) for g in range(0, DA_QK_W, V7X_MXU_WIDTH)],
            axis=1) * (1.0 / DA_QK_DIM)
        t = t * lax.rsqrt(ms + NORM_EPS) * qkw_ref[idx:idx + 1, :]
        partner = jnp.where(first_half,
                            pltpu.roll(t, DA_QK_W - DA_ROT_DIM // 2, axis=1),
                            pltpu.roll(t, DA_ROT_DIM // 2, axis=1))
        rot = t * cos_da + partner * sin_da
        out_ref[...] = _bf16(rot.T if idx == 0 else rot)

    dv_ref[...] = _bf16(section(2))
    rv_ref[...] = _bf16(section(5))
    rg = section(6)
    rg_ref[...] = _bf16(rg * jax.nn.sigmoid(rg))
    gl_ref[...] = _bf16(section(7))


def _rotary_constants():
    n_da, n_rt = DA_ROT_DIM // 2, RET_QK_DIM // 2
    inv_da = ROPE_THETA ** (-jnp.arange(0, DA_ROT_DIM, 2, dtype=jnp.float32) / DA_ROT_DIM)
    inv_rt = 1.0 / (RET_THETA_BASE ** jnp.linspace(0.0, 1.0, n_rt, dtype=jnp.float32))
    freqs = jnp.concatenate([inv_da, inv_rt])
    freq = jnp.broadcast_to(freqs[:, None], (n_da + n_rt, V7X_LANES))

    def trig_table(multiples):
        bits = lax.bitcast_convert_type(freqs, jnp.uint32) & jnp.uint32(0xFFFFE000)
        f_hi = lax.bitcast_convert_type(bits, jnp.float32)
        big = f_hi[:, None] * multiples[None, :]
        small = (freqs - f_hi)[:, None] * multiples[None, :]
        cos = jnp.cos(big) * jnp.cos(small) - jnp.sin(big) * jnp.sin(small)
        sin = jnp.sin(big) * jnp.cos(small) + jnp.cos(big) * jnp.sin(small)
        t = jnp.concatenate([cos, sin], axis=0)
        hi = _bf16(t)
        return jnp.concatenate([hi, _bf16(t - hi.astype(jnp.float32))], axis=1)

    coarse = trig_table(ROT_FINE * jnp.arange(ROT_COARSE, dtype=jnp.float32))
    fine = trig_table(jnp.arange(ROT_FINE, dtype=jnp.float32))

    def selection(n, period):
        e = np.zeros((4 * n, 2 * V7X_LANES), np.float32)
        for f in range(n):
            for base in range(0, V7X_LANES, period):
                for part in (0, 2 * n):
                    e[part + f, base + f] = e[part + f, base + n + f] = 1.0
                    e[part + n + f, V7X_LANES + base + f] = -1.0
                    e[part + n + f, V7X_LANES + base + n + f] = 1.0
        return jnp.asarray(e, jnp.bfloat16)

    return freq, coarse, fine, selection(n_da, DA_QK_DIM), selection(n_rt, RET_QK_DIM)


def _in_proj(x2, pos, n1w, w_in, qkw, ones_bd, table_trig):
    T = x2.shape[0]
    rows = min(PROJ_ROWS, T)
    assert T % rows == 0 and rows % V7X_LANES == 0
    freq, coarse, fine, e_da, e_rt = _rotary_constants()
    pos = pos.reshape(T // rows, 1, rows)
    row_spec = lambda width: pl.BlockSpec((rows, width), lambda i: (i, 0))
    const_spec = lambda shape: pl.BlockSpec(shape, lambda i: (0, 0))
    out_widths = SECTION_WIDTHS
    return pl.pallas_call(
        functools.partial(_in_proj_kernel, table_trig=table_trig),
        grid=(T // rows,),
        in_specs=[
            row_spec(D_MODEL),
            pl.BlockSpec((1, 1, rows), lambda i: (i, 0, 0)),
            const_spec((1, D_MODEL)),
            pl.BlockSpec((D_MODEL, IN_W), lambda i: (0, 0), pipeline_mode=pl.Buffered(1)),
            const_spec((2, DA_QK_W)),
            const_spec(freq.shape), const_spec(coarse.shape), const_spec(fine.shape),
            const_spec(e_da.shape), const_spec(e_rt.shape),
            const_spec((V7X_MXU_WIDTH, V7X_MXU_WIDTH)),
        ],
        out_specs=[pl.BlockSpec((DA_QK_W, rows), lambda i: (0, i))] + [row_spec(w) for w in out_widths[1:]],
        out_shape=[jax.ShapeDtypeStruct((DA_QK_W, T), jnp.bfloat16)]
        + [jax.ShapeDtypeStruct((T, w), jnp.bfloat16) for w in out_widths[1:]],
        compiler_params=pltpu.CompilerParams(
            dimension_semantics=("parallel",), vmem_limit_bytes=V7X_VMEM_LIMIT_BYTES),
        name="in_proj",
    )(x2, pos, n1w, w_in, qkw, freq, coarse, fine, e_da, e_rt, ones_bd)


def _diff_attn_kernel(q_ref, k_ref, v_ref, lam_ref, sw_ref, o_ref,
                      qst_ref, s_ref, p_ref, acc_ref, *, qb, kb):
    qi = pl.program_id(2)
    nq = 2 * qb
    n_strip = nq // V7X_LANES
    f32 = jnp.float32

    qt = q_ref[...]
    dim = lax.broadcasted_iota(jnp.int32, qt.shape, 0)
    qst_ref[:, :qb] = jnp.where(dim < DA_QK_DIM, qt, jnp.zeros_like(qt))
    qst_ref[:, qb:] = jnp.where(dim >= DA_QK_DIM, qt, jnp.zeros_like(qt))
    acc_ref[...] = jnp.zeros(acc_ref.shape, f32)
    p_ref[1] = jnp.zeros(p_ref.shape[1:], p_ref.dtype)

    def stage_a(t, slot):
        start = pl.multiple_of(t * kb, kb)
        s_ref[slot] = _dot(k_ref[pl.ds(start, kb), :], qst_ref[...])

    def stage_b(slot, m, l, key_offset):
        m_out, l_out, a_out = [], [], []
        for c in range(n_strip):
            cols = slice(c * V7X_LANES, (c + 1) * V7X_LANES)
            st = s_ref[slot, :, cols]
            if key_offset is not None:
                krow = lax.broadcasted_iota(jnp.int32, st.shape, 0) + key_offset
                qcol = lax.broadcasted_iota(jnp.int32, st.shape, 1) + (c * V7X_LANES) % qb
                st = jnp.where(krow <= qcol, st, MASK_VALUE)
            m_new = jnp.maximum(m[c], jnp.max(st, axis=0, keepdims=True))
            alpha = jnp.exp2(m[c] - m_new)
            p = jnp.exp2(st - m_new)
            l_out.append(alpha * l[c] + jnp.sum(p, axis=0, keepdims=True))
            p_ref[slot, :, cols] = _bf16(p)
            m_out.append(m_new)
            a_out.append(alpha)
        return tuple(m_out), tuple(l_out), tuple(a_out)

    def stage_c(t, slot, alpha):
        start = pl.multiple_of(jnp.maximum(t, 0) * kb, kb)
        vt = v_ref[pl.ds(start, kb), :]
        for c in range(0, n_strip, 2):
            cols = slice(c * V7X_LANES, (c + 2) * V7X_LANES)
            a = jnp.concatenate([alpha[c], alpha[c + 1]], axis=1)
            acc_ref[:, cols] = a * acc_ref[:, cols] + _dot_tn(vt, p_ref[slot, :, cols])

    def body(u, carry):
        m, l, alpha = carry
        t = 2 * u
        stage_c(t - 1, 1, alpha)
        stage_a(t + 1, 1)
        m, l, alpha = stage_b(0, m, l, None)
        stage_c(t, 0, alpha)
        stage_a(t + 2, 0)
        m, l, alpha = stage_b(1, m, l, None)
        return m, l, alpha

    row = lambda v: tuple(jnp.full((1, V7X_LANES), v, f32) for _ in range(n_strip))
    stage_a(0, 0)
    m, l, alpha = lax.fori_loop(0, qi, body, (row(MASK_VALUE), row(0.0), row(1.0)))
    t = 2 * qi
    stage_c(t - 1, 1, alpha)
    stage_a(t + 1, 1)
    m, l, alpha = stage_b(0, m, l, 0)
    stage_c(t, 0, alpha)
    m, l, alpha = stage_b(1, m, l, kb)
    stage_c(t + 1, 1, alpha)

    o = acc_ref[...] / jnp.concatenate(l, axis=1)
    o = o[:, :qb] - lam_ref[...] * o[:, qb:]
    o = o * lax.rsqrt(jnp.mean(o * o, axis=0, keepdims=True) + SUBLN_EPS)
    o_ref[...] = _bf16(o * jnp.concatenate([sw_ref[...]] * (qb // V7X_LANES), axis=1))


def _diff_attn(dq, dk, dv, lam, subln_w, B, S):
    blk = min(ATTN_BLOCK, S)
    kb = blk // 2
    assert S % blk == 0 and kb % V7X_LANES == 0
    nq = S // blk
    kernel = functools.partial(_diff_attn_kernel, qb=blk, kb=kb)
    return pl.pallas_call(
        kernel,
        grid=(B, DA_HEADS, nq),
        scratch_shapes=[
            pltpu.VMEM((DA_V_DIM, 2 * blk), jnp.bfloat16),
            pltpu.VMEM((2, kb, 2 * blk), jnp.float32),
            pltpu.VMEM((2, kb, 2 * blk), jnp.bfloat16),
            pltpu.VMEM((DA_V_DIM, 2 * blk), jnp.float32),
        ],
        in_specs=[
            pl.BlockSpec((2 * DA_QK_DIM, blk), lambda b, h, i: (h, b * nq + i)),
            pl.BlockSpec((S, V7X_LANES), lambda b, h, i: (b, h)),
            pl.BlockSpec((S, V7X_LANES), lambda b, h, i: (b, h)),
            pl.BlockSpec((1, 1), lambda b, h, i: (0, 0)),
            pl.BlockSpec((DA_V_DIM, V7X_LANES), lambda b, h, i: (0, 0)),
        ],
        out_specs=pl.BlockSpec((DA_V_DIM, blk), lambda b, h, i: (h, b * nq + i)),
        out_shape=jax.ShapeDtypeStruct((DA_V_W, B * S), jnp.bfloat16),
        compiler_params=pltpu.CompilerParams(
            dimension_semantics=("parallel", "parallel", "arbitrary"),
            vmem_limit_bytes=V7X_VMEM_LIMIT_BYTES),
        name="diff_attn",
    )(dq, dk, dv, lam, subln_w)


def _diff_attn_bounded_kernel(q_ref, k_ref, v_ref, lam_ref, sw_ref, o_ref, qst_ref, p_ref, *, seq):
    tile = V7X_MXU_WIDTH
    n_tok_tile = seq // tile

    qt = q_ref[...]
    dim = lax.broadcasted_iota(jnp.int32, qt.shape, 0)
    qst_ref[0] = jnp.where(dim < DA_QK_DIM, qt, jnp.zeros_like(qt))
    qst_ref[1] = jnp.where(dim >= DA_QK_DIM, qt, jnp.zeros_like(qt))
    sw = jnp.concatenate([sw_ref[...]] * (tile // V7X_LANES), axis=1)

    items = [(j, m) for j in reversed(range(n_tok_tile)) for m in range(2)]
    slots = p_ref.shape[0]

    def probs(n):
        j, m = items[n]
        keys = (j + 1) * tile
        s = _dot(k_ref[:keys, :], qst_ref[m, :, j * tile:(j + 1) * tile])
        krow = lax.broadcasted_iota(jnp.int32, (tile, tile), 0)
        qcol = lax.broadcasted_iota(jnp.int32, (tile, tile), 1)
        p_diag = jnp.exp2(jnp.where(krow <= qcol, s[keys - tile:], MASK_VALUE))
        p_ref[n % slots, keys - tile:keys] = _bf16(p_diag)
        part = jnp.sum(p_diag.reshape(tile // 8, 8, tile), axis=0)
        if keys > tile:
            p_full = jnp.exp2(s[:keys - tile])
            p_ref[n % slots, :keys - tile] = _bf16(p_full)
            part = part + jnp.sum(p_full.reshape((keys - tile) // 8, 8, tile), axis=0)
        return jnp.sum(part, axis=0, keepdims=True)

    def weighted_values(n):
        j, _ = items[n]
        keys = (j + 1) * tile
        return _dot_tn(v_ref[:keys, :], p_ref[n % slots, :keys])

    lsum, out = {}, {}
    for n in range(len(items) + ATTN_PV_LAG):
        if n < len(items):
            lsum[n] = probs(n)
        done = n - ATTN_PV_LAG
        if done >= 0:
            out[done] = weighted_values(done) * (1.0 / lsum.pop(done))
            j, m = items[done]
            if m == 1:
                o = out.pop(done - 1) - lam_ref[...] * out.pop(done)
                o = o * lax.rsqrt(jnp.mean(o * o, axis=0, keepdims=True) + SUBLN_EPS)
                o_ref[:, j * tile:(j + 1) * tile] = _bf16(o * sw)


def _diff_attn_bounded(dq, dk, dv, lam, subln_w, B, S):
    assert S % V7X_MXU_WIDTH == 0
    kernel = functools.partial(_diff_attn_bounded_kernel, seq=S)
    return pl.pallas_call(
        kernel,
        grid=(B, DA_HEADS),
        scratch_shapes=[
            pltpu.VMEM((2, DA_V_DIM, S), jnp.bfloat16),
            pltpu.VMEM((2 * ATTN_PV_LAG + 2, S, V7X_MXU_WIDTH), jnp.bfloat16),
        ],
        in_specs=[
            pl.BlockSpec((2 * DA_QK_DIM, S), lambda b, h: (h, b)),
            pl.BlockSpec((S, V7X_LANES), lambda b, h: (b, h)),
            pl.BlockSpec((S, V7X_LANES), lambda b, h: (b, h)),
            pl.BlockSpec((1, 1), lambda b, h: (0, 0)),
            pl.BlockSpec((DA_V_DIM, V7X_LANES), lambda b, h: (0, 0)),
        ],
        out_specs=pl.BlockSpec((DA_V_DIM, S), lambda b, h: (h, b)),
        out_shape=jax.ShapeDtypeStruct((DA_V_W, B * S), jnp.bfloat16),
        compiler_params=pltpu.CompilerParams(
            dimension_semantics=("parallel", "parallel"),
            vmem_limit_bytes=V7X_VMEM_LIMIT_BYTES),
        name="diff_attn_bounded",
    )(dq, dk, dv, lam, subln_w)


def _retention_kernel(q_ref, k_ref, v_ref, g_ref, lg_ref, o_ref, state_ref, decay_ref, xi_ref, zeta_ref,
                      *, chunk, n_chunks):
    f32 = jnp.float32

    @pl.when(pl.program_id(1) == 0)
    def _():
        state_ref[...] = jnp.zeros(state_ref.shape, f32)
        row = lax.broadcasted_iota(jnp.int32, (chunk, chunk), 0)
        col = lax.broadcasted_iota(jnp.int32, (chunk, chunk), 1)
        rel = (row - col).astype(f32)
        idx = lax.broadcasted_iota(jnp.int32, (chunk, V7X_LANES), 0).astype(f32)
        for h in range(RET_HEADS):
            lg = lg_ref[h, :, 0:1]
            decay_ref[h] = jnp.where(rel >= 0, jnp.exp(lg * jnp.maximum(rel, 0.0)), 0.0)
            xi_ref[h] = jnp.exp(lg * (idx + 1.0))
            zeta_ref[h] = jnp.exp(lg * (chunk - 1.0 - idx))

    def body(c, carry):
        rows = pl.ds(c * chunk, chunk)
        heads = range(RET_HEADS)
        qk_cols = [slice(h * RET_QK_DIM, (h + 1) * RET_QK_DIM) for h in heads]
        v_cols = [slice(h * RET_V_DIM, (h + 1) * RET_V_DIM) for h in heads]
        lg = [lg_ref[h, :, 0:1] for h in heads]
        qc = [q_ref[rows, qk_cols[h]] for h in heads]
        kc = [k_ref[rows, qk_cols[h]] for h in heads]
        scores = [_bf16(_dot_nt(qc[h], kc[h]) * decay_ref[h]) for h in heads]
        xi = [jnp.concatenate([xi_ref[h]] * (RET_V_DIM // V7X_LANES), axis=1) for h in heads]
        cross = [_dot(qc[h], _bf16(state_ref[h])) * xi[h] for h in heads]
        o = [_dot(scores[h], v_ref[rows, v_cols[h]]) + cross[h] for h in heads]
        for h in heads:
            kz = _bf16(kc[h].astype(f32) * zeta_ref[h])
            state_ref[h] = (jnp.exp(lg[h] * float(chunk)) * state_ref[h]
                            + _dot_tn(kz, v_ref[rows, v_cols[h]]))
        mu = [jnp.mean(o[h], axis=-1, keepdims=True) for h in heads]
        d = [o[h] - mu[h] for h in heads]
        var = [jnp.mean(d[h] * d[h], axis=-1, keepdims=True) for h in heads]
        for h in heads:
            on = d[h] * lax.rsqrt(var[h] + GN_EPS)
            o_ref[rows, v_cols[h]] = _bf16(g_ref[rows, v_cols[h]].astype(f32) * on)
        return carry

    for c in range(n_chunks):
        body(c, 0)


def _retention(rq, rk, rv, rg, lg_rows, B, S):
    chunk = min(RET_CHUNK, S)
    sblk = min(RET_SEQ_BLOCK, S)
    assert S % sblk == 0 and sblk % chunk == 0
    nsb = S // sblk
    kernel = functools.partial(_retention_kernel, chunk=chunk, n_chunks=sblk // chunk)
    qk_spec = pl.BlockSpec((sblk, RET_QK_W), lambda b, s: (b * nsb + s, 0))
    v_spec = pl.BlockSpec((sblk, RET_V_W), lambda b, s: (b * nsb + s, 0))
    return pl.pallas_call(
        kernel,
        grid=(B, nsb),
        scratch_shapes=[
            pltpu.VMEM((RET_HEADS, RET_QK_DIM, RET_V_DIM), jnp.float32),
            pltpu.VMEM((RET_HEADS, chunk, chunk), jnp.float32),
            pltpu.VMEM((RET_HEADS, chunk, V7X_LANES), jnp.float32),
            pltpu.VMEM((RET_HEADS, chunk, V7X_LANES), jnp.float32),
        ],
        in_specs=[
            qk_spec, qk_spec, v_spec, v_spec,
            pl.BlockSpec((RET_HEADS, 1, V7X_LANES), lambda b, s: (0, 0, 0)),
        ],
        out_specs=v_spec,
        out_shape=jax.ShapeDtypeStruct((B * S, RET_V_W), jnp.bfloat16),
        compiler_params=pltpu.CompilerParams(
            dimension_semantics=("parallel", "arbitrary"),
            vmem_limit_bytes=V7X_VMEM_LIMIT_BYTES),
        name="retention",
    )(rq, rk, rv, rg, lg_rows)


def _out_mlp_kernel(x_ref, oa_ref, or_ref, gl_ref, wa_ref, wr_ref, wo_ref, n2w_ref, w1_ref, w2_ref,
                    o_ref):
    ya = _dot_tn(oa_ref[...], wa_ref[...])
    yr = _dot(or_ref[...], wr_ref[...])
    ga = jax.nn.sigmoid(gl_ref[:, :D_MODEL].astype(jnp.float32))
    gr = jax.nn.sigmoid(gl_ref[:, D_MODEL:].astype(jnp.float32))
    merged = _bf16(ga * ya + gr * yr)
    x1 = x_ref[...] + _dot(merged, wo_ref[...])
    h2 = x1 * lax.rsqrt(jnp.mean(x1 * x1, axis=-1, keepdims=True) + NORM_EPS) * n2w_ref[...]
    h2 = _bf16(h2)
    acc = x1
    for c in range(D_FF // MLP_FF_CHUNK):
        lo = c * MLP_FF_CHUNK
        hid = jnp.maximum(_dot(h2, w1_ref[:, lo:lo + MLP_FF_CHUNK]), 0.0)
        acc = acc + _dot(_bf16(hid * hid), w2_ref[lo:lo + MLP_FF_CHUNK, :])
    o_ref[...] = acc


def _out_mlp(x2, oa, orr, gl, wa, wr, wo, n2w, w1, w2):
    T = x2.shape[0]
    rows = min(MLP_ROWS, T)
    assert T % rows == 0
    row_spec = lambda width: pl.BlockSpec((rows, width), lambda i: (i, 0))
    weight_spec = lambda shape: pl.BlockSpec(shape, lambda i: (0, 0), pipeline_mode=pl.Buffered(1))
    return pl.pallas_call(
        _out_mlp_kernel,
        grid=(T // rows,),
        in_specs=[
            row_spec(D_MODEL), pl.BlockSpec((DA_V_W, rows), lambda i: (0, i)),
            row_spec(RET_V_W), row_spec(GATE_W),
            weight_spec((DA_V_W, D_MODEL)), weight_spec((RET_V_W, D_MODEL)),
            weight_spec((D_MODEL, D_MODEL)),
            pl.BlockSpec((1, D_MODEL), lambda i: (0, 0)),
            weight_spec((D_MODEL, D_FF)), weight_spec((D_FF, D_MODEL)),
        ],
        out_specs=row_spec(D_MODEL),
        out_shape=jax.ShapeDtypeStruct((T, D_MODEL), jnp.float32),
        compiler_params=pltpu.CompilerParams(
            dimension_semantics=("parallel",), vmem_limit_bytes=V7X_VMEM_LIMIT_BYTES),
        name="out_mlp",
    )(x2, oa, orr, gl, wa, wr, wo, n2w, w1, w2)


def kernel(x, positions, norm1_w, w_in, q_norm_w, k_norm_w, lambda_q1, lambda_k1, lambda_q2,
           lambda_k2, da_subln_w, w_da_branch, ret_gn_w, w_ret_branch, w_out, norm2_w, w_mlp_in,
           w_mlp_out):
    B, S, _ = x.shape
    T = B * S
    f32 = jnp.float32
    x2 = x.reshape(T, D_MODEL)

    q_scale = (DA_QK_DIM ** -0.5) * LOG2_E
    qkw = jnp.stack([jnp.tile(q_norm_w[0].astype(f32) * q_scale, DA_QK_W // DA_QK_DIM),
                     jnp.tile(k_norm_w[0].astype(f32), DA_QK_W // DA_QK_DIM)])
    blk = np.arange(V7X_MXU_WIDTH) // DA_QK_DIM
    ones_bd = jnp.asarray(blk[:, None] == blk[None, :], jnp.bfloat16)

    in_table = jnp.logical_and(jnp.min(positions) >= 0, jnp.max(positions) < ROT_FINE * ROT_COARSE)
    in_proj_args = (x2, positions.astype(f32), norm1_w[0].astype(f32).reshape(1, D_MODEL),
                    _bf16(w_in[0]), qkw, ones_bd)
    dq, dk, dv, rq, rk, rv, rg, gl = lax.cond(
        in_table, functools.partial(_in_proj, table_trig=True),
        functools.partial(_in_proj, table_trig=False), *in_proj_args)

    lam = (jnp.exp(jnp.sum(lambda_q1[0].astype(f32) * lambda_k1[0].astype(f32)))
           - jnp.exp(jnp.sum(lambda_q2[0].astype(f32) * lambda_k2[0].astype(f32)))
           + LAMBDA_INIT).reshape(1, 1)
    subln = jnp.broadcast_to((da_subln_w[0].astype(f32) * (1.0 - LAMBDA_INIT))[:, None],
                             (DA_V_DIM, V7X_LANES))
    logit_bound = (DA_QK_DIM * jnp.max(jnp.abs(qkw[0])) * jnp.max(jnp.abs(qkw[1]))
                   * (1.0 + 2.0 ** -6))
    oa = lax.cond(logit_bound <= ATTN_BOUNDED_LOGIT,
                  functools.partial(_diff_attn_bounded, B=B, S=S),
                  functools.partial(_diff_attn, B=B, S=S),
                  dq, dk, dv, lam, subln)

    log_gamma = np.log1p(-np.exp2(-5.0 - np.arange(RET_HEADS, dtype=np.float64)))
    lg_rows = jnp.asarray(np.broadcast_to(log_gamma[:, None, None], (RET_HEADS, 1, V7X_LANES)), f32)
    orr = _retention(rq, rk, rv, rg, lg_rows, B, S)

    w_ret = _bf16(ret_gn_w[0].astype(f32)[:, None] * w_ret_branch[0].astype(f32))
    out = _out_mlp(x2, oa, orr, gl, _bf16(w_da_branch[0]), w_ret, _bf16(w_out[0]),
                   norm2_w[0].astype(f32).reshape(1, D_MODEL), _bf16(w_mlp_in[0]),
                   _bf16(w_mlp_out[0]))
    return out.reshape(B, S, D_MODEL)
```

```python
import functools
import math

import jax
import jax.numpy as jnp
import numpy as np
from jax import lax
from jax.experimental import pallas as pl
from jax.experimental.pallas import tpu as pltpu

D_MODEL = 1024
DA_HEADS = 4
DA_QK_DIM = 64
DA_V_DIM = 2 * DA_QK_DIM
DA_ROT_DIM = DA_QK_DIM // 4
ROPE_THETA = 500000.0
RET_HEADS = 4
RET_QK_DIM = 128
RET_V_DIM = 256
RET_THETA_BASE = 10000.0
D_FF = 4 * D_MODEL
NORM_EPS = 1e-6
SUBLN_EPS = 1e-5
GN_EPS = 1e-5
MASK_VALUE = -1e30
LAMBDA_INIT = 0.8 - 0.6 * math.exp(-0.3 * 0)

DA_QK_W = DA_HEADS * 2 * DA_QK_DIM
DA_V_W = DA_HEADS * DA_V_DIM
RET_QK_W = RET_HEADS * RET_QK_DIM
RET_V_W = RET_HEADS * RET_V_DIM
GATE_W = 2 * D_MODEL
SECTION_WIDTHS = (DA_QK_W, DA_QK_W, DA_V_W, RET_QK_W, RET_QK_W, RET_V_W, RET_V_W, GATE_W)
SECTION_STARTS = tuple(int(v) for v in np.cumsum((0,) + SECTION_WIDTHS[:-1]))
IN_W = sum(SECTION_WIDTHS)

V7X_LANES = 128
V7X_SUBLANES = 8
V7X_MXU_WIDTH = 256
V7X_VMEM_LIMIT_BYTES = 56 * 1024 * 1024

LOG2_E = math.log2(math.e)

PROJ_ROWS = 512
ROT_FINE = 64
ROT_COARSE = 128
ATTN_BLOCK = 512
ATTN_BOUNDED_LOGIT = 60.0
ATTN_PV_LAG = 4
ATTN_BOUND_MARGIN = 1.0 + 2.0 ** -6
RET_CHUNK = 256
RET_SEQ_BLOCK = 1024
MLP_ROWS = 512
MLP_FF_CHUNK = 1024


def _bf16(t):
    return t.astype(jnp.bfloat16)


def _dot(a, b):
    return jnp.dot(a, b, preferred_element_type=jnp.float32)


def _dot_nt(a, b):
    return lax.dot_general(a, b, (((1,), (1,)), ((), ())), preferred_element_type=jnp.float32)


def _dot_tn(a, b):
    return lax.dot_general(a, b, (((0,), (0,)), ((), ())), preferred_element_type=jnp.float32)


def _split_bf16(t):
    hi = _bf16(t)
    lo = _bf16(t - hi.astype(jnp.float32))
    return jnp.concatenate([hi, lo], axis=0)


def _in_proj_kernel(x_ref, pos_ref, n1w_ref, w_ref, qkw_ref, freq_ref, coarse_ref, fine_ref, eda_ref,
                    ert_ref, ones_ref,
                    dq_ref, dk_ref, dv_ref, rq_ref, rk_ref, rv_ref, rg_ref, gl_ref, *, table_trig):
    x = x_ref[...]
    h = x * lax.rsqrt(jnp.mean(x * x, axis=-1, keepdims=True) + NORM_EPS) * n1w_ref[...]
    h = _bf16(h)

    def section(idx):
        lo = SECTION_STARTS[idx]
        return _dot(h, w_ref[:, lo:lo + SECTION_WIDTHS[idx]])

    rows = x.shape[0]
    n_da = DA_ROT_DIM // 2
    n_freq = freq_ref.shape[0]
    pos = pos_ref[0]
    if table_trig:
        a = jnp.floor(pos * (1.0 / ROT_FINE))
        b = pos - ROT_FINE * a
        pick_a = jnp.where(lax.broadcasted_iota(jnp.int32, (ROT_COARSE, rows), 0).astype(jnp.float32) == a,
                           1.0, 0.0).astype(jnp.bfloat16)
        pick_b = jnp.where(lax.broadcasted_iota(jnp.int32, (ROT_FINE, rows), 0).astype(jnp.float32) == b,
                           1.0, 0.0).astype(jnp.bfloat16)
        cs_a = _dot(coarse_ref[...], jnp.concatenate([pick_a, pick_a], axis=0))
        cs_b = _dot(fine_ref[...], jnp.concatenate([pick_b, pick_b], axis=0))
        cos_t = cs_a[:n_freq] * cs_b[:n_freq] - cs_a[n_freq:] * cs_b[n_freq:]
        sin_t = cs_a[n_freq:] * cs_b[:n_freq] + cs_a[:n_freq] * cs_b[n_freq:]
    else:
        ang = jnp.concatenate([freq_ref[...]] * (rows // V7X_LANES), axis=1) * pos
        cos_t, sin_t = jnp.cos(ang), jnp.sin(ang)
    tab_da = _dot_tn(_split_bf16(jnp.concatenate([cos_t[:n_da], sin_t[:n_da]], axis=0)),
                     eda_ref[...])
    tab_rt = _dot_tn(_split_bf16(jnp.concatenate([cos_t[n_da:], sin_t[n_da:]], axis=0)),
                     ert_ref[...])
    lane1 = lax.broadcasted_iota(jnp.int32, (1, V7X_LANES), 1)
    unrotated = jnp.where(lane1 % DA_QK_DIM >= DA_ROT_DIM, 1.0, 0.0)

    cos_rt = tab_rt[:, :V7X_LANES]
    sin_rt = tab_rt[:, V7X_LANES:]
    for idx, out_ref, scale in ((3, rq_ref, 1.0), (4, rk_ref, RET_QK_DIM ** -0.5)):
        t = section(idx)
        for hh in range(RET_HEADS):
            th = t[:, hh * RET_QK_DIM:(hh + 1) * RET_QK_DIM]
            rot = th * cos_rt + pltpu.roll(th, RET_QK_DIM // 2, axis=1) * sin_rt
            if scale != 1.0:
                rot = rot * scale
            out_ref[:, hh * RET_QK_DIM:(hh + 1) * RET_QK_DIM] = _bf16(rot)

    reps = DA_QK_W // V7X_LANES
    cos_da = jnp.concatenate([tab_da[:, :V7X_LANES] + unrotated] * reps, axis=1)
    sin_da = jnp.concatenate([tab_da[:, V7X_LANES:]] * reps, axis=1)
    lane = lax.broadcasted_iota(jnp.int32, (rows, DA_QK_W), 1)
    first_half = (lane % DA_QK_DIM) < (DA_ROT_DIM // 2)
    ones_bd = ones_ref[...]
    for idx, out_ref in ((0, dq_ref), (1, dk_ref)):
        t = section(idx)
        tt = _bf16(t * t)
        ms = jnp.concatenate(
            [_dot(tt[:, g:g + V7X_MXU_WIDTH], ones_bd) for g in range(0, DA_QK_W, V7X_MXU_WIDTH)],
            axis=1) * (1.0 / DA_QK_DIM)
        t = t * lax.rsqrt(ms + NORM_EPS) * qkw_ref[idx:idx + 1, :]
        partner = jnp.where(first_half,
                            pltpu.roll(t, DA_QK_W - DA_ROT_DIM // 2, axis=1),
                            pltpu.roll(t, DA_ROT_DIM // 2, axis=1))
        rot = t * cos_da + partner * sin_da
        out_ref[...] = _bf16(rot.T if idx == 0 else rot)

    dv_ref[...] = _bf16(section(2))
    rv_ref[...] = _bf16(section(5))
    rg = section(6)
    rg_ref[...] = _bf16(rg * jax.nn.sigmoid(rg))
    gl_ref[...] = _bf16(section(7))


def _rotary_constants():
    n_da, n_rt = DA_ROT_DIM // 2, RET_QK_DIM // 2
    inv_da = ROPE_THETA ** (-jnp.arange(0, DA_ROT_DIM, 2, dtype=jnp.float32) / DA_ROT_DIM)
    inv_rt = 1.0 / (RET_THETA_BASE ** jnp.linspace(0.0, 1.0, n_rt, dtype=jnp.float32))
    freqs = jnp.concatenate([inv_da, inv_rt])
    freq = jnp.broadcast_to(freqs[:, None], (n_da + n_rt, V7X_LANES))

    def trig_table(multiples):
        keep_11_bits = jnp.uint32(0xFFFFFFFF << (24 - 11) & 0xFFFFFFFF)
        bits = lax.bitcast_convert_type(freqs, jnp.uint32) & keep_11_bits
        f_hi = lax.bitcast_convert_type(bits, jnp.float32)
        big = f_hi[:, None] * multiples[None, :]
        small = (freqs - f_hi)[:, None] * multiples[None, :]
        cos = jnp.cos(big) * jnp.cos(small) - jnp.sin(big) * jnp.sin(small)
        sin = jnp.sin(big) * jnp.cos(small) + jnp.cos(big) * jnp.sin(small)
        t = jnp.concatenate([cos, sin], axis=0)
        hi = _bf16(t)
        return jnp.concatenate([hi, _bf16(t - hi.astype(jnp.float32))], axis=1)

    coarse = trig_table(ROT_FINE * jnp.arange(ROT_COARSE, dtype=jnp.float32))
    fine = trig_table(jnp.arange(ROT_FINE, dtype=jnp.float32))

    def selection(n, period):
        e = np.zeros((4 * n, 2 * V7X_LANES), np.float32)
        for f in range(n):
            for base in range(0, V7X_LANES, period):
                for part in (0, 2 * n):
                    e[part + f, base + f] = e[part + f, base + n + f] = 1.0
                    e[part + n + f, V7X_LANES + base + f] = -1.0
                    e[part + n + f, V7X_LANES + base + n + f] = 1.0
        return jnp.asarray(e, jnp.bfloat16)

    return freq, coarse, fine, selection(n_da, DA_QK_DIM), selection(n_rt, RET_QK_DIM)


def _in_proj(x2, pos, n1w, w_in, qkw, ones_bd, table_trig):
    T = x2.shape[0]
    rows = min(PROJ_ROWS, T)
    assert T % rows == 0 and rows % V7X_LANES == 0
    freq, coarse, fine, e_da, e_rt = _rotary_constants()
    pos = pos.reshape(T // rows, 1, rows)
    row_spec = lambda width: pl.BlockSpec((rows, width), lambda i: (i, 0))
    const_spec = lambda shape: pl.BlockSpec(shape, lambda i: (0, 0))
    out_widths = SECTION_WIDTHS
    return pl.pallas_call(
        functools.partial(_in_proj_kernel, table_trig=table_trig),
        grid=(T // rows,),
        in_specs=[
            row_spec(D_MODEL),
            pl.BlockSpec((1, 1, rows), lambda i: (i, 0, 0)),
            const_spec((1, D_MODEL)),
            pl.BlockSpec((D_MODEL, IN_W), lambda i: (0, 0), pipeline_mode=pl.Buffered(1)),
            const_spec((2, DA_QK_W)),
            const_spec(freq.shape), const_spec(coarse.shape), const_spec(fine.shape),
            const_spec(e_da.shape), const_spec(e_rt.shape),
            const_spec((V7X_MXU_WIDTH, V7X_MXU_WIDTH)),
        ],
        out_specs=[pl.BlockSpec((DA_QK_W, rows), lambda i: (0, i))] + [row_spec(w) for w in out_widths[1:]],
        out_shape=[jax.ShapeDtypeStruct((DA_QK_W, T), jnp.bfloat16)]
        + [jax.ShapeDtypeStruct((T, w), jnp.bfloat16) for w in out_widths[1:]],
        compiler_params=pltpu.CompilerParams(
            dimension_semantics=("parallel",), vmem_limit_bytes=V7X_VMEM_LIMIT_BYTES),
        name="in_proj",
    )(x2, pos, n1w, w_in, qkw, freq, coarse, fine, e_da, e_rt, ones_bd)


def _diff_attn_kernel(q_ref, k_ref, v_ref, lam_ref, sw_ref, o_ref,
                      qst_ref, s_ref, p_ref, acc_ref, *, qb, kb):
    qi = pl.program_id(2)
    nq = 2 * qb
    n_strip = nq // V7X_LANES
    f32 = jnp.float32

    qt = q_ref[...]
    dim = lax.broadcasted_iota(jnp.int32, qt.shape, 0)
    qst_ref[:, :qb] = jnp.where(dim < DA_QK_DIM, qt, jnp.zeros_like(qt))
    qst_ref[:, qb:] = jnp.where(dim >= DA_QK_DIM, qt, jnp.zeros_like(qt))
    acc_ref[...] = jnp.zeros(acc_ref.shape, f32)
    p_ref[1] = jnp.zeros(p_ref.shape[1:], p_ref.dtype)

    def stage_a(t, slot):
        start = pl.multiple_of(t * kb, kb)
        s_ref[slot] = _dot(k_ref[pl.ds(start, kb), :], qst_ref[...])

    def stage_b(slot, m, l, key_offset):
        m_out, l_out, a_out = [], [], []
        for c in range(n_strip):
            cols = slice(c * V7X_LANES, (c + 1) * V7X_LANES)
            st = s_ref[slot, :, cols]
            if key_offset is not None:
                krow = lax.broadcasted_iota(jnp.int32, st.shape, 0) + key_offset
                qcol = lax.broadcasted_iota(jnp.int32, st.shape, 1) + (c * V7X_LANES) % qb
                st = jnp.where(krow <= qcol, st, MASK_VALUE)
            m_new = jnp.maximum(m[c], jnp.max(st, axis=0, keepdims=True))
            alpha = jnp.exp2(m[c] - m_new)
            p = jnp.exp2(st - m_new)
            l_out.append(alpha * l[c] + jnp.sum(p, axis=0, keepdims=True))
            p_ref[slot, :, cols] = _bf16(p)
            m_out.append(m_new)
            a_out.append(alpha)
        return tuple(m_out), tuple(l_out), tuple(a_out)

    def stage_c(t, slot, alpha):
        start = pl.multiple_of(jnp.maximum(t, 0) * kb, kb)
        vt = v_ref[pl.ds(start, kb), :]
        for c in range(0, n_strip, 2):
            cols = slice(c * V7X_LANES, (c + 2) * V7X_LANES)
            a = jnp.concatenate([alpha[c], alpha[c + 1]], axis=1)
            acc_ref[:, cols] = a * acc_ref[:, cols] + _dot_tn(vt, p_ref[slot, :, cols])

    def body(u, carry):
        m, l, alpha = carry
        t = 2 * u
        stage_c(t - 1, 1, alpha)
        stage_a(t + 1, 1)
        m, l, alpha = stage_b(0, m, l, None)
        stage_c(t, 0, alpha)
        stage_a(t + 2, 0)
        m, l, alpha = stage_b(1, m, l, None)
        return m, l, alpha

    row = lambda v: tuple(jnp.full((1, V7X_LANES), v, f32) for _ in range(n_strip))
    stage_a(0, 0)
    m, l, alpha = lax.fori_loop(0, qi, body, (row(MASK_VALUE), row(0.0), row(1.0)))
    t = 2 * qi
    stage_c(t - 1, 1, alpha)
    stage_a(t + 1, 1)
    m, l, alpha = stage_b(0, m, l, 0)
    stage_c(t, 0, alpha)
    m, l, alpha = stage_b(1, m, l, kb)
    stage_c(t + 1, 1, alpha)

    o = acc_ref[...] / jnp.concatenate(l, axis=1)
    o = o[:, :qb] - lam_ref[...] * o[:, qb:]
    o = o * lax.rsqrt(jnp.mean(o * o, axis=0, keepdims=True) + SUBLN_EPS)
    o_ref[...] = _bf16(o * jnp.concatenate([sw_ref[...]] * (qb // V7X_LANES), axis=1))


def _diff_attn(dq, dk, dv, lam, subln_w, B, S):
    blk = min(ATTN_BLOCK, S)
    kb = blk // 2
    assert S % blk == 0 and kb % V7X_LANES == 0
    nq = S // blk
    kernel = functools.partial(_diff_attn_kernel, qb=blk, kb=kb)
    return pl.pallas_call(
        kernel,
        grid=(B, DA_HEADS, nq),
        scratch_shapes=[
            pltpu.VMEM((DA_V_DIM, 2 * blk), jnp.bfloat16),
            pltpu.VMEM((2, kb, 2 * blk), jnp.float32),
            pltpu.VMEM((2, kb, 2 * blk), jnp.bfloat16),
            pltpu.VMEM((DA_V_DIM, 2 * blk), jnp.float32),
        ],
        in_specs=[
            pl.BlockSpec((2 * DA_QK_DIM, blk), lambda b, h, i: (h, b * nq + i)),
            pl.BlockSpec((S, V7X_LANES), lambda b, h, i: (b, h)),
            pl.BlockSpec((S, V7X_LANES), lambda b, h, i: (b, h)),
            pl.BlockSpec((1, 1), lambda b, h, i: (0, 0)),
            pl.BlockSpec((DA_V_DIM, V7X_LANES), lambda b, h, i: (0, 0)),
        ],
        out_specs=pl.BlockSpec((DA_V_DIM, blk), lambda b, h, i: (h, b * nq + i)),
        out_shape=jax.ShapeDtypeStruct((DA_V_W, B * S), jnp.bfloat16),
        compiler_params=pltpu.CompilerParams(
            dimension_semantics=("parallel", "parallel", "arbitrary"),
            vmem_limit_bytes=V7X_VMEM_LIMIT_BYTES),
        name="diff_attn",
    )(dq, dk, dv, lam, subln_w)


def _diff_attn_bounded_kernel(q_ref, k_ref, v_ref, lam_ref, sw_ref, o_ref, qst_ref, p_ref, *, seq):
    tile = V7X_MXU_WIDTH
    n_tok_tile = seq // tile

    qt = q_ref[...]
    dim = lax.broadcasted_iota(jnp.int32, qt.shape, 0)
    qst_ref[0] = jnp.where(dim < DA_QK_DIM, qt, jnp.zeros_like(qt))
    qst_ref[1] = jnp.where(dim >= DA_QK_DIM, qt, jnp.zeros_like(qt))
    sw = jnp.concatenate([sw_ref[...]] * (tile // V7X_LANES), axis=1)

    items = [(j, m) for j in reversed(range(n_tok_tile)) for m in range(2)]
    slots = p_ref.shape[0]

    def probs(n):
        j, m = items[n]
        keys = (j + 1) * tile
        s = _dot(k_ref[:keys, :], qst_ref[m, :, j * tile:(j + 1) * tile])
        krow = lax.broadcasted_iota(jnp.int32, (tile, tile), 0)
        qcol = lax.broadcasted_iota(jnp.int32, (tile, tile), 1)
        p_diag = jnp.exp2(jnp.where(krow <= qcol, s[keys - tile:], MASK_VALUE))
        p_ref[n % slots, keys - tile:keys] = _bf16(p_diag)
        sub = V7X_SUBLANES
        part = jnp.sum(p_diag.reshape(tile // sub, sub, tile), axis=0)
        if keys > tile:
            p_full = jnp.exp2(s[:keys - tile])
            p_ref[n % slots, :keys - tile] = _bf16(p_full)
            part = part + jnp.sum(p_full.reshape((keys - tile) // sub, sub, tile), axis=0)
        return jnp.sum(part, axis=0, keepdims=True)

    def weighted_values(n):
        j, _ = items[n]
        keys = (j + 1) * tile
        return _dot_tn(v_ref[:keys, :], p_ref[n % slots, :keys])

    lsum, out = {}, {}
    for n in range(len(items) + ATTN_PV_LAG):
        if n < len(items):
            lsum[n] = probs(n)
        done = n - ATTN_PV_LAG
        if done >= 0:
            out[done] = weighted_values(done) * (1.0 / lsum.pop(done))
            j, m = items[done]
            if m == 1:
                o = out.pop(done - 1) - lam_ref[...] * out.pop(done)
                o = o * lax.rsqrt(jnp.mean(o * o, axis=0, keepdims=True) + SUBLN_EPS)
                o_ref[:, j * tile:(j + 1) * tile] = _bf16(o * sw)


def _diff_attn_bounded(dq, dk, dv, lam, subln_w, B, S):
    assert S % V7X_MXU_WIDTH == 0
    kernel = functools.partial(_diff_attn_bounded_kernel, seq=S)
    return pl.pallas_call(
        kernel,
        grid=(B, DA_HEADS),
        scratch_shapes=[
            pltpu.VMEM((2, DA_V_DIM, S), jnp.bfloat16),
            pltpu.VMEM((2 * ATTN_PV_LAG + 2, S, V7X_MXU_WIDTH), jnp.bfloat16),
        ],
        in_specs=[
            pl.BlockSpec((2 * DA_QK_DIM, S), lambda b, h: (h, b)),
            pl.BlockSpec((S, V7X_LANES), lambda b, h: (b, h)),
            pl.BlockSpec((S, V7X_LANES), lambda b, h: (b, h)),
            pl.BlockSpec((1, 1), lambda b, h: (0, 0)),
            pl.BlockSpec((DA_V_DIM, V7X_LANES), lambda b, h: (0, 0)),
        ],
        out_specs=pl.BlockSpec((DA_V_DIM, S), lambda b, h: (h, b)),
        out_shape=jax.ShapeDtypeStruct((DA_V_W, B * S), jnp.bfloat16),
        compiler_params=pltpu.CompilerParams(
            dimension_semantics=("parallel", "parallel"),
            vmem_limit_bytes=V7X_VMEM_LIMIT_BYTES),
        name="diff_attn_bounded",
    )(dq, dk, dv, lam, subln_w)


def _retention_kernel(q_ref, k_ref, v_ref, g_ref, lg_ref, o_ref, state_ref, decay_ref, xi_ref, zeta_ref,
                      *, chunk, n_chunks):
    f32 = jnp.float32

    @pl.when(pl.program_id(1) == 0)
    def _():
        state_ref[...] = jnp.zeros(state_ref.shape, f32)
        row = lax.broadcasted_iota(jnp.int32, (chunk, chunk), 0)
        col = lax.broadcasted_iota(jnp.int32, (chunk, chunk), 1)
        rel = (row - col).astype(f32)
        idx = lax.broadcasted_iota(jnp.int32, (chunk, V7X_LANES), 0).astype(f32)
        for h in range(RET_HEADS):
            lg = lg_ref[h, :, 0:1]
            decay_ref[h] = jnp.where(rel >= 0, jnp.exp(lg * jnp.maximum(rel, 0.0)), 0.0)
            xi_ref[h] = jnp.exp(lg * (idx + 1.0))
            zeta_ref[h] = jnp.exp(lg * (chunk - 1.0 - idx))

    def head_chunk(c, h):
        rows = pl.ds(c * chunk, chunk)
        qk_cols = slice(h * RET_QK_DIM, (h + 1) * RET_QK_DIM)
        v_cols = slice(h * RET_V_DIM, (h + 1) * RET_V_DIM)
        lg = lg_ref[h, :, 0:1]
        qc = q_ref[rows, qk_cols]
        kc = k_ref[rows, qk_cols]
        vc = v_ref[rows, v_cols]
        scores = _bf16(_dot_nt(qc, kc) * decay_ref[h])
        xi = jnp.concatenate([xi_ref[h]] * (RET_V_DIM // V7X_LANES), axis=1)
        o = _dot(scores, vc) + _dot(qc, _bf16(state_ref[h])) * xi
        kz = _bf16(kc.astype(f32) * zeta_ref[h])
        state_ref[h] = jnp.exp(lg * float(chunk)) * state_ref[h] + _dot_tn(kz, vc)
        d = o - jnp.mean(o, axis=-1, keepdims=True)
        var = jnp.mean(d * d, axis=-1, keepdims=True)
        on = d * lax.rsqrt(var + GN_EPS)
        o_ref[rows, v_cols] = _bf16(g_ref[rows, v_cols].astype(f32) * on)

    for c in range(n_chunks):
        for h in range(RET_HEADS):
            head_chunk(c, h)


def _retention(rq, rk, rv, rg, lg_rows, B, S):
    chunk = min(RET_CHUNK, S)
    sblk = min(RET_SEQ_BLOCK, S)
    assert S % sblk == 0 and sblk % chunk == 0
    nsb = S // sblk
    kernel = functools.partial(_retention_kernel, chunk=chunk, n_chunks=sblk // chunk)
    qk_spec = pl.BlockSpec((sblk, RET_QK_W), lambda b, s: (b * nsb + s, 0))
    v_spec = pl.BlockSpec((sblk, RET_V_W), lambda b, s: (b * nsb + s, 0))
    return pl.pallas_call(
        kernel,
        grid=(B, nsb),
        scratch_shapes=[
            pltpu.VMEM((RET_HEADS, RET_QK_DIM, RET_V_DIM), jnp.float32),
            pltpu.VMEM((RET_HEADS, chunk, chunk), jnp.float32),
            pltpu.VMEM((RET_HEADS, chunk, V7X_LANES), jnp.float32),
            pltpu.VMEM((RET_HEADS, chunk, V7X_LANES), jnp.float32),
        ],
        in_specs=[
            qk_spec, qk_spec, v_spec, v_spec,
            pl.BlockSpec((RET_HEADS, 1, V7X_LANES), lambda b, s: (0, 0, 0)),
        ],
        out_specs=v_spec,
        out_shape=jax.ShapeDtypeStruct((B * S, RET_V_W), jnp.bfloat16),
        compiler_params=pltpu.CompilerParams(
            dimension_semantics=("parallel", "arbitrary"),
            vmem_limit_bytes=V7X_VMEM_LIMIT_BYTES),
        name="retention",
    )(rq, rk, rv, rg, lg_rows)


def _out_mlp_kernel(x_ref, oa_ref, or_ref, gl_ref, wa_ref, wr_ref, wo_ref, n2w_ref, w1_ref, w2_ref,
                    o_ref):
    ya = _dot_tn(oa_ref[...], wa_ref[...])
    yr = _dot(or_ref[...], wr_ref[...])
    ga = jax.nn.sigmoid(gl_ref[:, :D_MODEL].astype(jnp.float32))
    gr = jax.nn.sigmoid(gl_ref[:, D_MODEL:].astype(jnp.float32))
    merged = _bf16(ga * ya + gr * yr)
    x1 = x_ref[...] + _dot(merged, wo_ref[...])
    h2 = x1 * lax.rsqrt(jnp.mean(x1 * x1, axis=-1, keepdims=True) + NORM_EPS) * n2w_ref[...]
    h2 = _bf16(h2)
    acc = x1
    for c in range(D_FF // MLP_FF_CHUNK):
        lo = c * MLP_FF_CHUNK
        hid = jnp.maximum(_dot(h2, w1_ref[:, lo:lo + MLP_FF_CHUNK]), 0.0)
        acc = acc + _dot(_bf16(hid * hid), w2_ref[lo:lo + MLP_FF_CHUNK, :])
    o_ref[...] = acc


def _out_mlp(x2, oa, orr, gl, wa, wr, wo, n2w, w1, w2):
    T = x2.shape[0]
    rows = min(MLP_ROWS, T)
    assert T % rows == 0
    row_spec = lambda width: pl.BlockSpec((rows, width), lambda i: (i, 0))
    weight_spec = lambda shape: pl.BlockSpec(shape, lambda i: (0, 0), pipeline_mode=pl.Buffered(1))
    return pl.pallas_call(
        _out_mlp_kernel,
        grid=(T // rows,),
        in_specs=[
            row_spec(D_MODEL), pl.BlockSpec((DA_V_W, rows), lambda i: (0, i)),
            row_spec(RET_V_W), row_spec(GATE_W),
            weight_spec((DA_V_W, D_MODEL)), weight_spec((RET_V_W, D_MODEL)),
            weight_spec((D_MODEL, D_MODEL)),
            pl.BlockSpec((1, D_MODEL), lambda i: (0, 0)),
            weight_spec((D_MODEL, D_FF)), weight_spec((D_FF, D_MODEL)),
        ],
        out_specs=row_spec(D_MODEL),
        out_shape=jax.ShapeDtypeStruct((T, D_MODEL), jnp.float32),
        compiler_params=pltpu.CompilerParams(
            dimension_semantics=("parallel",), vmem_limit_bytes=V7X_VMEM_LIMIT_BYTES),
        name="out_mlp",
    )(x2, oa, orr, gl, wa, wr, wo, n2w, w1, w2)


def kernel(x, positions, norm1_w, w_in, q_norm_w, k_norm_w, lambda_q1, lambda_k1, lambda_q2,
           lambda_k2, da_subln_w, w_da_branch, ret_gn_w, w_ret_branch, w_out, norm2_w, w_mlp_in,
           w_mlp_out):
    B, S, _ = x.shape
    T = B * S
    f32 = jnp.float32
    x2 = x.reshape(T, D_MODEL)

    q_scale = (DA_QK_DIM ** -0.5) * LOG2_E
    qkw = jnp.stack([jnp.tile(q_norm_w[0].astype(f32) * q_scale, DA_QK_W // DA_QK_DIM),
                     jnp.tile(k_norm_w[0].astype(f32), DA_QK_W // DA_QK_DIM)])
    blk = np.arange(V7X_MXU_WIDTH) // DA_QK_DIM
    ones_bd = jnp.asarray(blk[:, None] == blk[None, :], jnp.bfloat16)

    in_table = jnp.logical_and(jnp.min(positions) >= 0, jnp.max(positions) < ROT_FINE * ROT_COARSE)
    in_proj_args = (x2, positions.astype(f32), norm1_w[0].astype(f32).reshape(1, D_MODEL),
                    _bf16(w_in[0]), qkw, ones_bd)
    dq, dk, dv, rq, rk, rv, rg, gl = lax.cond(
        in_table, functools.partial(_in_proj, table_trig=True),
        functools.partial(_in_proj, table_trig=False), *in_proj_args)

    lam = (jnp.exp(jnp.sum(lambda_q1[0].astype(f32) * lambda_k1[0].astype(f32)))
           - jnp.exp(jnp.sum(lambda_q2[0].astype(f32) * lambda_k2[0].astype(f32)))
           + LAMBDA_INIT).reshape(1, 1)
    subln = jnp.broadcast_to((da_subln_w[0].astype(f32) * (1.0 - LAMBDA_INIT))[:, None],
                             (DA_V_DIM, V7X_LANES))
    logit_bound = (DA_QK_DIM * jnp.max(jnp.abs(qkw[0])) * jnp.max(jnp.abs(qkw[1]))
                   * ATTN_BOUND_MARGIN)
    oa = lax.cond(logit_bound <= ATTN_BOUNDED_LOGIT,
                  functools.partial(_diff_attn_bounded, B=B, S=S),
                  functools.partial(_diff_attn, B=B, S=S),
                  dq, dk, dv, lam, subln)

    log_gamma = np.log1p(-np.exp2(-5.0 - np.arange(RET_HEADS, dtype=np.float64)))
    lg_rows = jnp.asarray(np.broadcast_to(log_gamma[:, None, None], (RET_HEADS, 1, V7X_LANES)), f32)
    orr = _retention(rq, rk, rv, rg, lg_rows, B, S)

    w_ret = _bf16(ret_gn_w[0].astype(f32)[:, None] * w_ret_branch[0].astype(f32))
    out = _out_mlp(x2, oa, orr, gl, _bf16(w_da_branch[0]), w_ret, _bf16(w_out[0]),
                   norm2_w[0].astype(f32).reshape(1, D_MODEL), _bf16(w_mlp_in[0]),
                   _bf16(w_mlp_out[0]))
    return out.reshape(B, S, D_MODEL)
```

```python
import functools
import math

import jax
import jax.numpy as jnp
import numpy as np
from jax import lax
from jax.experimental import pallas as pl
from jax.experimental.pallas import tpu as pltpu

D_MODEL = 1024
DA_HEADS = 4
DA_QK_DIM = 64
DA_V_DIM = 2 * DA_QK_DIM
DA_ROT_DIM = DA_QK_DIM // 4
ROPE_THETA = 500000.0
RET_HEADS = 4
RET_QK_DIM = 128
RET_V_DIM = 256
RET_THETA_BASE = 10000.0
D_FF = 4 * D_MODEL
NORM_EPS = 1e-6
SUBLN_EPS = 1e-5
GN_EPS = 1e-5
MASK_VALUE = -1e30
LAMBDA_INIT = 0.8 - 0.6 * math.exp(-0.3 * 0)

DA_QK_W = DA_HEADS * 2 * DA_QK_DIM
DA_V_W = DA_HEADS * DA_V_DIM
RET_QK_W = RET_HEADS * RET_QK_DIM
RET_V_W = RET_HEADS * RET_V_DIM
GATE_W = 2 * D_MODEL
SECTION_WIDTHS = (DA_QK_W, DA_QK_W, DA_V_W, RET_QK_W, RET_QK_W, RET_V_W, RET_V_W, GATE_W)
SECTION_STARTS = tuple(int(v) for v in np.cumsum((0,) + SECTION_WIDTHS[:-1]))
IN_W = sum(SECTION_WIDTHS)

V7X_LANES = 128
V7X_SUBLANES = 8
V7X_MXU_WIDTH = 256
V7X_VMEM_LIMIT_BYTES = 56 * 1024 * 1024

LOG2_E = math.log2(math.e)

PROJ_ROWS = 512
ROT_FINE = 64
ROT_COARSE = 128
ATTN_BLOCK = 512
ATTN_BOUNDED_LOGIT = 60.0
ATTN_PV_LAG = 4
ATTN_BOUND_MARGIN = 1.0 + 2.0 ** -6
RET_CHUNK = 256
RET_SEQ_BLOCK = 1024
MLP_ROWS = 512
MLP_FF_CHUNK = 1024


def _bf16(t):
    return t.astype(jnp.bfloat16)


def _dot(a, b):
    return jnp.dot(a, b, preferred_element_type=jnp.float32)


def _dot_nt(a, b):
    return lax.dot_general(a, b, (((1,), (1,)), ((), ())), preferred_element_type=jnp.float32)


def _dot_tn(a, b):
    return lax.dot_general(a, b, (((0,), (0,)), ((), ())), preferred_element_type=jnp.float32)


def _split_bf16(t):
    hi = _bf16(t)
    lo = _bf16(t - hi.astype(jnp.float32))
    return jnp.concatenate([hi, lo], axis=0)


def _in_proj_kernel(x_ref, pos_ref, n1w_ref, w_ref, qkw_ref, freq_ref, coarse_ref, fine_ref, eda_ref,
                    ert_ref, ones_ref,
                    dq_ref, dk_ref, dv_ref, rq_ref, rk_ref, rv_ref, rg_ref, gl_ref, *, table_trig):
    x = x_ref[...]
    h = x * lax.rsqrt(jnp.mean(x * x, axis=-1, keepdims=True) + NORM_EPS) * n1w_ref[...]

    def section(idx):
        lo = SECTION_STARTS[idx]
        return _dot(h, w_ref[:, lo:lo + SECTION_WIDTHS[idx]])

    rows = x.shape[0]
    n_da = DA_ROT_DIM // 2
    n_freq = freq_ref.shape[0]
    pos = pos_ref[0]
    if table_trig:
        a = jnp.floor(pos * (1.0 / ROT_FINE))
        b = pos - ROT_FINE * a
        pick_a = jnp.where(lax.broadcasted_iota(jnp.int32, (ROT_COARSE, rows), 0).astype(jnp.float32) == a,
                           1.0, 0.0).astype(jnp.bfloat16)
        pick_b = jnp.where(lax.broadcasted_iota(jnp.int32, (ROT_FINE, rows), 0).astype(jnp.float32) == b,
                           1.0, 0.0).astype(jnp.bfloat16)
        cs_a = _dot(coarse_ref[...], jnp.concatenate([pick_a, pick_a], axis=0))
        cs_b = _dot(fine_ref[...], jnp.concatenate([pick_b, pick_b], axis=0))
        cos_t = cs_a[:n_freq] * cs_b[:n_freq] - cs_a[n_freq:] * cs_b[n_freq:]
        sin_t = cs_a[n_freq:] * cs_b[:n_freq] + cs_a[:n_freq] * cs_b[n_freq:]
    else:
        ang = jnp.concatenate([freq_ref[...]] * (rows // V7X_LANES), axis=1) * pos
        cos_t, sin_t = jnp.cos(ang), jnp.sin(ang)
    tab_da = _dot_tn(_split_bf16(jnp.concatenate([cos_t[:n_da], sin_t[:n_da]], axis=0)),
                     eda_ref[...])
    tab_rt = _dot_tn(_split_bf16(jnp.concatenate([cos_t[n_da:], sin_t[n_da:]], axis=0)),
                     ert_ref[...])
    lane1 = lax.broadcasted_iota(jnp.int32, (1, V7X_LANES), 1)
    unrotated = jnp.where(lane1 % DA_QK_DIM >= DA_ROT_DIM, 1.0, 0.0)

    cos_rt = tab_rt[:, :V7X_LANES]
    sin_rt = tab_rt[:, V7X_LANES:]
    for idx, out_ref, scale in ((3, rq_ref, 1.0), (4, rk_ref, RET_QK_DIM ** -0.5)):
        t = section(idx)
        for hh in range(RET_HEADS):
            th = t[:, hh * RET_QK_DIM:(hh + 1) * RET_QK_DIM]
            rot = th * cos_rt + pltpu.roll(th, RET_QK_DIM // 2, axis=1) * sin_rt
            if scale != 1.0:
                rot = rot * scale
            out_ref[:, hh * RET_QK_DIM:(hh + 1) * RET_QK_DIM] = _bf16(rot)

    reps = DA_QK_W // V7X_LANES
    cos_da = jnp.concatenate([tab_da[:, :V7X_LANES] + unrotated] * reps, axis=1)
    sin_da = jnp.concatenate([tab_da[:, V7X_LANES:]] * reps, axis=1)
    lane = lax.broadcasted_iota(jnp.int32, (rows, DA_QK_W), 1)
    first_half = (lane % DA_QK_DIM) < (DA_ROT_DIM // 2)
    ones_bd = ones_ref[...]
    for idx, out_ref in ((0, dq_ref), (1, dk_ref)):
        t = section(idx)
        tt = _bf16(t * t)
        ms = jnp.concatenate(
            [_dot(tt[:, g:g + V7X_MXU_WIDTH], ones_bd) for g in range(0, DA_QK_W, V7X_MXU_WIDTH)],
            axis=1) * (1.0 / DA_QK_DIM)
        t = t * lax.rsqrt(ms + NORM_EPS) * qkw_ref[idx:idx + 1, :]
        partner = jnp.where(first_half,
                            pltpu.roll(t, DA_QK_W - DA_ROT_DIM // 2, axis=1),
                            pltpu.roll(t, DA_ROT_DIM // 2, axis=1))
        rot = t * cos_da + partner * sin_da
        out_ref[...] = _bf16(rot.T if idx == 0 else rot)

    dv_ref[...] = _bf16(section(2))
    rv_ref[...] = _bf16(section(5))
    rg = section(6)
    rg_ref[...] = _bf16(rg * jax.nn.sigmoid(rg))
    gl_ref[...] = _bf16(section(7))


def _rotary_constants():
    n_da, n_rt = DA_ROT_DIM // 2, RET_QK_DIM // 2
    inv_da = ROPE_THETA ** (-jnp.arange(0, DA_ROT_DIM, 2, dtype=jnp.float32) / DA_ROT_DIM)
    inv_rt = 1.0 / (RET_THETA_BASE ** jnp.linspace(0.0, 1.0, n_rt, dtype=jnp.float32))
    freqs = jnp.concatenate([inv_da, inv_rt])
    freq = jnp.broadcast_to(freqs[:, None], (n_da + n_rt, V7X_LANES))

    def trig_table(multiples):
        keep_11_bits = jnp.uint32(0xFFFFFFFF << (24 - 11) & 0xFFFFFFFF)
        bits = lax.bitcast_convert_type(freqs, jnp.uint32) & keep_11_bits
        f_hi = lax.bitcast_convert_type(bits, jnp.float32)
        big = f_hi[:, None] * multiples[None, :]
        small = (freqs - f_hi)[:, None] * multiples[None, :]
        cos = jnp.cos(big) * jnp.cos(small) - jnp.sin(big) * jnp.sin(small)
        sin = jnp.sin(big) * jnp.cos(small) + jnp.cos(big) * jnp.sin(small)
        t = jnp.concatenate([cos, sin], axis=0)
        hi = _bf16(t)
        return jnp.concatenate([hi, _bf16(t - hi.astype(jnp.float32))], axis=1)

    coarse = trig_table(ROT_FINE * jnp.arange(ROT_COARSE, dtype=jnp.float32))
    fine = trig_table(jnp.arange(ROT_FINE, dtype=jnp.float32))

    def selection(n, period):
        e = np.zeros((4 * n, 2 * V7X_LANES), np.float32)
        for f in range(n):
            for base in range(0, V7X_LANES, period):
                for part in (0, 2 * n):
                    e[part + f, base + f] = e[part + f, base + n + f] = 1.0
                    e[part + n + f, V7X_LANES + base + f] = -1.0
                    e[part + n + f, V7X_LANES + base + n + f] = 1.0
        return jnp.asarray(e, jnp.bfloat16)

    return freq, coarse, fine, selection(n_da, DA_QK_DIM), selection(n_rt, RET_QK_DIM)


def _in_proj(x2, pos, n1w, w_in, qkw, ones_bd, table_trig):
    T = x2.shape[0]
    rows = min(PROJ_ROWS, T)
    assert T % rows == 0 and rows % V7X_LANES == 0
    freq, coarse, fine, e_da, e_rt = _rotary_constants()
    pos = pos.reshape(T // rows, 1, rows)
    row_spec = lambda width: pl.BlockSpec((rows, width), lambda i: (i, 0))
    const_spec = lambda shape: pl.BlockSpec(shape, lambda i: (0, 0))
    out_widths = SECTION_WIDTHS
    return pl.pallas_call(
        functools.partial(_in_proj_kernel, table_trig=table_trig),
        grid=(T // rows,),
        in_specs=[
            row_spec(D_MODEL),
            pl.BlockSpec((1, 1, rows), lambda i: (i, 0, 0)),
            const_spec((1, D_MODEL)),
            pl.BlockSpec((D_MODEL, IN_W), lambda i: (0, 0), pipeline_mode=pl.Buffered(1)),
            const_spec((2, DA_QK_W)),
            const_spec(freq.shape), const_spec(coarse.shape), const_spec(fine.shape),
            const_spec(e_da.shape), const_spec(e_rt.shape),
            const_spec((V7X_MXU_WIDTH, V7X_MXU_WIDTH)),
        ],
        out_specs=[pl.BlockSpec((DA_QK_W, rows), lambda i: (0, i))] + [row_spec(w) for w in out_widths[1:]],
        out_shape=[jax.ShapeDtypeStruct((DA_QK_W, T), jnp.bfloat16)]
        + [jax.ShapeDtypeStruct((T, w), jnp.bfloat16) for w in out_widths[1:]],
        compiler_params=pltpu.CompilerParams(
            dimension_semantics=("parallel",), vmem_limit_bytes=V7X_VMEM_LIMIT_BYTES),
        name="in_proj",
    )(x2, pos, n1w, w_in, qkw, freq, coarse, fine, e_da, e_rt, ones_bd)


def _diff_attn_kernel(q_ref, k_ref, v_ref, lam_ref, sw_ref, o_ref,
                      qst_ref, s_ref, p_ref, acc_ref, *, qb, kb):
    qi = pl.program_id(2)
    nq = 2 * qb
    n_strip = nq // V7X_LANES
    f32 = jnp.float32

    qt = q_ref[...]
    dim = lax.broadcasted_iota(jnp.int32, qt.shape, 0)
    qst_ref[:, :qb] = jnp.where(dim < DA_QK_DIM, qt, jnp.zeros_like(qt))
    qst_ref[:, qb:] = jnp.where(dim >= DA_QK_DIM, qt, jnp.zeros_like(qt))
    acc_ref[...] = jnp.zeros(acc_ref.shape, f32)
    p_ref[1] = jnp.zeros(p_ref.shape[1:], p_ref.dtype)

    def stage_a(t, slot):
        start = pl.multiple_of(t * kb, kb)
        s_ref[slot] = _dot(k_ref[pl.ds(start, kb), :], qst_ref[...])

    def stage_b(slot, m, l, key_offset):
        m_out, l_out, a_out = [], [], []
        for c in range(n_strip):
            cols = slice(c * V7X_LANES, (c + 1) * V7X_LANES)
            st = s_ref[slot, :, cols]
            if key_offset is not None:
                krow = lax.broadcasted_iota(jnp.int32, st.shape, 0) + key_offset
                qcol = lax.broadcasted_iota(jnp.int32, st.shape, 1) + (c * V7X_LANES) % qb
                st = jnp.where(krow <= qcol, st, MASK_VALUE)
            m_new = jnp.maximum(m[c], jnp.max(st, axis=0, keepdims=True))
            alpha = jnp.exp2(m[c] - m_new)
            p = jnp.exp2(st - m_new)
            l_out.append(alpha * l[c] + jnp.sum(p, axis=0, keepdims=True))
            p_ref[slot, :, cols] = _bf16(p)
            m_out.append(m_new)
            a_out.append(alpha)
        return tuple(m_out), tuple(l_out), tuple(a_out)

    def stage_c(t, slot, alpha):
        start = pl.multiple_of(jnp.maximum(t, 0) * kb, kb)
        vt = v_ref[pl.ds(start, kb), :]
        for c in range(0, n_strip, 2):
            cols = slice(c * V7X_LANES, (c + 2) * V7X_LANES)
            a = jnp.concatenate([alpha[c], alpha[c + 1]], axis=1)
            acc_ref[:, cols] = a * acc_ref[:, cols] + _dot_tn(vt, p_ref[slot, :, cols])

    def body(u, carry):
        m, l, alpha = carry
        t = 2 * u
        stage_c(t - 1, 1, alpha)
        stage_a(t + 1, 1)
        m, l, alpha = stage_b(0, m, l, None)
        stage_c(t, 0, alpha)
        stage_a(t + 2, 0)
        m, l, alpha = stage_b(1, m, l, None)
        return m, l, alpha

    row = lambda v: tuple(jnp.full((1, V7X_LANES), v, f32) for _ in range(n_strip))
    stage_a(0, 0)
    m, l, alpha = lax.fori_loop(0, qi, body, (row(MASK_VALUE), row(0.0), row(1.0)))
    t = 2 * qi
    stage_c(t - 1, 1, alpha)
    stage_a(t + 1, 1)
    m, l, alpha = stage_b(0, m, l, 0)
    stage_c(t, 0, alpha)
    m, l, alpha = stage_b(1, m, l, kb)
    stage_c(t + 1, 1, alpha)

    o = acc_ref[...] / jnp.concatenate(l, axis=1)
    o = o[:, :qb] - lam_ref[...] * o[:, qb:]
    o = o * lax.rsqrt(jnp.mean(o * o, axis=0, keepdims=True) + SUBLN_EPS)
    o_ref[...] = _bf16(o * jnp.concatenate([sw_ref[...]] * (qb // V7X_LANES), axis=1))


def _diff_attn(dq, dk, dv, lam, subln_w, B, S):
    blk = min(ATTN_BLOCK, S)
    kb = blk // 2
    assert S % blk == 0 and kb % V7X_LANES == 0
    nq = S // blk
    kernel = functools.partial(_diff_attn_kernel, qb=blk, kb=kb)
    return pl.pallas_call(
        kernel,
        grid=(B, DA_HEADS, nq),
        scratch_shapes=[
            pltpu.VMEM((DA_V_DIM, 2 * blk), jnp.bfloat16),
            pltpu.VMEM((2, kb, 2 * blk), jnp.float32),
            pltpu.VMEM((2, kb, 2 * blk), jnp.bfloat16),
            pltpu.VMEM((DA_V_DIM, 2 * blk), jnp.float32),
        ],
        in_specs=[
            pl.BlockSpec((2 * DA_QK_DIM, blk), lambda b, h, i: (h, b * nq + i)),
            pl.BlockSpec((S, V7X_LANES), lambda b, h, i: (b, h)),
            pl.BlockSpec((S, V7X_LANES), lambda b, h, i: (b, h)),
            pl.BlockSpec((1, 1), lambda b, h, i: (0, 0)),
            pl.BlockSpec((DA_V_DIM, V7X_LANES), lambda b, h, i: (0, 0)),
        ],
        out_specs=pl.BlockSpec((DA_V_DIM, blk), lambda b, h, i: (h, b * nq + i)),
        out_shape=jax.ShapeDtypeStruct((DA_V_W, B * S), jnp.bfloat16),
        compiler_params=pltpu.CompilerParams(
            dimension_semantics=("parallel", "parallel", "arbitrary"),
            vmem_limit_bytes=V7X_VMEM_LIMIT_BYTES),
        name="diff_attn",
    )(dq, dk, dv, lam, subln_w)


def _diff_attn_bounded_kernel(q_ref, k_ref, v_ref, lam_ref, sw_ref, o_ref, qst_ref, p_ref, *, seq):
    tile = V7X_MXU_WIDTH
    n_tok_tile = seq // tile

    qt = q_ref[...]
    dim = lax.broadcasted_iota(jnp.int32, qt.shape, 0)
    qst_ref[0] = jnp.where(dim < DA_QK_DIM, qt, jnp.zeros_like(qt))
    qst_ref[1] = jnp.where(dim >= DA_QK_DIM, qt, jnp.zeros_like(qt))
    sw = jnp.concatenate([sw_ref[...]] * (tile // V7X_LANES), axis=1)

    items = [(j, m) for j in reversed(range(n_tok_tile)) for m in range(2)]
    slots = p_ref.shape[0]

    def probs(n):
        j, m = items[n]
        keys = (j + 1) * tile
        s = _dot(k_ref[:keys, :], qst_ref[m, :, j * tile:(j + 1) * tile])
        krow = lax.broadcasted_iota(jnp.int32, (tile, tile), 0)
        qcol = lax.broadcasted_iota(jnp.int32, (tile, tile), 1)
        p_diag = jnp.exp2(jnp.where(krow <= qcol, s[keys - tile:], MASK_VALUE))
        p_ref[n % slots, keys - tile:keys] = _bf16(p_diag)
        sub = V7X_SUBLANES
        part = jnp.sum(p_diag.reshape(tile // sub, sub, tile), axis=0)
        if keys > tile:
            p_full = jnp.exp2(s[:keys - tile])
            p_ref[n % slots, :keys - tile] = _bf16(p_full)
            part = part + jnp.sum(p_full.reshape((keys - tile) // sub, sub, tile), axis=0)
        return jnp.sum(part, axis=0, keepdims=True)

    def weighted_values(n):
        j, _ = items[n]
        keys = (j + 1) * tile
        return _dot_tn(v_ref[:keys, :], p_ref[n % slots, :keys])

    lsum, out = {}, {}
    for n in range(len(items) + ATTN_PV_LAG):
        if n < len(items):
            lsum[n] = probs(n)
        done = n - ATTN_PV_LAG
        if done >= 0:
            out[done] = weighted_values(done) * (1.0 / lsum.pop(done))
            j, m = items[done]
            if m == 1:
                o = out.pop(done - 1) - lam_ref[...] * out.pop(done)
                o = o * lax.rsqrt(jnp.mean(o * o, axis=0, keepdims=True) + SUBLN_EPS)
                o_ref[:, j * tile:(j + 1) * tile] = _bf16(o * sw)


def _diff_attn_bounded(dq, dk, dv, lam, subln_w, B, S):
    assert S % V7X_MXU_WIDTH == 0
    kernel = functools.partial(_diff_attn_bounded_kernel, seq=S)
    return pl.pallas_call(
        kernel,
        grid=(B, DA_HEADS),
        scratch_shapes=[
            pltpu.VMEM((2, DA_V_DIM, S), jnp.bfloat16),
            pltpu.VMEM((2 * ATTN_PV_LAG + 2, S, V7X_MXU_WIDTH), jnp.bfloat16),
        ],
        in_specs=[
            pl.BlockSpec((2 * DA_QK_DIM, S), lambda b, h: (h, b)),
            pl.BlockSpec((S, V7X_LANES), lambda b, h: (b, h)),
            pl.BlockSpec((S, V7X_LANES), lambda b, h: (b, h)),
            pl.BlockSpec((1, 1), lambda b, h: (0, 0)),
            pl.BlockSpec((DA_V_DIM, V7X_LANES), lambda b, h: (0, 0)),
        ],
        out_specs=pl.BlockSpec((DA_V_DIM, S), lambda b, h: (h, b)),
        out_shape=jax.ShapeDtypeStruct((DA_V_W, B * S), jnp.bfloat16),
        compiler_params=pltpu.CompilerParams(
            dimension_semantics=("parallel", "parallel"),
            vmem_limit_bytes=V7X_VMEM_LIMIT_BYTES),
        name="diff_attn_bounded",
    )(dq, dk, dv, lam, subln_w)


def _retention_kernel(q_ref, k_ref, v_ref, g_ref, lg_ref, o_ref, state_ref, decay_ref, xi_ref, zeta_ref,
                      *, chunk, n_chunks):
    f32 = jnp.float32

    @pl.when(pl.program_id(1) == 0)
    def _():
        state_ref[...] = jnp.zeros(state_ref.shape, f32)
        row = lax.broadcasted_iota(jnp.int32, (chunk, chunk), 0)
        col = lax.broadcasted_iota(jnp.int32, (chunk, chunk), 1)
        rel = (row - col).astype(f32)
        idx = lax.broadcasted_iota(jnp.int32, (chunk, V7X_LANES), 0).astype(f32)
        for h in range(RET_HEADS):
            lg = lg_ref[h, :, 0:1]
            decay_ref[h] = jnp.where(rel >= 0, jnp.exp(lg * jnp.maximum(rel, 0.0)), 0.0)
            xi_ref[h] = jnp.exp(lg * (idx + 1.0))
            zeta_ref[h] = jnp.exp(lg * (chunk - 1.0 - idx))

    def head_chunk(c, h):
        rows = pl.ds(c * chunk, chunk)
        qk_cols = slice(h * RET_QK_DIM, (h + 1) * RET_QK_DIM)
        v_cols = slice(h * RET_V_DIM, (h + 1) * RET_V_DIM)
        lg = lg_ref[h, :, 0:1]
        qc = q_ref[rows, qk_cols]
        kc = k_ref[rows, qk_cols]
        vc = v_ref[rows, v_cols]
        scores = _bf16(_dot_nt(qc, kc) * decay_ref[h])
        xi = jnp.concatenate([xi_ref[h]] * (RET_V_DIM // V7X_LANES), axis=1)
        o = _dot(scores, vc) + _dot(qc, _bf16(state_ref[h])) * xi
        kz = _bf16(kc.astype(f32) * zeta_ref[h])
        state_ref[h] = jnp.exp(lg * float(chunk)) * state_ref[h] + _dot_tn(kz, vc)
        d = o - jnp.mean(o, axis=-1, keepdims=True)
        var = jnp.mean(d * d, axis=-1, keepdims=True)
        on = d * lax.rsqrt(var + GN_EPS)
        o_ref[rows, v_cols] = _bf16(g_ref[rows, v_cols].astype(f32) * on)

    for c in range(n_chunks):
        for h in range(RET_HEADS):
            head_chunk(c, h)


def _retention(rq, rk, rv, rg, lg_rows, B, S):
    chunk = min(RET_CHUNK, S)
    sblk = min(RET_SEQ_BLOCK, S)
    assert S % sblk == 0 and sblk % chunk == 0
    nsb = S // sblk
    kernel = functools.partial(_retention_kernel, chunk=chunk, n_chunks=sblk // chunk)
    qk_spec = pl.BlockSpec((sblk, RET_QK_W), lambda b, s: (b * nsb + s, 0))
    v_spec = pl.BlockSpec((sblk, RET_V_W), lambda b, s: (b * nsb + s, 0))
    return pl.pallas_call(
        kernel,
        grid=(B, nsb),
        scratch_shapes=[
            pltpu.VMEM((RET_HEADS, RET_QK_DIM, RET_V_DIM), jnp.float32),
            pltpu.VMEM((RET_HEADS, chunk, chunk), jnp.float32),
            pltpu.VMEM((RET_HEADS, chunk, V7X_LANES), jnp.float32),
            pltpu.VMEM((RET_HEADS, chunk, V7X_LANES), jnp.float32),
        ],
        in_specs=[
            qk_spec, qk_spec, v_spec, v_spec,
            pl.BlockSpec((RET_HEADS, 1, V7X_LANES), lambda b, s: (0, 0, 0)),
        ],
        out_specs=v_spec,
        out_shape=jax.ShapeDtypeStruct((B * S, RET_V_W), jnp.bfloat16),
        compiler_params=pltpu.CompilerParams(
            dimension_semantics=("parallel", "arbitrary"),
            vmem_limit_bytes=V7X_VMEM_LIMIT_BYTES),
        name="retention",
    )(rq, rk, rv, rg, lg_rows)


def _out_mlp_kernel(x_ref, oa_ref, or_ref, gl_ref, wa_ref, wr_ref, wo_ref, n2w_ref, w1_ref, w2_ref,
                    o_ref):
    ya = _dot_tn(oa_ref[...], wa_ref[...])
    yr = _dot(or_ref[...], wr_ref[...])
    ga = jax.nn.sigmoid(gl_ref[:, :D_MODEL].astype(jnp.float32))
    gr = jax.nn.sigmoid(gl_ref[:, D_MODEL:].astype(jnp.float32))
    merged = _bf16(ga * ya + gr * yr)
    x1 = x_ref[...] + _dot(merged, wo_ref[...])
    h2 = x1 * lax.rsqrt(jnp.mean(x1 * x1, axis=-1, keepdims=True) + NORM_EPS) * n2w_ref[...]
    h2 = _bf16(h2)
    acc = x1
    for c in range(D_FF // MLP_FF_CHUNK):
        lo = c * MLP_FF_CHUNK
        hid = jnp.maximum(_dot(h2, w1_ref[:, lo:lo + MLP_FF_CHUNK]), 0.0)
        acc = acc + _dot(_bf16(hid * hid), w2_ref[lo:lo + MLP_FF_CHUNK, :])
    o_ref[...] = acc


def _out_mlp(x2, oa, orr, gl, wa, wr, wo, n2w, w1, w2):
    T = x2.shape[0]
    rows = min(MLP_ROWS, T)
    assert T % rows == 0
    row_spec = lambda width: pl.BlockSpec((rows, width), lambda i: (i, 0))
    weight_spec = lambda shape: pl.BlockSpec(shape, lambda i: (0, 0), pipeline_mode=pl.Buffered(1))
    return pl.pallas_call(
        _out_mlp_kernel,
        grid=(T // rows,),
        in_specs=[
            row_spec(D_MODEL), pl.BlockSpec((DA_V_W, rows), lambda i: (0, i)),
            row_spec(RET_V_W), row_spec(GATE_W),
            weight_spec((DA_V_W, D_MODEL)), weight_spec((RET_V_W, D_MODEL)),
            weight_spec((D_MODEL, D_MODEL)),
            pl.BlockSpec((1, D_MODEL), lambda i: (0, 0)),
            weight_spec((D_MODEL, D_FF)), weight_spec((D_FF, D_MODEL)),
        ],
        out_specs=row_spec(D_MODEL),
        out_shape=jax.ShapeDtypeStruct((T, D_MODEL), jnp.float32),
        compiler_params=pltpu.CompilerParams(
            dimension_semantics=("parallel",), vmem_limit_bytes=V7X_VMEM_LIMIT_BYTES),
        name="out_mlp",
    )(x2, oa, orr, gl, wa, wr, wo, n2w, w1, w2)


def kernel(x, positions, norm1_w, w_in, q_norm_w, k_norm_w, lambda_q1, lambda_k1, lambda_q2,
           lambda_k2, da_subln_w, w_da_branch, ret_gn_w, w_ret_branch, w_out, norm2_w, w_mlp_in,
           w_mlp_out):
    B, S, _ = x.shape
    T = B * S
    f32 = jnp.float32
    x2 = x.reshape(T, D_MODEL)

    q_scale = (DA_QK_DIM ** -0.5) * LOG2_E
    qkw = jnp.stack([jnp.tile(q_norm_w[0].astype(f32) * q_scale, DA_QK_W // DA_QK_DIM),
                     jnp.tile(k_norm_w[0].astype(f32), DA_QK_W // DA_QK_DIM)])
    blk = np.arange(V7X_MXU_WIDTH) // DA_QK_DIM
    ones_bd = jnp.asarray(blk[:, None] == blk[None, :], jnp.bfloat16)

    in_table = jnp.logical_and(jnp.min(positions) >= 0, jnp.max(positions) < ROT_FINE * ROT_COARSE)
    in_proj_args = (x2, positions.astype(f32), norm1_w[0].astype(f32).reshape(1, D_MODEL),
                    w_in[0].astype(f32), qkw, ones_bd)
    dq, dk, dv, rq, rk, rv, rg, gl = lax.cond(
        in_table, functools.partial(_in_proj, table_trig=True),
        functools.partial(_in_proj, table_trig=False), *in_proj_args)

    lam = (jnp.exp(jnp.sum(lambda_q1[0].astype(f32) * lambda_k1[0].astype(f32)))
           - jnp.exp(jnp.sum(lambda_q2[0].astype(f32) * lambda_k2[0].astype(f32)))
           + LAMBDA_INIT).reshape(1, 1)
    subln = jnp.broadcast_to((da_subln_w[0].astype(f32) * (1.0 - LAMBDA_INIT))[:, None],
                             (DA_V_DIM, V7X_LANES))
    logit_bound = (DA_QK_DIM * jnp.max(jnp.abs(qkw[0])) * jnp.max(jnp.abs(qkw[1]))
                   * ATTN_BOUND_MARGIN)
    oa = lax.cond(logit_bound <= ATTN_BOUNDED_LOGIT,
                  functools.partial(_diff_attn_bounded, B=B, S=S),
                  functools.partial(_diff_attn, B=B, S=S),
                  dq, dk, dv, lam, subln)

    log_gamma = np.log1p(-np.exp2(-5.0 - np.arange(RET_HEADS, dtype=np.float64)))
    lg_rows = jnp.asarray(np.broadcast_to(log_gamma[:, None, None], (RET_HEADS, 1, V7X_LANES)), f32)
    orr = _retention(rq, rk, rv, rg, lg_rows, B, S)

    w_ret = _bf16(ret_gn_w[0].astype(f32)[:, None] * w_ret_branch[0].astype(f32))
    out = _out_mlp(x2, oa, orr, gl, _bf16(w_da_branch[0]), w_ret, _bf16(w_out[0]),
                   norm2_w[0].astype(f32).reshape(1, D_MODEL), _bf16(w_mlp_in[0]),
                   _bf16(w_mlp_out[0]))
    return out.reshape(B, S, D_MODEL)
```

```python
import functools
import math

import jax
import jax.numpy as jnp
import numpy as np
from jax import lax
from jax.experimental import pallas as pl
from jax.experimental.pallas import tpu as pltpu

D_MODEL = 1024
DA_HEADS = 4
DA_QK_DIM = 64
DA_V_DIM = 2 * DA_QK_DIM
DA_ROT_DIM = DA_QK_DIM // 4
ROPE_THETA = 500000.0
RET_HEADS = 4
RET_QK_DIM = 128
RET_V_DIM = 256
RET_THETA_BASE = 10000.0
D_FF = 4 * D_MODEL
NORM_EPS = 1e-6
SUBLN_EPS = 1e-5
GN_EPS = 1e-5
MASK_VALUE = -1e30
LAMBDA_INIT = 0.8 - 0.6 * math.exp(-0.3 * 0)

DA_QK_W = DA_HEADS * 2 * DA_QK_DIM
DA_V_W = DA_HEADS * DA_V_DIM
RET_QK_W = RET_HEADS * RET_QK_DIM
RET_V_W = RET_HEADS * RET_V_DIM
GATE_W = 2 * D_MODEL
SECTION_WIDTHS = (DA_QK_W, DA_QK_W, DA_V_W, RET_QK_W, RET_QK_W, RET_V_W, RET_V_W, GATE_W)
SECTION_STARTS = tuple(int(v) for v in np.cumsum((0,) + SECTION_WIDTHS[:-1]))
IN_W = sum(SECTION_WIDTHS)

V7X_LANES = 128
V7X_SUBLANES = 8
V7X_MXU_WIDTH = 256
V7X_VMEM_LIMIT_BYTES = 56 * 1024 * 1024

LOG2_E = math.log2(math.e)

PROJ_ROWS = 512
IN_PROJ_GATE_PIECES = 4
ROT_FINE = 64
ROT_COARSE = 128
ATTN_BLOCK = 512
ATTN_BOUNDED_LOGIT = 60.0
ATTN_PV_LAG = 4
ATTN_BOUND_MARGIN = 1.0 + 2.0 ** -6
RET_CHUNK = 256
RET_SEQ_BLOCK = 1024
RET_BATCHES = 2
MLP_ROWS = 512
MLP_FF_CHUNK = 1024


def _bf16(t):
    return t.astype(jnp.bfloat16)


def _dot(a, b):
    return jnp.dot(a, b, preferred_element_type=jnp.float32)


def _dot_nt(a, b):
    return lax.dot_general(a, b, (((1,), (1,)), ((), ())), preferred_element_type=jnp.float32)


def _dot_tn(a, b):
    return lax.dot_general(a, b, (((0,), (0,)), ((), ())), preferred_element_type=jnp.float32)


def _split_bf16(t):
    hi = _bf16(t)
    lo = _bf16(t - hi.astype(jnp.float32))
    return jnp.concatenate([hi, lo], axis=0)


def _in_proj_kernel(x_ref, pos_ref, n1w_ref, w_ref, qkw_ref, freq_ref, coarse_ref, fine_ref, eda_ref,
                    ert_ref, ones_ref,
                    dq_ref, dk_ref, dv_ref, rq_ref, rk_ref, rv_ref, rg_ref, gl_ref, *, table_trig):
    x = x_ref[...]
    h = x * lax.rsqrt(jnp.mean(x * x, axis=-1, keepdims=True) + NORM_EPS) * n1w_ref[...]
    h = _bf16(h)

    def section(idx):
        lo = SECTION_STARTS[idx]
        return _dot(h, w_ref[:, lo:lo + SECTION_WIDTHS[idx]])

    rows = x.shape[0]
    n_da = DA_ROT_DIM // 2
    n_freq = freq_ref.shape[0]
    pos = pos_ref[0]
    if table_trig:
        a = jnp.floor(pos * (1.0 / ROT_FINE))
        b = pos - ROT_FINE * a
        pick_a = jnp.where(lax.broadcasted_iota(jnp.int32, (ROT_COARSE, rows), 0).astype(jnp.float32) == a,
                           1.0, 0.0).astype(jnp.bfloat16)
        pick_b = jnp.where(lax.broadcasted_iota(jnp.int32, (ROT_FINE, rows), 0).astype(jnp.float32) == b,
                           1.0, 0.0).astype(jnp.bfloat16)
        cs_a = _dot(coarse_ref[...], jnp.concatenate([pick_a, pick_a], axis=0))
        cs_b = _dot(fine_ref[...], jnp.concatenate([pick_b, pick_b], axis=0))
        cos_t = cs_a[:n_freq] * cs_b[:n_freq] - cs_a[n_freq:] * cs_b[n_freq:]
        sin_t = cs_a[n_freq:] * cs_b[:n_freq] + cs_a[:n_freq] * cs_b[n_freq:]
    else:
        ang = jnp.concatenate([freq_ref[...]] * (rows // V7X_LANES), axis=1) * pos
        cos_t, sin_t = jnp.cos(ang), jnp.sin(ang)
    tab_da = _dot_tn(_split_bf16(jnp.concatenate([cos_t[:n_da], sin_t[:n_da]], axis=0)),
                     eda_ref[...])
    tab_rt = _dot_tn(_split_bf16(jnp.concatenate([cos_t[n_da:], sin_t[n_da:]], axis=0)),
                     ert_ref[...])
    lane1 = lax.broadcasted_iota(jnp.int32, (1, V7X_LANES), 1)
    unrotated = jnp.where(lane1 % DA_QK_DIM >= DA_ROT_DIM, 1.0, 0.0)

    def plain(idx, out_ref, lo_col, width):
        lo = SECTION_STARTS[idx] + lo_col
        out_ref[:, lo_col:lo_col + width] = _bf16(_dot(h, w_ref[:, lo:lo + width]))

    cos_rt = tab_rt[:, :V7X_LANES]
    sin_rt = tab_rt[:, V7X_LANES:]

    def retention_qk(idx, out_ref, scale):
        t = section(idx)
        for hh in range(RET_HEADS):
            th = t[:, hh * RET_QK_DIM:(hh + 1) * RET_QK_DIM]
            rot = th * cos_rt + pltpu.roll(th, RET_QK_DIM // 2, axis=1) * sin_rt
            if scale != 1.0:
                rot = rot * scale
            out_ref[:, hh * RET_QK_DIM:(hh + 1) * RET_QK_DIM] = _bf16(rot)

    reps = DA_QK_W // V7X_LANES
    cos_da = jnp.concatenate([tab_da[:, :V7X_LANES] + unrotated] * reps, axis=1)
    sin_da = jnp.concatenate([tab_da[:, V7X_LANES:]] * reps, axis=1)
    lane = lax.broadcasted_iota(jnp.int32, (rows, DA_QK_W), 1)
    first_half = (lane % DA_QK_DIM) < (DA_ROT_DIM // 2)
    ones_bd = ones_ref[...]

    def diff_attn_qk(idx, out_ref):
        t = section(idx)
        tt = _bf16(t * t)
        ms = jnp.concatenate(
            [_dot(tt[:, g:g + V7X_MXU_WIDTH], ones_bd) for g in range(0, DA_QK_W, V7X_MXU_WIDTH)],
            axis=1) * (1.0 / DA_QK_DIM)
        t = t * lax.rsqrt(ms + NORM_EPS) * qkw_ref[idx:idx + 1, :]
        partner = jnp.where(first_half,
                            pltpu.roll(t, DA_QK_W - DA_ROT_DIM // 2, axis=1),
                            pltpu.roll(t, DA_ROT_DIM // 2, axis=1))
        rot = t * cos_da + partner * sin_da
        out_ref[...] = _bf16(rot.T if idx == 0 else rot)

    retention_qk(3, rq_ref, 1.0)
    plain(2, dv_ref, 0, DA_V_W)
    retention_qk(4, rk_ref, RET_QK_DIM ** -0.5)
    plain(5, rv_ref, 0, RET_V_W // 2)
    diff_attn_qk(0, dq_ref)
    plain(5, rv_ref, RET_V_W // 2, RET_V_W // 2)
    diff_attn_qk(1, dk_ref)
    for piece in range(IN_PROJ_GATE_PIECES):
        wg, wl = RET_V_W // IN_PROJ_GATE_PIECES, GATE_W // IN_PROJ_GATE_PIECES
        lo = SECTION_STARTS[6] + piece * wg
        rg = _dot(h, w_ref[:, lo:lo + wg])
        rg_ref[:, piece * wg:(piece + 1) * wg] = _bf16(rg * jax.nn.sigmoid(rg))
        plain(7, gl_ref, piece * wl, wl)


def _rotary_constants():
    n_da, n_rt = DA_ROT_DIM // 2, RET_QK_DIM // 2
    inv_da = ROPE_THETA ** (-jnp.arange(0, DA_ROT_DIM, 2, dtype=jnp.float32) / DA_ROT_DIM)
    inv_rt = 1.0 / (RET_THETA_BASE ** jnp.linspace(0.0, 1.0, n_rt, dtype=jnp.float32))
    freqs = jnp.concatenate([inv_da, inv_rt])
    freq = jnp.broadcast_to(freqs[:, None], (n_da + n_rt, V7X_LANES))

    def trig_table(multiples):
        keep_11_bits = jnp.uint32(0xFFFFFFFF << (24 - 11) & 0xFFFFFFFF)
        bits = lax.bitcast_convert_type(freqs, jnp.uint32) & keep_11_bits
        f_hi = lax.bitcast_convert_type(bits, jnp.float32)
        big = f_hi[:, None] * multiples[None, :]
        small = (freqs - f_hi)[:, None] * multiples[None, :]
        cos = jnp.cos(big) * jnp.cos(small) - jnp.sin(big) * jnp.sin(small)
        sin = jnp.sin(big) * jnp.cos(small) + jnp.cos(big) * jnp.sin(small)
        t = jnp.concatenate([cos, sin], axis=0)
        hi = _bf16(t)
        return jnp.concatenate([hi, _bf16(t - hi.astype(jnp.float32))], axis=1)

    coarse = trig_table(ROT_FINE * jnp.arange(ROT_COARSE, dtype=jnp.float32))
    fine = trig_table(jnp.arange(ROT_FINE, dtype=jnp.float32))

    def selection(n, period):
        e = np.zeros((4 * n, 2 * V7X_LANES), np.float32)
        for f in range(n):
            for base in range(0, V7X_LANES, period):
                for part in (0, 2 * n):
                    e[part + f, base + f] = e[part + f, base + n + f] = 1.0
                    e[part + n + f, V7X_LANES + base + f] = -1.0
                    e[part + n + f, V7X_LANES + base + n + f] = 1.0
        return jnp.asarray(e, jnp.bfloat16)

    return freq, coarse, fine, selection(n_da, DA_QK_DIM), selection(n_rt, RET_QK_DIM)


def _in_proj(x2, pos, n1w, w_in, qkw, ones_bd, table_trig):
    T = x2.shape[0]
    rows = min(PROJ_ROWS, T)
    assert T % rows == 0 and rows % V7X_LANES == 0
    freq, coarse, fine, e_da, e_rt = _rotary_constants()
    pos = pos.reshape(T // rows, 1, rows)
    row_spec = lambda width: pl.BlockSpec((rows, width), lambda i: (i, 0))
    const_spec = lambda shape: pl.BlockSpec(shape, lambda i: (0, 0))
    out_widths = SECTION_WIDTHS
    return pl.pallas_call(
        functools.partial(_in_proj_kernel, table_trig=table_trig),
        grid=(T // rows,),
        in_specs=[
            row_spec(D_MODEL),
            pl.BlockSpec((1, 1, rows), lambda i: (i, 0, 0)),
            const_spec((1, D_MODEL)),
            pl.BlockSpec((D_MODEL, IN_W), lambda i: (0, 0), pipeline_mode=pl.Buffered(1)),
            const_spec((2, DA_QK_W)),
            const_spec(freq.shape), const_spec(coarse.shape), const_spec(fine.shape),
            const_spec(e_da.shape), const_spec(e_rt.shape),
            const_spec((V7X_MXU_WIDTH, V7X_MXU_WIDTH)),
        ],
        out_specs=[pl.BlockSpec((DA_QK_W, rows), lambda i: (0, i))] + [row_spec(w) for w in out_widths[1:]],
        out_shape=[jax.ShapeDtypeStruct((DA_QK_W, T), jnp.bfloat16)]
        + [jax.ShapeDtypeStruct((T, w), jnp.bfloat16) for w in out_widths[1:]],
        compiler_params=pltpu.CompilerParams(
            dimension_semantics=("parallel",), vmem_limit_bytes=V7X_VMEM_LIMIT_BYTES),
        name="in_proj",
    )(x2, pos, n1w, w_in, qkw, freq, coarse, fine, e_da, e_rt, ones_bd)


def _diff_attn_kernel(q_ref, k_ref, v_ref, lam_ref, sw_ref, o_ref,
                      qst_ref, s_ref, p_ref, acc_ref, *, qb, kb):
    qi = pl.program_id(2)
    nq = 2 * qb
    n_strip = nq // V7X_LANES
    f32 = jnp.float32

    qt = q_ref[...]
    dim = lax.broadcasted_iota(jnp.int32, qt.shape, 0)
    qst_ref[:, :qb] = jnp.where(dim < DA_QK_DIM, qt, jnp.zeros_like(qt))
    qst_ref[:, qb:] = jnp.where(dim >= DA_QK_DIM, qt, jnp.zeros_like(qt))
    acc_ref[...] = jnp.zeros(acc_ref.shape, f32)
    p_ref[1] = jnp.zeros(p_ref.shape[1:], p_ref.dtype)

    def stage_a(t, slot):
        start = pl.multiple_of(t * kb, kb)
        s_ref[slot] = _dot(k_ref[pl.ds(start, kb), :], qst_ref[...])

    def stage_b(slot, m, l, key_offset):
        m_out, l_out, a_out = [], [], []
        for c in range(n_strip):
            cols = slice(c * V7X_LANES, (c + 1) * V7X_LANES)
            st = s_ref[slot, :, cols]
            if key_offset is not None:
                krow = lax.broadcasted_iota(jnp.int32, st.shape, 0) + key_offset
                qcol = lax.broadcasted_iota(jnp.int32, st.shape, 1) + (c * V7X_LANES) % qb
                st = jnp.where(krow <= qcol, st, MASK_VALUE)
            m_new = jnp.maximum(m[c], jnp.max(st, axis=0, keepdims=True))
            alpha = jnp.exp2(m[c] - m_new)
            p = jnp.exp2(st - m_new)
            l_out.append(alpha * l[c] + jnp.sum(p, axis=0, keepdims=True))
            p_ref[slot, :, cols] = _bf16(p)
            m_out.append(m_new)
            a_out.append(alpha)
        return tuple(m_out), tuple(l_out), tuple(a_out)

    def stage_c(t, slot, alpha):
        start = pl.multiple_of(jnp.maximum(t, 0) * kb, kb)
        vt = v_ref[pl.ds(start, kb), :]
        for c in range(0, n_strip, 2):
            cols = slice(c * V7X_LANES, (c + 2) * V7X_LANES)
            a = jnp.concatenate([alpha[c], alpha[c + 1]], axis=1)
            acc_ref[:, cols] = a * acc_ref[:, cols] + _dot_tn(vt, p_ref[slot, :, cols])

    def body(u, carry):
        m, l, alpha = carry
        t = 2 * u
        stage_c(t - 1, 1, alpha)
        stage_a(t + 1, 1)
        m, l, alpha = stage_b(0, m, l, None)
        stage_c(t, 0, alpha)
        stage_a(t + 2, 0)
        m, l, alpha = stage_b(1, m, l, None)
        return m, l, alpha

    row = lambda v: tuple(jnp.full((1, V7X_LANES), v, f32) for _ in range(n_strip))
    stage_a(0, 0)
    m, l, alpha = lax.fori_loop(0, qi, body, (row(MASK_VALUE), row(0.0), row(1.0)))
    t = 2 * qi
    stage_c(t - 1, 1, alpha)
    stage_a(t + 1, 1)
    m, l, alpha = stage_b(0, m, l, 0)
    stage_c(t, 0, alpha)
    m, l, alpha = stage_b(1, m, l, kb)
    stage_c(t + 1, 1, alpha)

    o = acc_ref[...] / jnp.concatenate(l, axis=1)
    o = o[:, :qb] - lam_ref[...] * o[:, qb:]
    o = o * lax.rsqrt(jnp.mean(o * o, axis=0, keepdims=True) + SUBLN_EPS)
    o_ref[...] = _bf16(o * jnp.concatenate([sw_ref[...]] * (qb // V7X_LANES), axis=1))


def _diff_attn(dq, dk, dv, lam, subln_w, B, S):
    blk = min(ATTN_BLOCK, S)
    kb = blk // 2
    assert S % blk == 0 and kb % V7X_LANES == 0
    nq = S // blk
    kernel = functools.partial(_diff_attn_kernel, qb=blk, kb=kb)
    return pl.pallas_call(
        kernel,
        grid=(B, DA_HEADS, nq),
        scratch_shapes=[
            pltpu.VMEM((DA_V_DIM, 2 * blk), jnp.bfloat16),
            pltpu.VMEM((2, kb, 2 * blk), jnp.float32),
            pltpu.VMEM((2, kb, 2 * blk), jnp.bfloat16),
            pltpu.VMEM((DA_V_DIM, 2 * blk), jnp.float32),
        ],
        in_specs=[
            pl.BlockSpec((2 * DA_QK_DIM, blk), lambda b, h, i: (h, b * nq + i)),
            pl.BlockSpec((S, V7X_LANES), lambda b, h, i: (b, h)),
            pl.BlockSpec((S, V7X_LANES), lambda b, h, i: (b, h)),
            pl.BlockSpec((1, 1), lambda b, h, i: (0, 0)),
            pl.BlockSpec((DA_V_DIM, V7X_LANES), lambda b, h, i: (0, 0)),
        ],
        out_specs=pl.BlockSpec((DA_V_DIM, blk), lambda b, h, i: (h, b * nq + i)),
        out_shape=jax.ShapeDtypeStruct((DA_V_W, B * S), jnp.bfloat16),
        compiler_params=pltpu.CompilerParams(
            dimension_semantics=("parallel", "parallel", "arbitrary"),
            vmem_limit_bytes=V7X_VMEM_LIMIT_BYTES),
        name="diff_attn",
    )(dq, dk, dv, lam, subln_w)


def _diff_attn_bounded_kernel(q_ref, k_ref, v_ref, lam_ref, sw_ref, o_ref, qst_ref, p_ref, *, seq):
    tile = V7X_MXU_WIDTH
    n_tok_tile = seq // tile

    qt = q_ref[...]
    dim = lax.broadcasted_iota(jnp.int32, qt.shape, 0)
    qst_ref[0] = jnp.where(dim < DA_QK_DIM, qt, jnp.zeros_like(qt))
    qst_ref[1] = jnp.where(dim >= DA_QK_DIM, qt, jnp.zeros_like(qt))
    sw = jnp.concatenate([sw_ref[...]] * (tile // V7X_LANES), axis=1)

    items = [(j, m) for j in reversed(range(n_tok_tile)) for m in range(2)]
    slots = p_ref.shape[0]

    def probs(n):
        j, m = items[n]
        keys = (j + 1) * tile
        s = _dot(k_ref[:keys, :], qst_ref[m, :, j * tile:(j + 1) * tile])
        krow = lax.broadcasted_iota(jnp.int32, (tile, tile), 0)
        qcol = lax.broadcasted_iota(jnp.int32, (tile, tile), 1)
        p_diag = jnp.exp2(jnp.where(krow <= qcol, s[keys - tile:], MASK_VALUE))
        p_ref[n % slots, keys - tile:keys] = _bf16(p_diag)
        sub = V7X_SUBLANES
        part = jnp.sum(p_diag.reshape(tile // sub, sub, tile), axis=0)
        if keys > tile:
            p_full = jnp.exp2(s[:keys - tile])
            p_ref[n % slots, :keys - tile] = _bf16(p_full)
            part = part + jnp.sum(p_full.reshape((keys - tile) // sub, sub, tile), axis=0)
        return jnp.sum(part, axis=0, keepdims=True)

    def weighted_values(n):
        j, _ = items[n]
        keys = (j + 1) * tile
        return _dot_tn(v_ref[:keys, :], p_ref[n % slots, :keys])

    lsum, out = {}, {}
    for n in range(len(items) + ATTN_PV_LAG):
        if n < len(items):
            lsum[n] = probs(n)
        done = n - ATTN_PV_LAG
        if done >= 0:
            out[done] = weighted_values(done) * (1.0 / lsum.pop(done))
            j, m = items[done]
            if m == 1:
                o = out.pop(done - 1) - lam_ref[...] * out.pop(done)
                o = o * lax.rsqrt(jnp.mean(o * o, axis=0, keepdims=True) + SUBLN_EPS)
                o_ref[:, j * tile:(j + 1) * tile] = _bf16(o * sw)


def _diff_attn_bounded(dq, dk, dv, lam, subln_w, B, S):
    assert S % V7X_MXU_WIDTH == 0
    kernel = functools.partial(_diff_attn_bounded_kernel, seq=S)
    return pl.pallas_call(
        kernel,
        grid=(B, DA_HEADS),
        scratch_shapes=[
            pltpu.VMEM((2, DA_V_DIM, S), jnp.bfloat16),
            pltpu.VMEM((2 * ATTN_PV_LAG + 2, S, V7X_MXU_WIDTH), jnp.bfloat16),
        ],
        in_specs=[
            pl.BlockSpec((2 * DA_QK_DIM, S), lambda b, h: (h, b)),
            pl.BlockSpec((S, V7X_LANES), lambda b, h: (b, h)),
            pl.BlockSpec((S, V7X_LANES), lambda b, h: (b, h)),
            pl.BlockSpec((1, 1), lambda b, h: (0, 0)),
            pl.BlockSpec((DA_V_DIM, V7X_LANES), lambda b, h: (0, 0)),
        ],
        out_specs=pl.BlockSpec((DA_V_DIM, S), lambda b, h: (h, b)),
        out_shape=jax.ShapeDtypeStruct((DA_V_W, B * S), jnp.bfloat16),
        compiler_params=pltpu.CompilerParams(
            dimension_semantics=("parallel", "parallel"),
            vmem_limit_bytes=V7X_VMEM_LIMIT_BYTES),
        name="diff_attn_bounded",
    )(dq, dk, dv, lam, subln_w)


def _retention_kernel(q_ref, k_ref, v_ref, g_ref, lg_ref, o_ref, state_ref, decay_ref, xi_ref, zeta_ref,
                      *, chunk, n_chunks):
    f32 = jnp.float32
    n_batch = q_ref.shape[0]

    @pl.when(pl.program_id(1) == 0)
    def _():
        state_ref[...] = jnp.zeros(state_ref.shape, f32)
        row = lax.broadcasted_iota(jnp.int32, (chunk, chunk), 0)
        col = lax.broadcasted_iota(jnp.int32, (chunk, chunk), 1)
        rel = (row - col).astype(f32)
        idx = lax.broadcasted_iota(jnp.int32, (chunk, V7X_LANES), 0).astype(f32)
        for h in range(RET_HEADS):
            lg = lg_ref[h, :, 0:1]
            decay_ref[h] = jnp.where(rel >= 0, jnp.exp(lg * jnp.maximum(rel, 0.0)), 0.0)
            xi_ref[h] = jnp.exp(lg * (idx + 1.0))
            zeta_ref[h] = jnp.exp(lg * (chunk - 1.0 - idx))

    def head_chunk(c, h, b):
        rows = pl.ds(c * chunk, chunk)
        qk_cols = slice(h * RET_QK_DIM, (h + 1) * RET_QK_DIM)
        v_cols = slice(h * RET_V_DIM, (h + 1) * RET_V_DIM)
        lg = lg_ref[h, :, 0:1]
        qc = q_ref[b, rows, qk_cols]
        kc = k_ref[b, rows, qk_cols]
        vc = v_ref[b, rows, v_cols]
        scores = _bf16(_dot_nt(qc, kc) * decay_ref[h])
        xi = jnp.concatenate([xi_ref[h]] * (RET_V_DIM // V7X_LANES), axis=1)
        o = _dot(scores, vc) + _dot(qc, _bf16(state_ref[b, h])) * xi
        kz = _bf16(kc.astype(f32) * zeta_ref[h])
        state_ref[b, h] = jnp.exp(lg * float(chunk)) * state_ref[b, h] + _dot_tn(kz, vc)
        d = o - jnp.mean(o, axis=-1, keepdims=True)
        var = jnp.mean(d * d, axis=-1, keepdims=True)
        on = d * lax.rsqrt(var + GN_EPS)
        o_ref[b, rows, v_cols] = _bf16(g_ref[b, rows, v_cols].astype(f32) * on)

    for c in range(n_chunks):
        for h in range(RET_HEADS):
            for b in range(n_batch):
                head_chunk(c, h, b)


def _retention(rq, rk, rv, rg, lg_rows, B, S):
    chunk = min(RET_CHUNK, S)
    sblk = min(RET_SEQ_BLOCK, S)
    assert S % sblk == 0 and sblk % chunk == 0
    nsb = S // sblk
    nb = RET_BATCHES if B % RET_BATCHES == 0 else 1
    kernel = functools.partial(_retention_kernel, chunk=chunk, n_chunks=sblk // chunk)
    qk_spec = pl.BlockSpec((nb, sblk, RET_QK_W), lambda b, s: (b, s, 0))
    v_spec = pl.BlockSpec((nb, sblk, RET_V_W), lambda b, s: (b, s, 0))
    by_batch = lambda t: t.reshape(B, S, t.shape[-1])
    out = pl.pallas_call(
        kernel,
        grid=(B // nb, nsb),
        scratch_shapes=[
            pltpu.VMEM((nb, RET_HEADS, RET_QK_DIM, RET_V_DIM), jnp.float32),
            pltpu.VMEM((RET_HEADS, chunk, chunk), jnp.float32),
            pltpu.VMEM((RET_HEADS, chunk, V7X_LANES), jnp.float32),
            pltpu.VMEM((RET_HEADS, chunk, V7X_LANES), jnp.float32),
        ],
        in_specs=[
            qk_spec, qk_spec, v_spec, v_spec,
            pl.BlockSpec((RET_HEADS, 1, V7X_LANES), lambda b, s: (0, 0, 0)),
        ],
        out_specs=v_spec,
        out_shape=jax.ShapeDtypeStruct((B, S, RET_V_W), jnp.bfloat16),
        compiler_params=pltpu.CompilerParams(
            dimension_semantics=("parallel", "arbitrary"),
            vmem_limit_bytes=V7X_VMEM_LIMIT_BYTES),
        name="retention",
    )(by_batch(rq), by_batch(rk), by_batch(rv), by_batch(rg), lg_rows)
    return out.reshape(B * S, RET_V_W)


def _out_mlp_kernel(x_ref, oa_ref, or_ref, gl_ref, wa_ref, wr_ref, wo_ref, n2w_ref, w1_ref, w2_ref,
                    o_ref):
    ya = _dot_tn(oa_ref[...], wa_ref[...])
    yr = _dot(or_ref[...], wr_ref[...])
    ga = jax.nn.sigmoid(gl_ref[:, :D_MODEL].astype(jnp.float32))
    gr = jax.nn.sigmoid(gl_ref[:, D_MODEL:].astype(jnp.float32))
    merged = _bf16(ga * ya + gr * yr)
    x1 = x_ref[...] + _dot(merged, wo_ref[...])
    h2 = x1 * lax.rsqrt(jnp.mean(x1 * x1, axis=-1, keepdims=True) + NORM_EPS) * n2w_ref[...]
    h2 = _bf16(h2)
    acc = x1
    for c in range(D_FF // MLP_FF_CHUNK):
        lo = c * MLP_FF_CHUNK
        hid = jnp.maximum(_dot(h2, w1_ref[:, lo:lo + MLP_FF_CHUNK]), 0.0)
        acc = acc + _dot(_bf16(hid * hid), w2_ref[lo:lo + MLP_FF_CHUNK, :])
    o_ref[...] = acc


def _out_mlp(x2, oa, orr, gl, wa, wr, wo, n2w, w1, w2):
    T = x2.shape[0]
    rows = min(MLP_ROWS, T)
    assert T % rows == 0
    row_spec = lambda width: pl.BlockSpec((rows, width), lambda i: (i, 0))
    weight_spec = lambda shape: pl.BlockSpec(shape, lambda i: (0, 0), pipeline_mode=pl.Buffered(1))
    return pl.pallas_call(
        _out_mlp_kernel,
        grid=(T // rows,),
        in_specs=[
            row_spec(D_MODEL), pl.BlockSpec((DA_V_W, rows), lambda i: (0, i)),
            row_spec(RET_V_W), row_spec(GATE_W),
            weight_spec((DA_V_W, D_MODEL)), weight_spec((RET_V_W, D_MODEL)),
            weight_spec((D_MODEL, D_MODEL)),
            pl.BlockSpec((1, D_MODEL), lambda i: (0, 0)),
            weight_spec((D_MODEL, D_FF)), weight_spec((D_FF, D_MODEL)),
        ],
        out_specs=row_spec(D_MODEL),
        out_shape=jax.ShapeDtypeStruct((T, D_MODEL), jnp.float32),
        compiler_params=pltpu.CompilerParams(
            dimension_semantics=("parallel",), vmem_limit_bytes=V7X_VMEM_LIMIT_BYTES),
        name="out_mlp",
    )(x2, oa, orr, gl, wa, wr, wo, n2w, w1, w2)


def kernel(x, positions, norm1_w, w_in, q_norm_w, k_norm_w, lambda_q1, lambda_k1, lambda_q2,
           lambda_k2, da_subln_w, w_da_branch, ret_gn_w, w_ret_branch, w_out, norm2_w, w_mlp_in,
           w_mlp_out):
    B, S, _ = x.shape
    T = B * S
    f32 = jnp.float32
    x2 = x.reshape(T, D_MODEL)

    q_scale = (DA_QK_DIM ** -0.5) * LOG2_E
    qkw = jnp.stack([jnp.tile(q_norm_w[0].astype(f32) * q_scale, DA_QK_W // DA_QK_DIM),
                     jnp.tile(k_norm_w[0].astype(f32), DA_QK_W // DA_QK_DIM)])
    blk = np.arange(V7X_MXU_WIDTH) // DA_QK_DIM
    ones_bd = jnp.asarray(blk[:, None] == blk[None, :], jnp.bfloat16)

    in_table = jnp.logical_and(jnp.min(positions) >= 0, jnp.max(positions) < ROT_FINE * ROT_COARSE)
    in_proj_args = (x2, positions.astype(f32), norm1_w[0].astype(f32).reshape(1, D_MODEL),
                    _bf16(w_in[0]), qkw, ones_bd)
    dq, dk, dv, rq, rk, rv, rg, gl = lax.cond(
        in_table, functools.partial(_in_proj, table_trig=True),
        functools.partial(_in_proj, table_trig=False), *in_proj_args)

    lam = (jnp.exp(jnp.sum(lambda_q1[0].astype(f32) * lambda_k1[0].astype(f32)))
           - jnp.exp(jnp.sum(lambda_q2[0].astype(f32) * lambda_k2[0].astype(f32)))
           + LAMBDA_INIT).reshape(1, 1)
    subln = jnp.broadcast_to((da_subln_w[0].astype(f32) * (1.0 - LAMBDA_INIT))[:, None],
                             (DA_V_DIM, V7X_LANES))
    logit_bound = (DA_QK_DIM * jnp.max(jnp.abs(qkw[0])) * jnp.max(jnp.abs(qkw[1]))
                   * ATTN_BOUND_MARGIN)
    oa = lax.cond(logit_bound <= ATTN_BOUNDED_LOGIT,
                  functools.partial(_diff_attn_bounded, B=B, S=S),
                  functools.partial(_diff_attn, B=B, S=S),
                  dq, dk, dv, lam, subln)

    log_gamma = np.log1p(-np.exp2(-5.0 - np.arange(RET_HEADS, dtype=np.float64)))
    lg_rows = jnp.asarray(np.broadcast_to(log_gamma[:, None, None], (RET_HEADS, 1, V7X_LANES)), f32)
    orr = _retention(rq, rk, rv, rg, lg_rows, B, S)

    w_ret = _bf16(ret_gn_w[0].astype(f32)[:, None] * w_ret_branch[0].astype(f32))
    out = _out_mlp(x2, oa, orr, gl, _bf16(w_da_branch[0]), w_ret, _bf16(w_out[0]),
                   norm2_w[0].astype(f32).reshape(1, D_MODEL), _bf16(w_mlp_in[0]),
                   _bf16(w_mlp_out[0]))
    return out.reshape(B, S, D_MODEL)
```

```python
import functools
import math

import jax
import jax.numpy as jnp
import numpy as np
from jax import lax
from jax.experimental import pallas as pl
from jax.experimental.pallas import tpu as pltpu

D_MODEL = 1024
DA_HEADS = 4
DA_QK_DIM = 64
DA_V_DIM = 2 * DA_QK_DIM
DA_ROT_DIM = DA_QK_DIM // 4
ROPE_THETA = 500000.0
RET_HEADS = 4
RET_QK_DIM = 128
RET_V_DIM = 256
RET_THETA_BASE = 10000.0
D_FF = 4 * D_MODEL
NORM_EPS = 1e-6
SUBLN_EPS = 1e-5
GN_EPS = 1e-5
MASK_VALUE = -1e30
LAMBDA_INIT = 0.8 - 0.6 * math.exp(-0.3 * 0)

DA_QK_W = DA_HEADS * 2 * DA_QK_DIM
DA_V_W = DA_HEADS * DA_V_DIM
RET_QK_W = RET_HEADS * RET_QK_DIM
RET_V_W = RET_HEADS * RET_V_DIM
GATE_W = 2 * D_MODEL
SECTION_WIDTHS = (DA_QK_W, DA_QK_W, DA_V_W, RET_QK_W, RET_QK_W, RET_V_W, RET_V_W, GATE_W)
SECTION_STARTS = tuple(int(v) for v in np.cumsum((0,) + SECTION_WIDTHS[:-1]))
IN_W = sum(SECTION_WIDTHS)

V7X_LANES = 128
V7X_SUBLANES = 8
V7X_MXU_WIDTH = 256
V7X_VMEM_LIMIT_BYTES = 56 * 1024 * 1024

LOG2_E = math.log2(math.e)

PROJ_ROWS = 512
IN_PROJ_GATE_PIECES = 4
ROT_FINE = 64
ROT_COARSE = 128
ATTN_BLOCK = 512
ATTN_BOUNDED_LOGIT = 60.0
ATTN_PV_LAG = 4
ATTN_BOUND_MARGIN = 1.0 + 2.0 ** -6
RET_CHUNK = 256
RET_SEQ_BLOCK = 1024
MLP_ROWS = 512
MLP_FF_CHUNK = 1024


def _bf16(t):
    return t.astype(jnp.bfloat16)


def _dot(a, b):
    return jnp.dot(a, b, preferred_element_type=jnp.float32)


def _dot_nt(a, b):
    return lax.dot_general(a, b, (((1,), (1,)), ((), ())), preferred_element_type=jnp.float32)


def _dot_tn(a, b):
    return lax.dot_general(a, b, (((0,), (0,)), ((), ())), preferred_element_type=jnp.float32)


def _split_bf16(t):
    hi = _bf16(t)
    lo = _bf16(t - hi.astype(jnp.float32))
    return jnp.concatenate([hi, lo], axis=0)


def _in_proj_kernel(x_ref, pos_ref, n1w_ref, w_ref, qkw_ref, freq_ref, coarse_ref, fine_ref, eda_ref,
                    ert_ref, ones_ref,
                    dq_ref, dk_ref, dv_ref, rq_ref, rk_ref, rv_ref, rg_ref, gl_ref, *, table_trig):
    x = x_ref[...]
    h = x * lax.rsqrt(jnp.mean(x * x, axis=-1, keepdims=True) + NORM_EPS) * n1w_ref[...]
    h = _bf16(h)

    def section(idx):
        lo = SECTION_STARTS[idx]
        return _dot(h, w_ref[:, lo:lo + SECTION_WIDTHS[idx]])

    rows = x.shape[0]
    n_da = DA_ROT_DIM // 2
    n_freq = freq_ref.shape[0]
    pos = pos_ref[0]
    if table_trig:
        a = jnp.floor(pos * (1.0 / ROT_FINE))
        b = pos - ROT_FINE * a
        pick_a = jnp.where(lax.broadcasted_iota(jnp.int32, (ROT_COARSE, rows), 0).astype(jnp.float32) == a,
                           1.0, 0.0).astype(jnp.bfloat16)
        pick_b = jnp.where(lax.broadcasted_iota(jnp.int32, (ROT_FINE, rows), 0).astype(jnp.float32) == b,
                           1.0, 0.0).astype(jnp.bfloat16)
        cs_a = _dot(coarse_ref[...], jnp.concatenate([pick_a, pick_a], axis=0))
        cs_b = _dot(fine_ref[...], jnp.concatenate([pick_b, pick_b], axis=0))
        cos_t = cs_a[:n_freq] * cs_b[:n_freq] - cs_a[n_freq:] * cs_b[n_freq:]
        sin_t = cs_a[n_freq:] * cs_b[:n_freq] + cs_a[:n_freq] * cs_b[n_freq:]
    else:
        ang = jnp.concatenate([freq_ref[...]] * (rows // V7X_LANES), axis=1) * pos
        cos_t, sin_t = jnp.cos(ang), jnp.sin(ang)
    tab_da = _dot_tn(_split_bf16(jnp.concatenate([cos_t[:n_da], sin_t[:n_da]], axis=0)),
                     eda_ref[...])
    tab_rt = _dot_tn(_split_bf16(jnp.concatenate([cos_t[n_da:], sin_t[n_da:]], axis=0)),
                     ert_ref[...])
    lane1 = lax.broadcasted_iota(jnp.int32, (1, V7X_LANES), 1)
    unrotated = jnp.where(lane1 % DA_QK_DIM >= DA_ROT_DIM, 1.0, 0.0)

    def plain(idx, out_ref, lo_col, width):
        lo = SECTION_STARTS[idx] + lo_col
        out_ref[:, lo_col:lo_col + width] = _bf16(_dot(h, w_ref[:, lo:lo + width]))

    cos_rt = tab_rt[:, :V7X_LANES]
    sin_rt = tab_rt[:, V7X_LANES:]

    def retention_qk(idx, out_ref, scale):
        t = section(idx)
        for hh in range(RET_HEADS):
            th = t[:, hh * RET_QK_DIM:(hh + 1) * RET_QK_DIM]
            rot = th * cos_rt + pltpu.roll(th, RET_QK_DIM // 2, axis=1) * sin_rt
            if scale != 1.0:
                rot = rot * scale
            out_ref[:, hh * RET_QK_DIM:(hh + 1) * RET_QK_DIM] = _bf16(rot)

    reps = DA_QK_W // V7X_LANES
    cos_da = jnp.concatenate([tab_da[:, :V7X_LANES] + unrotated] * reps, axis=1)
    sin_da = jnp.concatenate([tab_da[:, V7X_LANES:]] * reps, axis=1)
    lane = lax.broadcasted_iota(jnp.int32, (rows, DA_QK_W), 1)
    first_half = (lane % DA_QK_DIM) < (DA_ROT_DIM // 2)
    ones_bd = ones_ref[...]

    def diff_attn_qk(idx, out_ref):
        t = section(idx)
        tt = _bf16(t * t)
        ms = jnp.concatenate(
            [_dot(tt[:, g:g + V7X_MXU_WIDTH], ones_bd) for g in range(0, DA_QK_W, V7X_MXU_WIDTH)],
            axis=1) * (1.0 / DA_QK_DIM)
        t = t * lax.rsqrt(ms + NORM_EPS) * qkw_ref[idx:idx + 1, :]
        partner = jnp.where(first_half,
                            pltpu.roll(t, DA_QK_W - DA_ROT_DIM // 2, axis=1),
                            pltpu.roll(t, DA_ROT_DIM // 2, axis=1))
        rot = t * cos_da + partner * sin_da
        out_ref[...] = _bf16(rot.T if idx == 0 else rot)

    retention_qk(3, rq_ref, 1.0)
    plain(2, dv_ref, 0, DA_V_W)
    retention_qk(4, rk_ref, RET_QK_DIM ** -0.5)
    plain(5, rv_ref, 0, RET_V_W // 2)
    diff_attn_qk(0, dq_ref)
    plain(5, rv_ref, RET_V_W // 2, RET_V_W // 2)
    diff_attn_qk(1, dk_ref)
    for piece in range(IN_PROJ_GATE_PIECES):
        wg, wl = RET_V_W // IN_PROJ_GATE_PIECES, GATE_W // IN_PROJ_GATE_PIECES
        lo = SECTION_STARTS[6] + piece * wg
        rg = _dot(h, w_ref[:, lo:lo + wg])
        rg_ref[:, piece * wg:(piece + 1) * wg] = _bf16(rg * jax.nn.sigmoid(rg))
        plain(7, gl_ref, piece * wl, wl)


def _rotary_constants():
    n_da, n_rt = DA_ROT_DIM // 2, RET_QK_DIM // 2
    inv_da = ROPE_THETA ** (-jnp.arange(0, DA_ROT_DIM, 2, dtype=jnp.float32) / DA_ROT_DIM)
    inv_rt = 1.0 / (RET_THETA_BASE ** jnp.linspace(0.0, 1.0, n_rt, dtype=jnp.float32))
    freqs = jnp.concatenate([inv_da, inv_rt])
    freq = jnp.broadcast_to(freqs[:, None], (n_da + n_rt, V7X_LANES))

    def trig_table(multiples):
        keep_11_bits = jnp.uint32(0xFFFFFFFF << (24 - 11) & 0xFFFFFFFF)
        bits = lax.bitcast_convert_type(freqs, jnp.uint32) & keep_11_bits
        f_hi = lax.bitcast_convert_type(bits, jnp.float32)
        big = f_hi[:, None] * multiples[None, :]
        small = (freqs - f_hi)[:, None] * multiples[None, :]
        cos = jnp.cos(big) * jnp.cos(small) - jnp.sin(big) * jnp.sin(small)
        sin = jnp.sin(big) * jnp.cos(small) + jnp.cos(big) * jnp.sin(small)
        t = jnp.concatenate([cos, sin], axis=0)
        hi = _bf16(t)
        return jnp.concatenate([hi, _bf16(t - hi.astype(jnp.float32))], axis=1)

    coarse = trig_table(ROT_FINE * jnp.arange(ROT_COARSE, dtype=jnp.float32))
    fine = trig_table(jnp.arange(ROT_FINE, dtype=jnp.float32))

    def selection(n, period):
        e = np.zeros((4 * n, 2 * V7X_LANES), np.float32)
        for f in range(n):
            for base in range(0, V7X_LANES, period):
                for part in (0, 2 * n):
                    e[part + f, base + f] = e[part + f, base + n + f] = 1.0
                    e[part + n + f, V7X_LANES + base + f] = -1.0
                    e[part + n + f, V7X_LANES + base + n + f] = 1.0
        return jnp.asarray(e, jnp.bfloat16)

    return freq, coarse, fine, selection(n_da, DA_QK_DIM), selection(n_rt, RET_QK_DIM)


def _in_proj(x2, pos, n1w, w_in, qkw, ones_bd, table_trig):
    T = x2.shape[0]
    rows = min(PROJ_ROWS, T)
    assert T % rows == 0 and rows % V7X_LANES == 0
    freq, coarse, fine, e_da, e_rt = _rotary_constants()
    pos = pos.reshape(T // rows, 1, rows)
    row_spec = lambda width: pl.BlockSpec((rows, width), lambda i: (i, 0))
    const_spec = lambda shape: pl.BlockSpec(shape, lambda i: (0, 0))
    out_widths = SECTION_WIDTHS
    return pl.pallas_call(
        functools.partial(_in_proj_kernel, table_trig=table_trig),
        grid=(T // rows,),
        in_specs=[
            row_spec(D_MODEL),
            pl.BlockSpec((1, 1, rows), lambda i: (i, 0, 0)),
            const_spec((1, D_MODEL)),
            pl.BlockSpec((D_MODEL, IN_W), lambda i: (0, 0), pipeline_mode=pl.Buffered(1)),
            const_spec((2, DA_QK_W)),
            const_spec(freq.shape), const_spec(coarse.shape), const_spec(fine.shape),
            const_spec(e_da.shape), const_spec(e_rt.shape),
            const_spec((V7X_MXU_WIDTH, V7X_MXU_WIDTH)),
        ],
        out_specs=[pl.BlockSpec((DA_QK_W, rows), lambda i: (0, i))] + [row_spec(w) for w in out_widths[1:]],
        out_shape=[jax.ShapeDtypeStruct((DA_QK_W, T), jnp.bfloat16)]
        + [jax.ShapeDtypeStruct((T, w), jnp.bfloat16) for w in out_widths[1:]],
        compiler_params=pltpu.CompilerParams(
            dimension_semantics=("parallel",), vmem_limit_bytes=V7X_VMEM_LIMIT_BYTES),
        name="in_proj",
    )(x2, pos, n1w, w_in, qkw, freq, coarse, fine, e_da, e_rt, ones_bd)


def _diff_attn_kernel(q_ref, k_ref, v_ref, lam_ref, sw_ref, o_ref,
                      qst_ref, s_ref, p_ref, acc_ref, *, qb, kb):
    qi = pl.program_id(2)
    nq = 2 * qb
    n_strip = nq // V7X_LANES
    f32 = jnp.float32

    qt = q_ref[...]
    dim = lax.broadcasted_iota(jnp.int32, qt.shape, 0)
    qst_ref[:, :qb] = jnp.where(dim < DA_QK_DIM, qt, jnp.zeros_like(qt))
    qst_ref[:, qb:] = jnp.where(dim >= DA_QK_DIM, qt, jnp.zeros_like(qt))
    acc_ref[...] = jnp.zeros(acc_ref.shape, f32)
    p_ref[1] = jnp.zeros(p_ref.shape[1:], p_ref.dtype)

    def stage_a(t, slot):
        start = pl.multiple_of(t * kb, kb)
        s_ref[slot] = _dot(k_ref[pl.ds(start, kb), :], qst_ref[...])

    def stage_b(slot, m, l, key_offset):
        m_out, l_out, a_out = [], [], []
        for c in range(n_strip):
            cols = slice(c * V7X_LANES, (c + 1) * V7X_LANES)
            st = s_ref[slot, :, cols]
            if key_offset is not None:
                krow = lax.broadcasted_iota(jnp.int32, st.shape, 0) + key_offset
                qcol = lax.broadcasted_iota(jnp.int32, st.shape, 1) + (c * V7X_LANES) % qb
                st = jnp.where(krow <= qcol, st, MASK_VALUE)
            m_new = jnp.maximum(m[c], jnp.max(st, axis=0, keepdims=True))
            alpha = jnp.exp2(m[c] - m_new)
            p = jnp.exp2(st - m_new)
            l_out.append(alpha * l[c] + jnp.sum(p, axis=0, keepdims=True))
            p_ref[slot, :, cols] = _bf16(p)
            m_out.append(m_new)
            a_out.append(alpha)
        return tuple(m_out), tuple(l_out), tuple(a_out)

    def stage_c(t, slot, alpha):
        start = pl.multiple_of(jnp.maximum(t, 0) * kb, kb)
        vt = v_ref[pl.ds(start, kb), :]
        for c in range(0, n_strip, 2):
            cols = slice(c * V7X_LANES, (c + 2) * V7X_LANES)
            a = jnp.concatenate([alpha[c], alpha[c + 1]], axis=1)
            acc_ref[:, cols] = a * acc_ref[:, cols] + _dot_tn(vt, p_ref[slot, :, cols])

    def body(u, carry):
        m, l, alpha = carry
        t = 2 * u
        stage_c(t - 1, 1, alpha)
        stage_a(t + 1, 1)
        m, l, alpha = stage_b(0, m, l, None)
        stage_c(t, 0, alpha)
        stage_a(t + 2, 0)
        m, l, alpha = stage_b(1, m, l, None)
        return m, l, alpha

    row = lambda v: tuple(jnp.full((1, V7X_LANES), v, f32) for _ in range(n_strip))
    stage_a(0, 0)
    m, l, alpha = lax.fori_loop(0, qi, body, (row(MASK_VALUE), row(0.0), row(1.0)))
    t = 2 * qi
    stage_c(t - 1, 1, alpha)
    stage_a(t + 1, 1)
    m, l, alpha = stage_b(0, m, l, 0)
    stage_c(t, 0, alpha)
    m, l, alpha = stage_b(1, m, l, kb)
    stage_c(t + 1, 1, alpha)

    o = acc_ref[...] / jnp.concatenate(l, axis=1)
    o = o[:, :qb] - lam_ref[...] * o[:, qb:]
    o = o * lax.rsqrt(jnp.mean(o * o, axis=0, keepdims=True) + SUBLN_EPS)
    o_ref[...] = _bf16(o * jnp.concatenate([sw_ref[...]] * (qb // V7X_LANES), axis=1))


def _diff_attn(dq, dk, dv, lam, subln_w, B, S):
    blk = min(ATTN_BLOCK, S)
    kb = blk // 2
    assert S % blk == 0 and kb % V7X_LANES == 0
    nq = S // blk
    kernel = functools.partial(_diff_attn_kernel, qb=blk, kb=kb)
    return pl.pallas_call(
        kernel,
        grid=(B, DA_HEADS, nq),
        scratch_shapes=[
            pltpu.VMEM((DA_V_DIM, 2 * blk), jnp.bfloat16),
            pltpu.VMEM((2, kb, 2 * blk), jnp.float32),
            pltpu.VMEM((2, kb, 2 * blk), jnp.bfloat16),
            pltpu.VMEM((DA_V_DIM, 2 * blk), jnp.float32),
        ],
        in_specs=[
            pl.BlockSpec((2 * DA_QK_DIM, blk), lambda b, h, i: (h, b * nq + i)),
            pl.BlockSpec((S, V7X_LANES), lambda b, h, i: (b, h)),
            pl.BlockSpec((S, V7X_LANES), lambda b, h, i: (b, h)),
            pl.BlockSpec((1, 1), lambda b, h, i: (0, 0)),
            pl.BlockSpec((DA_V_DIM, V7X_LANES), lambda b, h, i: (0, 0)),
        ],
        out_specs=pl.BlockSpec((DA_V_DIM, blk), lambda b, h, i: (h, b * nq + i)),
        out_shape=jax.ShapeDtypeStruct((DA_V_W, B * S), jnp.bfloat16),
        compiler_params=pltpu.CompilerParams(
            dimension_semantics=("parallel", "parallel", "arbitrary"),
            vmem_limit_bytes=V7X_VMEM_LIMIT_BYTES),
        name="diff_attn",
    )(dq, dk, dv, lam, subln_w)


def _diff_attn_bounded_kernel(q_ref, k_ref, v_ref, lam_ref, sw_ref, o_ref, qst_ref, p_ref, *, seq):
    tile = V7X_MXU_WIDTH
    n_tok_tile = seq // tile

    qt = q_ref[...]
    dim = lax.broadcasted_iota(jnp.int32, qt.shape, 0)
    qst_ref[0] = jnp.where(dim < DA_QK_DIM, qt, jnp.zeros_like(qt))
    qst_ref[1] = jnp.where(dim >= DA_QK_DIM, qt, jnp.zeros_like(qt))
    sw = jnp.concatenate([sw_ref[...]] * (tile // V7X_LANES), axis=1)

    items = [(j, m) for j in reversed(range(n_tok_tile)) for m in range(2)]
    slots = p_ref.shape[0]

    def probs(n):
        j, m = items[n]
        keys = (j + 1) * tile
        s = _dot(k_ref[:keys, :], qst_ref[m, :, j * tile:(j + 1) * tile])
        krow = lax.broadcasted_iota(jnp.int32, (tile, tile), 0)
        qcol = lax.broadcasted_iota(jnp.int32, (tile, tile), 1)
        p_diag = jnp.exp2(jnp.where(krow <= qcol, s[keys - tile:], MASK_VALUE))
        p_ref[n % slots, keys - tile:keys] = _bf16(p_diag)
        sub = V7X_SUBLANES
        part = jnp.sum(p_diag.reshape(tile // sub, sub, tile), axis=0)
        if keys > tile:
            p_full = jnp.exp2(s[:keys - tile])
            p_ref[n % slots, :keys - tile] = _bf16(p_full)
            part = part + jnp.sum(p_full.reshape((keys - tile) // sub, sub, tile), axis=0)
        return jnp.sum(part, axis=0, keepdims=True)

    def weighted_values(n):
        j, _ = items[n]
        keys = (j + 1) * tile
        return _dot_tn(v_ref[:keys, :], p_ref[n % slots, :keys])

    lsum, out = {}, {}
    for n in range(len(items) + ATTN_PV_LAG):
        if n < len(items):
            lsum[n] = probs(n)
        done = n - ATTN_PV_LAG
        if done >= 0:
            out[done] = weighted_values(done) * (1.0 / lsum.pop(done))
            j, m = items[done]
            if m == 1:
                o = out.pop(done - 1) - lam_ref[...] * out.pop(done)
                o = o * lax.rsqrt(jnp.mean(o * o, axis=0, keepdims=True) + SUBLN_EPS)
                o_ref[:, j * tile:(j + 1) * tile] = _bf16(o * sw)


def _diff_attn_bounded(dq, dk, dv, lam, subln_w, B, S):
    assert S % V7X_MXU_WIDTH == 0
    kernel = functools.partial(_diff_attn_bounded_kernel, seq=S)
    return pl.pallas_call(
        kernel,
        grid=(B, DA_HEADS),
        scratch_shapes=[
            pltpu.VMEM((2, DA_V_DIM, S), jnp.bfloat16),
            pltpu.VMEM((2 * ATTN_PV_LAG + 2, S, V7X_MXU_WIDTH), jnp.bfloat16),
        ],
        in_specs=[
            pl.BlockSpec((2 * DA_QK_DIM, S), lambda b, h: (h, b)),
            pl.BlockSpec((S, V7X_LANES), lambda b, h: (b, h)),
            pl.BlockSpec((S, V7X_LANES), lambda b, h: (b, h)),
            pl.BlockSpec((1, 1), lambda b, h: (0, 0)),
            pl.BlockSpec((DA_V_DIM, V7X_LANES), lambda b, h: (0, 0)),
        ],
        out_specs=pl.BlockSpec((DA_V_DIM, S), lambda b, h: (h, b)),
        out_shape=jax.ShapeDtypeStruct((DA_V_W, B * S), jnp.bfloat16),
        compiler_params=pltpu.CompilerParams(
            dimension_semantics=("parallel", "parallel"),
            vmem_limit_bytes=V7X_VMEM_LIMIT_BYTES),
        name="diff_attn_bounded",
    )(dq, dk, dv, lam, subln_w)


def _retention_kernel(q_ref, k_ref, v_ref, g_ref, lg_ref, o_ref, state_ref, decay_ref, xi_ref, zeta_ref,
                      *, chunk, n_chunks):
    f32 = jnp.float32

    @pl.when(pl.program_id(1) == 0)
    def _():
        state_ref[...] = jnp.zeros(state_ref.shape, f32)
        row = lax.broadcasted_iota(jnp.int32, (chunk, chunk), 0)
        col = lax.broadcasted_iota(jnp.int32, (chunk, chunk), 1)
        rel = (row - col).astype(f32)
        idx = lax.broadcasted_iota(jnp.int32, (chunk, V7X_LANES), 0).astype(f32)
        for h in range(RET_HEADS):
            lg = lg_ref[h, :, 0:1]
            decay_ref[h] = jnp.where(rel >= 0, jnp.exp(lg * jnp.maximum(rel, 0.0)), 0.0)
            xi_ref[h] = jnp.exp(lg * (idx + 1.0))
            zeta_ref[h] = jnp.exp(lg * (chunk - 1.0 - idx))

    def head_chunk(c, h):
        rows = pl.ds(c * chunk, chunk)
        qk_cols = slice(h * RET_QK_DIM, (h + 1) * RET_QK_DIM)
        v_cols = slice(h * RET_V_DIM, (h + 1) * RET_V_DIM)
        lg = lg_ref[h, :, 0:1]
        qc = q_ref[rows, qk_cols]
        kc = k_ref[rows, qk_cols]
        vc = v_ref[rows, v_cols]
        scores = _bf16(_dot_nt(qc, kc) * decay_ref[h])
        xi = jnp.concatenate([xi_ref[h]] * (RET_V_DIM // V7X_LANES), axis=1)
        o = _dot(scores, vc) + _dot(qc, _bf16(state_ref[h])) * xi
        kz = _bf16(kc.astype(f32) * zeta_ref[h])
        state_ref[h] = jnp.exp(lg * float(chunk)) * state_ref[h] + _dot_tn(kz, vc)
        d = o - jnp.mean(o, axis=-1, keepdims=True)
        var = jnp.mean(d * d, axis=-1, keepdims=True)
        on = d * lax.rsqrt(var + GN_EPS)
        o_ref[rows, v_cols] = _bf16(g_ref[rows, v_cols].astype(f32) * on)

    for c in range(n_chunks):
        for h in range(RET_HEADS):
            head_chunk(c, h)


def _retention(rq, rk, rv, rg, lg_rows, B, S):
    chunk = min(RET_CHUNK, S)
    sblk = min(RET_SEQ_BLOCK, S)
    assert S % sblk == 0 and sblk % chunk == 0
    nsb = S // sblk
    kernel = functools.partial(_retention_kernel, chunk=chunk, n_chunks=sblk // chunk)
    qk_spec = pl.BlockSpec((sblk, RET_QK_W), lambda b, s: (b * nsb + s, 0))
    v_spec = pl.BlockSpec((sblk, RET_V_W), lambda b, s: (b * nsb + s, 0))
    return pl.pallas_call(
        kernel,
        grid=(B, nsb),
        scratch_shapes=[
            pltpu.VMEM((RET_HEADS, RET_QK_DIM, RET_V_DIM), jnp.float32),
            pltpu.VMEM((RET_HEADS, chunk, chunk), jnp.float32),
            pltpu.VMEM((RET_HEADS, chunk, V7X_LANES), jnp.float32),
            pltpu.VMEM((RET_HEADS, chunk, V7X_LANES), jnp.float32),
        ],
        in_specs=[
            qk_spec, qk_spec, v_spec, v_spec,
            pl.BlockSpec((RET_HEADS, 1, V7X_LANES), lambda b, s: (0, 0, 0)),
        ],
        out_specs=v_spec,
        out_shape=jax.ShapeDtypeStruct((B * S, RET_V_W), jnp.bfloat16),
        compiler_params=pltpu.CompilerParams(
            dimension_semantics=("parallel", "arbitrary"),
            vmem_limit_bytes=V7X_VMEM_LIMIT_BYTES),
        name="retention",
    )(rq, rk, rv, rg, lg_rows)


def _out_mlp_kernel(x_ref, oa_ref, or_ref, gl_ref, wa_ref, wr_ref, wo_ref, n2w_ref, w1_ref, w2_ref,
                    o_ref):
    ya = _dot_tn(oa_ref[...], wa_ref[...])
    yr = _dot(or_ref[...], wr_ref[...])
    ga = jax.nn.sigmoid(gl_ref[:, :D_MODEL].astype(jnp.float32))
    gr = jax.nn.sigmoid(gl_ref[:, D_MODEL:].astype(jnp.float32))
    merged = _bf16(ga * ya + gr * yr)
    x1 = x_ref[...] + _dot(merged, wo_ref[...])
    h2 = x1 * lax.rsqrt(jnp.mean(x1 * x1, axis=-1, keepdims=True) + NORM_EPS) * n2w_ref[...]
    h2 = _bf16(h2)
    acc = x1
    for c in range(D_FF // MLP_FF_CHUNK):
        lo = c * MLP_FF_CHUNK
        hid = jnp.maximum(_dot(h2, w1_ref[:, lo:lo + MLP_FF_CHUNK]), 0.0)
        acc = acc + _dot(_bf16(hid * hid), w2_ref[lo:lo + MLP_FF_CHUNK, :])
    o_ref[...] = acc


def _out_mlp(x2, oa, orr, gl, wa, wr, wo, n2w, w1, w2):
    T = x2.shape[0]
    rows = min(MLP_ROWS, T)
    assert T % rows == 0
    row_spec = lambda width: pl.BlockSpec((rows, width), lambda i: (i, 0))
    weight_spec = lambda shape: pl.BlockSpec(shape, lambda i: (0, 0), pipeline_mode=pl.Buffered(1))
    return pl.pallas_call(
        _out_mlp_kernel,
        grid=(T // rows,),
        in_specs=[
            row_spec(D_MODEL), pl.BlockSpec((DA_V_W, rows), lambda i: (0, i)),
            row_spec(RET_V_W), row_spec(GATE_W),
            weight_spec((DA_V_W, D_MODEL)), weight_spec((RET_V_W, D_MODEL)),
            weight_spec((D_MODEL, D_MODEL)),
            pl.BlockSpec((1, D_MODEL), lambda i: (0, 0)),
            weight_spec((D_MODEL, D_FF)), weight_spec((D_FF, D_MODEL)),
        ],
        out_specs=row_spec(D_MODEL),
        out_shape=jax.ShapeDtypeStruct((T, D_MODEL), jnp.float32),
        compiler_params=pltpu.CompilerParams(
            dimension_semantics=("parallel",), vmem_limit_bytes=V7X_VMEM_LIMIT_BYTES),
        name="out_mlp",
    )(x2, oa, orr, gl, wa, wr, wo, n2w, w1, w2)


def kernel(x, positions, norm1_w, w_in, q_norm_w, k_norm_w, lambda_q1, lambda_k1, lambda_q2,
           lambda_k2, da_subln_w, w_da_branch, ret_gn_w, w_ret_branch, w_out, norm2_w, w_mlp_in,
           w_mlp_out):
    B, S, _ = x.shape
    T = B * S
    f32 = jnp.float32
    x2 = x.reshape(T, D_MODEL)

    q_scale = (DA_QK_DIM ** -0.5) * LOG2_E
    qkw = jnp.stack([jnp.tile(q_norm_w[0].astype(f32) * q_scale, DA_QK_W // DA_QK_DIM),
                     jnp.tile(k_norm_w[0].astype(f32), DA_QK_W // DA_QK_DIM)])
    blk = np.arange(V7X_MXU_WIDTH) // DA_QK_DIM
    ones_bd = jnp.asarray(blk[:, None] == blk[None, :], jnp.bfloat16)

    in_table = jnp.logical_and(jnp.min(positions) >= 0, jnp.max(positions) < ROT_FINE * ROT_COARSE)
    in_proj_args = (x2, positions.astype(f32), norm1_w[0].astype(f32).reshape(1, D_MODEL),
                    _bf16(w_in[0]), qkw, ones_bd)
    dq, dk, dv, rq, rk, rv, rg, gl = lax.cond(
        in_table, functools.partial(_in_proj, table_trig=True),
        functools.partial(_in_proj, table_trig=False), *in_proj_args)

    lam = (jnp.exp(jnp.sum(lambda_q1[0].astype(f32) * lambda_k1[0].astype(f32)))
           - jnp.exp(jnp.sum(lambda_q2[0].astype(f32) * lambda_k2[0].astype(f32)))
           + LAMBDA_INIT).reshape(1, 1)
    subln = jnp.broadcast_to((da_subln_w[0].astype(f32) * (1.0 - LAMBDA_INIT))[:, None],
                             (DA_V_DIM, V7X_LANES))
    logit_bound = (DA_QK_DIM * jnp.max(jnp.abs(qkw[0])) * jnp.max(jnp.abs(qkw[1]))
                   * ATTN_BOUND_MARGIN)
    oa = lax.cond(logit_bound <= ATTN_BOUNDED_LOGIT,
                  functools.partial(_diff_attn_bounded, B=B, S=S),
                  functools.partial(_diff_attn, B=B, S=S),
                  dq, dk, dv, lam, subln)

    log_gamma = np.log1p(-np.exp2(-5.0 - np.arange(RET_HEADS, dtype=np.float64)))
    lg_rows = jnp.asarray(np.broadcast_to(log_gamma[:, None, None], (RET_HEADS, 1, V7X_LANES)), f32)
    orr = _retention(rq, rk, rv, rg, lg_rows, B, S)

    w_ret = _bf16(ret_gn_w[0].astype(f32)[:, None] * w_ret_branch[0].astype(f32))
    out = _out_mlp(x2, oa, orr, gl, _bf16(w_da_branch[0]), w_ret, _bf16(w_out[0]),
                   norm2_w[0].astype(f32).reshape(1, D_MODEL), _bf16(w_mlp_in[0]),
                   _bf16(w_mlp_out[0]))
    return out.reshape(B, S, D_MODEL)
```

```python
import functools
import math

import jax
import jax.numpy as jnp
import numpy as np
from jax import lax
from jax.experimental import pallas as pl
from jax.experimental.pallas import tpu as pltpu

D_MODEL = 1024
DA_HEADS = 4
DA_QK_DIM = 64
DA_V_DIM = 2 * DA_QK_DIM
DA_ROT_DIM = DA_QK_DIM // 4
ROPE_THETA = 500000.0
RET_HEADS = 4
RET_QK_DIM = 128
RET_V_DIM = 256
RET_THETA_BASE = 10000.0
D_FF = 4 * D_MODEL
NORM_EPS = 1e-6
SUBLN_EPS = 1e-5
GN_EPS = 1e-5
MASK_VALUE = -1e30
LAMBDA_INIT = 0.8 - 0.6 * math.exp(-0.3 * 0)

DA_QK_W = DA_HEADS * 2 * DA_QK_DIM
DA_V_W = DA_HEADS * DA_V_DIM
RET_QK_W = RET_HEADS * RET_QK_DIM
RET_V_W = RET_HEADS * RET_V_DIM
GATE_W = 2 * D_MODEL
SECTION_WIDTHS = (DA_QK_W, DA_QK_W, DA_V_W, RET_QK_W, RET_QK_W, RET_V_W, RET_V_W, GATE_W)
SECTION_STARTS = tuple(int(v) for v in np.cumsum((0,) + SECTION_WIDTHS[:-1]))
IN_W = sum(SECTION_WIDTHS)

V7X_LANES = 128
V7X_SUBLANES = 8
V7X_MXU_WIDTH = 256
V7X_VMEM_LIMIT_BYTES = 56 * 1024 * 1024

LOG2_E = math.log2(math.e)

PROJ_ROWS = 512
IN_PROJ_GATE_PIECES = 4
ROT_FINE = 64
ROT_COARSE = 128
ATTN_BLOCK = 512
ATTN_BOUNDED_LOGIT = 60.0
ATTN_PV_LAG = 4
ATTN_BOUND_MARGIN = 1.0 + 2.0 ** -6
RET_CHUNK = 256
RET_SEQ_BLOCK = 1024
MLP_ROWS = 512
MLP_FF_CHUNK = 1024


def _bf16(t):
    return t.astype(jnp.bfloat16)


def _dot(a, b):
    return jnp.dot(a, b, preferred_element_type=jnp.float32)


def _dot_nt(a, b):
    return lax.dot_general(a, b, (((1,), (1,)), ((), ())), preferred_element_type=jnp.float32)


def _dot_tn(a, b):
    return lax.dot_general(a, b, (((0,), (0,)), ((), ())), preferred_element_type=jnp.float32)


def _split_bf16(t):
    hi = _bf16(t)
    lo = _bf16(t - hi.astype(jnp.float32))
    return jnp.concatenate([hi, lo], axis=0)


def _in_proj_kernel(x_ref, pos_ref, n1w_ref, w_ref, qkw_ref, freq_ref, coarse_ref, fine_ref, eda_ref,
                    ert_ref, ones_ref,
                    dq_ref, dk_ref, dv_ref, rq_ref, rk_ref, rv_ref, rg_ref, gl_ref, *, table_trig):
    x = x_ref[...]
    h = x * lax.rsqrt(jnp.mean(x * x, axis=-1, keepdims=True) + NORM_EPS) * n1w_ref[...]
    h = _bf16(h)

    def section(idx):
        lo = SECTION_STARTS[idx]
        return _dot(h, w_ref[:, lo:lo + SECTION_WIDTHS[idx]])

    rows = x.shape[0]
    n_da = DA_ROT_DIM // 2
    n_freq = freq_ref.shape[0]
    pos = pos_ref[0]
    if table_trig:
        a = jnp.floor(pos * (1.0 / ROT_FINE))
        b = pos - ROT_FINE * a
        pick_a = jnp.where(lax.broadcasted_iota(jnp.int32, (ROT_COARSE, rows), 0).astype(jnp.float32) == a,
                           1.0, 0.0).astype(jnp.bfloat16)
        pick_b = jnp.where(lax.broadcasted_iota(jnp.int32, (ROT_FINE, rows), 0).astype(jnp.float32) == b,
                           1.0, 0.0).astype(jnp.bfloat16)
        cs_a = _dot(coarse_ref[...], jnp.concatenate([pick_a, pick_a], axis=0))
        cs_b = _dot(fine_ref[...], jnp.concatenate([pick_b, pick_b], axis=0))
        cos_t = cs_a[:n_freq] * cs_b[:n_freq] - cs_a[n_freq:] * cs_b[n_freq:]
        sin_t = cs_a[n_freq:] * cs_b[:n_freq] + cs_a[:n_freq] * cs_b[n_freq:]
    else:
        ang = jnp.concatenate([freq_ref[...]] * (rows // V7X_LANES), axis=1) * pos
        cos_t, sin_t = jnp.cos(ang), jnp.sin(ang)
    tab_da = _dot_tn(_split_bf16(jnp.concatenate([cos_t[:n_da], sin_t[:n_da]], axis=0)),
                     eda_ref[...])
    tab_rt = _dot_tn(_split_bf16(jnp.concatenate([cos_t[n_da:], sin_t[n_da:]], axis=0)),
                     ert_ref[...])
    lane1 = lax.broadcasted_iota(jnp.int32, (1, V7X_LANES), 1)
    unrotated = jnp.where(lane1 % DA_QK_DIM >= DA_ROT_DIM, 1.0, 0.0)

    def plain(idx, out_ref, lo_col, width):
        lo = SECTION_STARTS[idx] + lo_col
        out_ref[:, lo_col:lo_col + width] = _bf16(_dot(h, w_ref[:, lo:lo + width]))

    cos_rt = tab_rt[:, :V7X_LANES]
    sin_rt = tab_rt[:, V7X_LANES:]

    def retention_qk(idx, out_ref, scale):
        t = section(idx)
        for hh in range(RET_HEADS):
            th = t[:, hh * RET_QK_DIM:(hh + 1) * RET_QK_DIM]
            rot = th * cos_rt + pltpu.roll(th, RET_QK_DIM // 2, axis=1) * sin_rt
            if scale != 1.0:
                rot = rot * scale
            out_ref[:, hh * RET_QK_DIM:(hh + 1) * RET_QK_DIM] = _bf16(rot)

    reps = DA_QK_W // V7X_LANES
    cos_da = jnp.concatenate([tab_da[:, :V7X_LANES] + unrotated] * reps, axis=1)
    sin_da = jnp.concatenate([tab_da[:, V7X_LANES:]] * reps, axis=1)
    lane = lax.broadcasted_iota(jnp.int32, (rows, DA_QK_W), 1)
    first_half = (lane % DA_QK_DIM) < (DA_ROT_DIM // 2)
    ones_bd = ones_ref[...]

    def diff_attn_qk(idx, out_ref):
        t = section(idx)
        tt = _bf16(t * t)
        ms = jnp.concatenate(
            [_dot(tt[:, g:g + V7X_MXU_WIDTH], ones_bd) for g in range(0, DA_QK_W, V7X_MXU_WIDTH)],
            axis=1) * (1.0 / DA_QK_DIM)
        t = t * lax.rsqrt(ms + NORM_EPS) * qkw_ref[idx:idx + 1, :]
        partner = jnp.where(first_half,
                            pltpu.roll(t, DA_QK_W - DA_ROT_DIM // 2, axis=1),
                            pltpu.roll(t, DA_ROT_DIM // 2, axis=1))
        rot = t * cos_da + partner * sin_da
        out_ref[...] = _bf16(rot.T if idx == 0 else rot)

    retention_qk(3, rq_ref, 1.0)
    plain(2, dv_ref, 0, DA_V_W)
    retention_qk(4, rk_ref, RET_QK_DIM ** -0.5)
    plain(5, rv_ref, 0, RET_V_W // 2)
    diff_attn_qk(0, dq_ref)
    plain(5, rv_ref, RET_V_W // 2, RET_V_W // 2)
    diff_attn_qk(1, dk_ref)
    for piece in range(IN_PROJ_GATE_PIECES):
        wg, wl = RET_V_W // IN_PROJ_GATE_PIECES, GATE_W // IN_PROJ_GATE_PIECES
        lo = SECTION_STARTS[6] + piece * wg
        rg = _dot(h, w_ref[:, lo:lo + wg])
        rg_ref[:, piece * wg:(piece + 1) * wg] = _bf16(rg * jax.nn.sigmoid(rg))
        plain(7, gl_ref, piece * wl, wl)


def _rotary_constants():
    n_da, n_rt = DA_ROT_DIM // 2, RET_QK_DIM // 2
    inv_da = ROPE_THETA ** (-jnp.arange(0, DA_ROT_DIM, 2, dtype=jnp.float32) / DA_ROT_DIM)
    inv_rt = 1.0 / (RET_THETA_BASE ** jnp.linspace(0.0, 1.0, n_rt, dtype=jnp.float32))
    freqs = jnp.concatenate([inv_da, inv_rt])
    freq = jnp.broadcast_to(freqs[:, None], (n_da + n_rt, V7X_LANES))

    def trig_table(multiples):
        keep_11_bits = jnp.uint32(0xFFFFFFFF << (24 - 11) & 0xFFFFFFFF)
        bits = lax.bitcast_convert_type(freqs, jnp.uint32) & keep_11_bits
        f_hi = lax.bitcast_convert_type(bits, jnp.float32)
        big = f_hi[:, None] * multiples[None, :]
        small = (freqs - f_hi)[:, None] * multiples[None, :]
        cos = jnp.cos(big) * jnp.cos(small) - jnp.sin(big) * jnp.sin(small)
        sin = jnp.sin(big) * jnp.cos(small) + jnp.cos(big) * jnp.sin(small)
        t = jnp.concatenate([cos, sin], axis=0)
        hi = _bf16(t)
        return jnp.concatenate([hi, _bf16(t - hi.astype(jnp.float32))], axis=1)

    coarse = trig_table(ROT_FINE * jnp.arange(ROT_COARSE, dtype=jnp.float32))
    fine = trig_table(jnp.arange(ROT_FINE, dtype=jnp.float32))

    def selection(n, period):
        e = np.zeros((4 * n, 2 * V7X_LANES), np.float32)
        for f in range(n):
            for base in range(0, V7X_LANES, period):
                for part in (0, 2 * n):
                    e[part + f, base + f] = e[part + f, base + n + f] = 1.0
                    e[part + n + f, V7X_LANES + base + f] = -1.0
                    e[part + n + f, V7X_LANES + base + n + f] = 1.0
        return jnp.asarray(e, jnp.bfloat16)

    return freq, coarse, fine, selection(n_da, DA_QK_DIM), selection(n_rt, RET_QK_DIM)


def _in_proj(x2, pos, n1w, w_in, qkw, ones_bd, table_trig):
    T = x2.shape[0]
    rows = min(PROJ_ROWS, T)
    assert T % rows == 0 and rows % V7X_LANES == 0
    freq, coarse, fine, e_da, e_rt = _rotary_constants()
    pos = pos.reshape(T // rows, 1, rows)
    row_spec = lambda width: pl.BlockSpec((rows, width), lambda i: (i, 0))
    const_spec = lambda shape: pl.BlockSpec(shape, lambda i: (0, 0))
    out_widths = SECTION_WIDTHS
    return pl.pallas_call(
        functools.partial(_in_proj_kernel, table_trig=table_trig),
        grid=(T // rows,),
        in_specs=[
            row_spec(D_MODEL),
            pl.BlockSpec((1, 1, rows), lambda i: (i, 0, 0)),
            const_spec((1, D_MODEL)),
            pl.BlockSpec((D_MODEL, IN_W), lambda i: (0, 0), pipeline_mode=pl.Buffered(1)),
            const_spec((2, DA_QK_W)),
            const_spec(freq.shape), const_spec(coarse.shape), const_spec(fine.shape),
            const_spec(e_da.shape), const_spec(e_rt.shape),
            const_spec((V7X_MXU_WIDTH, V7X_MXU_WIDTH)),
        ],
        out_specs=[pl.BlockSpec((DA_QK_W, rows), lambda i: (0, i))] + [row_spec(w) for w in out_widths[1:]],
        out_shape=[jax.ShapeDtypeStruct((DA_QK_W, T), jnp.bfloat16)]
        + [jax.ShapeDtypeStruct((T, w), jnp.bfloat16) for w in out_widths[1:]],
        compiler_params=pltpu.CompilerParams(
            dimension_semantics=("parallel",), vmem_limit_bytes=V7X_VMEM_LIMIT_BYTES),
        name="in_proj",
    )(x2, pos, n1w, w_in, qkw, freq, coarse, fine, e_da, e_rt, ones_bd)


def _diff_attn_kernel(q_ref, k_ref, v_ref, lam_ref, sw_ref, o_ref,
                      qst_ref, s_ref, p_ref, acc_ref, *, qb, kb):
    qi = pl.program_id(2)
    nq = 2 * qb
    n_strip = nq // V7X_LANES
    f32 = jnp.float32

    qt = q_ref[...]
    dim = lax.broadcasted_iota(jnp.int32, qt.shape, 0)
    qst_ref[:, :qb] = jnp.where(dim < DA_QK_DIM, qt, jnp.zeros_like(qt))
    qst_ref[:, qb:] = jnp.where(dim >= DA_QK_DIM, qt, jnp.zeros_like(qt))
    acc_ref[...] = jnp.zeros(acc_ref.shape, f32)
    p_ref[1] = jnp.zeros(p_ref.shape[1:], p_ref.dtype)

    def stage_a(t, slot):
        start = pl.multiple_of(t * kb, kb)
        s_ref[slot] = _dot(k_ref[pl.ds(start, kb), :], qst_ref[...])

    def stage_b(slot, m, l, key_offset):
        m_out, l_out, a_out = [], [], []
        for c in range(n_strip):
            cols = slice(c * V7X_LANES, (c + 1) * V7X_LANES)
            st = s_ref[slot, :, cols]
            if key_offset is not None:
                krow = lax.broadcasted_iota(jnp.int32, st.shape, 0) + key_offset
                qcol = lax.broadcasted_iota(jnp.int32, st.shape, 1) + (c * V7X_LANES) % qb
                st = jnp.where(krow <= qcol, st, MASK_VALUE)
            m_new = jnp.maximum(m[c], jnp.max(st, axis=0, keepdims=True))
            alpha = jnp.exp2(m[c] - m_new)
            p = jnp.exp2(st - m_new)
            l_out.append(alpha * l[c] + jnp.sum(p, axis=0, keepdims=True))
            p_ref[slot, :, cols] = _bf16(p)
            m_out.append(m_new)
            a_out.append(alpha)
        return tuple(m_out), tuple(l_out), tuple(a_out)

    def stage_c(t, slot, alpha):
        start = pl.multiple_of(jnp.maximum(t, 0) * kb, kb)
        vt = v_ref[pl.ds(start, kb), :]
        for c in range(0, n_strip, 2):
            cols = slice(c * V7X_LANES, (c + 2) * V7X_LANES)
            a = jnp.concatenate([alpha[c], alpha[c + 1]], axis=1)
            acc_ref[:, cols] = a * acc_ref[:, cols] + _dot_tn(vt, p_ref[slot, :, cols])

    def body(u, carry):
        m, l, alpha = carry
        t = 2 * u
        stage_c(t - 1, 1, alpha)
        stage_a(t + 1, 1)
        m, l, alpha = stage_b(0, m, l, None)
        stage_c(t, 0, alpha)
        stage_a(t + 2, 0)
        m, l, alpha = stage_b(1, m, l, None)
        return m, l, alpha

    row = lambda v: tuple(jnp.full((1, V7X_LANES), v, f32) for _ in range(n_strip))
    stage_a(0, 0)
    m, l, alpha = lax.fori_loop(0, qi, body, (row(MASK_VALUE), row(0.0), row(1.0)))
    t = 2 * qi
    stage_c(t - 1, 1, alpha)
    stage_a(t + 1, 1)
    m, l, alpha = stage_b(0, m, l, 0)
    stage_c(t, 0, alpha)
    m, l, alpha = stage_b(1, m, l, kb)
    stage_c(t + 1, 1, alpha)

    o = acc_ref[...] / jnp.concatenate(l, axis=1)
    o = o[:, :qb] - lam_ref[...] * o[:, qb:]
    o = o * lax.rsqrt(jnp.mean(o * o, axis=0, keepdims=True) + SUBLN_EPS)
    o_ref[...] = _bf16(o * jnp.concatenate([sw_ref[...]] * (qb // V7X_LANES), axis=1))


def _diff_attn(dq, dk, dv, lam, subln_w, B, S):
    blk = min(ATTN_BLOCK, S)
    kb = blk // 2
    assert S % blk == 0 and kb % V7X_LANES == 0
    nq = S // blk
    kernel = functools.partial(_diff_attn_kernel, qb=blk, kb=kb)
    return pl.pallas_call(
        kernel,
        grid=(B, DA_HEADS, nq),
        scratch_shapes=[
            pltpu.VMEM((DA_V_DIM, 2 * blk), jnp.bfloat16),
            pltpu.VMEM((2, kb, 2 * blk), jnp.float32),
            pltpu.VMEM((2, kb, 2 * blk), jnp.bfloat16),
            pltpu.VMEM((DA_V_DIM, 2 * blk), jnp.float32),
        ],
        in_specs=[
            pl.BlockSpec((2 * DA_QK_DIM, blk), lambda b, h, i: (h, b * nq + i)),
            pl.BlockSpec((S, V7X_LANES), lambda b, h, i: (b, h)),
            pl.BlockSpec((S, V7X_LANES), lambda b, h, i: (b, h)),
            pl.BlockSpec((1, 1), lambda b, h, i: (0, 0)),
            pl.BlockSpec((DA_V_DIM, V7X_LANES), lambda b, h, i: (0, 0)),
        ],
        out_specs=pl.BlockSpec((DA_V_DIM, blk), lambda b, h, i: (h, b * nq + i)),
        out_shape=jax.ShapeDtypeStruct((DA_V_W, B * S), jnp.bfloat16),
        compiler_params=pltpu.CompilerParams(
            dimension_semantics=("parallel", "parallel", "arbitrary"),
            vmem_limit_bytes=V7X_VMEM_LIMIT_BYTES),
        name="diff_attn",
    )(dq, dk, dv, lam, subln_w)


def _diff_attn_bounded_kernel(q_ref, k_ref, v_ref, lam_ref, sw_ref, o_ref, qst_ref, p_ref, *, seq):
    tile = V7X_MXU_WIDTH
    n_tok_tile = seq // tile

    qt = q_ref[...]
    dim = lax.broadcasted_iota(jnp.int32, qt.shape, 0)
    qst_ref[0] = jnp.where(dim < DA_QK_DIM, qt, jnp.zeros_like(qt))
    qst_ref[1] = jnp.where(dim >= DA_QK_DIM, qt, jnp.zeros_like(qt))
    sw = jnp.concatenate([sw_ref[...]] * (tile // V7X_LANES), axis=1)

    items = [(j, m) for j in reversed(range(n_tok_tile)) for m in range(2)]
    slots = p_ref.shape[0]

    def probs(n):
        j, m = items[n]
        keys = (j + 1) * tile
        s = _dot(k_ref[:keys, :], qst_ref[m, :, j * tile:(j + 1) * tile])
        krow = lax.broadcasted_iota(jnp.int32, (tile, tile), 0)
        qcol = lax.broadcasted_iota(jnp.int32, (tile, tile), 1)
        p_diag = jnp.exp2(jnp.where(krow <= qcol, s[keys - tile:], MASK_VALUE))
        p_ref[n % slots, keys - tile:keys] = _bf16(p_diag)
        sub = V7X_SUBLANES
        part = jnp.sum(p_diag.reshape(tile // sub, sub, tile), axis=0)
        if keys > tile:
            p_full = jnp.exp2(s[:keys - tile])
            p_ref[n % slots, :keys - tile] = _bf16(p_full)
            part = part + jnp.sum(p_full.reshape((keys - tile) // sub, sub, tile), axis=0)
        return jnp.sum(part, axis=0, keepdims=True)

    def weighted_values(n):
        j, _ = items[n]
        keys = (j + 1) * tile
        return _dot_tn(v_ref[:keys, :], p_ref[n % slots, :keys])

    lsum, out = {}, {}
    for n in range(len(items) + ATTN_PV_LAG):
        if n < len(items):
            lsum[n] = probs(n)
        done = n - ATTN_PV_LAG
        if done >= 0:
            out[done] = weighted_values(done) * (1.0 / lsum.pop(done))
            j, m = items[done]
            if m == 1:
                o = out.pop(done - 1) - lam_ref[...] * out.pop(done)
                o = o * lax.rsqrt(jnp.mean(o * o, axis=0, keepdims=True) + SUBLN_EPS)
                o_ref[:, j * tile:(j + 1) * tile] = _bf16(o * sw)


def _diff_attn_bounded(dq, dk, dv, lam, subln_w, B, S):
    assert S % V7X_MXU_WIDTH == 0
    kernel = functools.partial(_diff_attn_bounded_kernel, seq=S)
    return pl.pallas_call(
        kernel,
        grid=(B, DA_HEADS),
        scratch_shapes=[
            pltpu.VMEM((2, DA_V_DIM, S), jnp.bfloat16),
            pltpu.VMEM((2 * ATTN_PV_LAG + 2, S, V7X_MXU_WIDTH), jnp.bfloat16),
        ],
        in_specs=[
            pl.BlockSpec((2 * DA_QK_DIM, S), lambda b, h: (h, b)),
            pl.BlockSpec((S, V7X_LANES), lambda b, h: (b, h)),
            pl.BlockSpec((S, V7X_LANES), lambda b, h: (b, h)),
            pl.BlockSpec((1, 1), lambda b, h: (0, 0)),
            pl.BlockSpec((DA_V_DIM, V7X_LANES), lambda b, h: (0, 0)),
        ],
        out_specs=pl.BlockSpec((DA_V_DIM, S), lambda b, h: (h, b)),
        out_shape=jax.ShapeDtypeStruct((DA_V_W, B * S), jnp.bfloat16),
        compiler_params=pltpu.CompilerParams(
            dimension_semantics=("parallel", "parallel"),
            vmem_limit_bytes=V7X_VMEM_LIMIT_BYTES),
        name="diff_attn_bounded",
    )(dq, dk, dv, lam, subln_w)


def _retention_kernel(q_ref, k_ref, v_ref, g_ref, lg_ref, *refs, chunk, n_chunks, n_side):
    side_in, o_ref, side_out = refs[:n_side], refs[n_side], refs[n_side + 1:2 * n_side + 1]
    state_ref, decay_ref, xi_ref, zeta_ref = refs[2 * n_side + 1:]
    f32 = jnp.float32
    for src, dst in zip(side_in, side_out):
        dst[...] = _bf16(src[...])

    @pl.when(pl.program_id(1) == 0)
    def _():
        state_ref[...] = jnp.zeros(state_ref.shape, f32)
        row = lax.broadcasted_iota(jnp.int32, (chunk, chunk), 0)
        col = lax.broadcasted_iota(jnp.int32, (chunk, chunk), 1)
        rel = (row - col).astype(f32)
        idx = lax.broadcasted_iota(jnp.int32, (chunk, V7X_LANES), 0).astype(f32)
        for h in range(RET_HEADS):
            lg = lg_ref[h, :, 0:1]
            decay_ref[h] = jnp.where(rel >= 0, jnp.exp(lg * jnp.maximum(rel, 0.0)), 0.0)
            xi_ref[h] = jnp.exp(lg * (idx + 1.0))
            zeta_ref[h] = jnp.exp(lg * (chunk - 1.0 - idx))

    def head_chunk(c, h):
        rows = pl.ds(c * chunk, chunk)
        qk_cols = slice(h * RET_QK_DIM, (h + 1) * RET_QK_DIM)
        v_cols = slice(h * RET_V_DIM, (h + 1) * RET_V_DIM)
        lg = lg_ref[h, :, 0:1]
        qc = q_ref[rows, qk_cols]
        kc = k_ref[rows, qk_cols]
        vc = v_ref[rows, v_cols]
        scores = _bf16(_dot_nt(qc, kc) * decay_ref[h])
        xi = jnp.concatenate([xi_ref[h]] * (RET_V_DIM // V7X_LANES), axis=1)
        o = _dot(scores, vc) + _dot(qc, _bf16(state_ref[h])) * xi
        kz = _bf16(kc.astype(f32) * zeta_ref[h])
        state_ref[h] = jnp.exp(lg * float(chunk)) * state_ref[h] + _dot_tn(kz, vc)
        d = o - jnp.mean(o, axis=-1, keepdims=True)
        var = jnp.mean(d * d, axis=-1, keepdims=True)
        on = d * lax.rsqrt(var + GN_EPS)
        o_ref[rows, v_cols] = _bf16(g_ref[rows, v_cols].astype(f32) * on)

    for c in range(n_chunks):
        for h in range(RET_HEADS):
            head_chunk(c, h)


def _retention(rq, rk, rv, rg, lg_rows, B, S, side_weights=()):
    chunk = min(RET_CHUNK, S)
    sblk = min(RET_SEQ_BLOCK, S)
    assert S % sblk == 0 and sblk % chunk == 0
    nsb = S // sblk
    n_steps = B * nsb
    bf16_rows = 2 * V7X_SUBLANES
    assert all(w.shape[0] % (n_steps * bf16_rows) == 0 for w in side_weights)
    side_specs = [pl.BlockSpec((w.shape[0] // n_steps, w.shape[1]), lambda b, s: (b * nsb + s, 0))
                  for w in side_weights]
    kernel = functools.partial(_retention_kernel, chunk=chunk, n_chunks=sblk // chunk,
                               n_side=len(side_weights))
    qk_spec = pl.BlockSpec((sblk, RET_QK_W), lambda b, s: (b * nsb + s, 0))
    v_spec = pl.BlockSpec((sblk, RET_V_W), lambda b, s: (b * nsb + s, 0))
    outs = pl.pallas_call(
        kernel,
        grid=(B, nsb),
        scratch_shapes=[
            pltpu.VMEM((RET_HEADS, RET_QK_DIM, RET_V_DIM), jnp.float32),
            pltpu.VMEM((RET_HEADS, chunk, chunk), jnp.float32),
            pltpu.VMEM((RET_HEADS, chunk, V7X_LANES), jnp.float32),
            pltpu.VMEM((RET_HEADS, chunk, V7X_LANES), jnp.float32),
        ],
        in_specs=[
            qk_spec, qk_spec, v_spec, v_spec,
            pl.BlockSpec((RET_HEADS, 1, V7X_LANES), lambda b, s: (0, 0, 0)),
        ] + side_specs,
        out_specs=[v_spec] + side_specs,
        out_shape=[jax.ShapeDtypeStruct((B * S, RET_V_W), jnp.bfloat16)]
        + [jax.ShapeDtypeStruct(w.shape, jnp.bfloat16) for w in side_weights],
        compiler_params=pltpu.CompilerParams(
            dimension_semantics=("parallel", "arbitrary"),
            vmem_limit_bytes=V7X_VMEM_LIMIT_BYTES),
        name="retention",
    )(rq, rk, rv, rg, lg_rows, *side_weights)
    return outs[0], tuple(outs[1:])


def _out_mlp_kernel(x_ref, oa_ref, or_ref, gl_ref, wa_ref, wr_ref, wo_ref, n2w_ref, w1_ref, w2_ref,
                    o_ref):
    ya = _dot_tn(oa_ref[...], wa_ref[...])
    yr = _dot(or_ref[...], wr_ref[...])
    ga = jax.nn.sigmoid(gl_ref[:, :D_MODEL].astype(jnp.float32))
    gr = jax.nn.sigmoid(gl_ref[:, D_MODEL:].astype(jnp.float32))
    merged = _bf16(ga * ya + gr * yr)
    x1 = x_ref[...] + _dot(merged, wo_ref[...])
    h2 = x1 * lax.rsqrt(jnp.mean(x1 * x1, axis=-1, keepdims=True) + NORM_EPS) * n2w_ref[...]
    h2 = _bf16(h2)
    acc = x1
    for c in range(D_FF // MLP_FF_CHUNK):
        lo = c * MLP_FF_CHUNK
        hid = jnp.maximum(_dot(h2, w1_ref[:, lo:lo + MLP_FF_CHUNK]), 0.0)
        acc = acc + _dot(_bf16(hid * hid), w2_ref[lo:lo + MLP_FF_CHUNK, :])
    o_ref[...] = acc


def _out_mlp(x2, oa, orr, gl, wa, wr, wo, n2w, w1, w2):
    T = x2.shape[0]
    rows = min(MLP_ROWS, T)
    assert T % rows == 0
    row_spec = lambda width: pl.BlockSpec((rows, width), lambda i: (i, 0))
    weight_spec = lambda shape: pl.BlockSpec(shape, lambda i: (0, 0), pipeline_mode=pl.Buffered(1))
    return pl.pallas_call(
        _out_mlp_kernel,
        grid=(T // rows,),
        in_specs=[
            row_spec(D_MODEL), pl.BlockSpec((DA_V_W, rows), lambda i: (0, i)),
            row_spec(RET_V_W), row_spec(GATE_W),
            weight_spec((DA_V_W, D_MODEL)), weight_spec((RET_V_W, D_MODEL)),
            weight_spec((D_MODEL, D_MODEL)),
            pl.BlockSpec((1, D_MODEL), lambda i: (0, 0)),
            weight_spec((D_MODEL, D_FF)), weight_spec((D_FF, D_MODEL)),
        ],
        out_specs=row_spec(D_MODEL),
        out_shape=jax.ShapeDtypeStruct((T, D_MODEL), jnp.float32),
        compiler_params=pltpu.CompilerParams(
            dimension_semantics=("parallel",), vmem_limit_bytes=V7X_VMEM_LIMIT_BYTES),
        name="out_mlp",
    )(x2, oa, orr, gl, wa, wr, wo, n2w, w1, w2)


def kernel(x, positions, norm1_w, w_in, q_norm_w, k_norm_w, lambda_q1, lambda_k1, lambda_q2,
           lambda_k2, da_subln_w, w_da_branch, ret_gn_w, w_ret_branch, w_out, norm2_w, w_mlp_in,
           w_mlp_out):
    B, S, _ = x.shape
    T = B * S
    f32 = jnp.float32
    x2 = x.reshape(T, D_MODEL)

    q_scale = (DA_QK_DIM ** -0.5) * LOG2_E
    qkw = jnp.stack([jnp.tile(q_norm_w[0].astype(f32) * q_scale, DA_QK_W // DA_QK_DIM),
                     jnp.tile(k_norm_w[0].astype(f32), DA_QK_W // DA_QK_DIM)])
    blk = np.arange(V7X_MXU_WIDTH) // DA_QK_DIM
    ones_bd = jnp.asarray(blk[:, None] == blk[None, :], jnp.bfloat16)

    in_table = jnp.logical_and(jnp.min(positions) >= 0, jnp.max(positions) < ROT_FINE * ROT_COARSE)
    in_proj_args = (x2, positions.astype(f32), norm1_w[0].astype(f32).reshape(1, D_MODEL),
                    _bf16(w_in[0]), qkw, ones_bd)
    dq, dk, dv, rq, rk, rv, rg, gl = lax.cond(
        in_table, functools.partial(_in_proj, table_trig=True),
        functools.partial(_in_proj, table_trig=False), *in_proj_args)

    lam = (jnp.exp(jnp.sum(lambda_q1[0].astype(f32) * lambda_k1[0].astype(f32)))
           - jnp.exp(jnp.sum(lambda_q2[0].astype(f32) * lambda_k2[0].astype(f32)))
           + LAMBDA_INIT).reshape(1, 1)
    subln = jnp.broadcast_to((da_subln_w[0].astype(f32) * (1.0 - LAMBDA_INIT))[:, None],
                             (DA_V_DIM, V7X_LANES))
    logit_bound = (DA_QK_DIM * jnp.max(jnp.abs(qkw[0])) * jnp.max(jnp.abs(qkw[1]))
                   * ATTN_BOUND_MARGIN)
    oa = lax.cond(logit_bound <= ATTN_BOUNDED_LOGIT,
                  functools.partial(_diff_attn_bounded, B=B, S=S),
                  functools.partial(_diff_attn, B=B, S=S),
                  dq, dk, dv, lam, subln)

    log_gamma = np.log1p(-np.exp2(-5.0 - np.arange(RET_HEADS, dtype=np.float64)))
    lg_rows = jnp.asarray(np.broadcast_to(log_gamma[:, None, None], (RET_HEADS, 1, V7X_LANES)), f32)
    later_weights = tuple(w[0].astype(f32) for w in (w_out, w_mlp_in, w_mlp_out))
    ret_steps = B * (S // min(RET_SEQ_BLOCK, S))
    if all(w.shape[0] % (ret_steps * 2 * V7X_SUBLANES) == 0 for w in later_weights):
        orr, (w_out_b, w_mlp_in_b, w_mlp_out_b) = _retention(rq, rk, rv, rg, lg_rows, B, S, later_weights)
    else:
        orr, _ = _retention(rq, rk, rv, rg, lg_rows, B, S)
        w_out_b, w_mlp_in_b, w_mlp_out_b = (_bf16(w) for w in later_weights)

    w_ret = _bf16(ret_gn_w[0].astype(f32)[:, None] * w_ret_branch[0].astype(f32))
    out = _out_mlp(x2, oa, orr, gl, _bf16(w_da_branch[0]), w_ret, w_out_b,
                   norm2_w[0].astype(f32).reshape(1, D_MODEL), w_mlp_in_b, w_mlp_out_b)
    return out.reshape(B, S, D_MODEL)
```

```python
import functools
import math

import jax
import jax.numpy as jnp
import numpy as np
from jax import lax
from jax.experimental import pallas as pl
from jax.experimental.pallas import tpu as pltpu

D_MODEL = 1024
DA_HEADS = 4
DA_QK_DIM = 64
DA_V_DIM = 2 * DA_QK_DIM
DA_ROT_DIM = DA_QK_DIM // 4
ROPE_THETA = 500000.0
RET_HEADS = 4
RET_QK_DIM = 128
RET_V_DIM = 256
RET_THETA_BASE = 10000.0
D_FF = 4 * D_MODEL
NORM_EPS = 1e-6
SUBLN_EPS = 1e-5
GN_EPS = 1e-5
MASK_VALUE = -1e30
LAMBDA_INIT = 0.8 - 0.6 * math.exp(-0.3 * 0)

DA_QK_W = DA_HEADS * 2 * DA_QK_DIM
DA_V_W = DA_HEADS * DA_V_DIM
RET_QK_W = RET_HEADS * RET_QK_DIM
RET_V_W = RET_HEADS * RET_V_DIM
GATE_W = 2 * D_MODEL
SECTION_WIDTHS = (DA_QK_W, DA_QK_W, DA_V_W, RET_QK_W, RET_QK_W, RET_V_W, RET_V_W, GATE_W)
SECTION_STARTS = tuple(int(v) for v in np.cumsum((0,) + SECTION_WIDTHS[:-1]))
IN_W = sum(SECTION_WIDTHS)

V7X_LANES = 128
V7X_SUBLANES = 8
V7X_MXU_WIDTH = 256
V7X_VMEM_LIMIT_BYTES = 56 * 1024 * 1024

LOG2_E = math.log2(math.e)

PROJ_ROWS = 512
IN_PROJ_GATE_PIECES = 4
ROT_FINE = 64
ROT_COARSE = 128
ATTN_BLOCK = 512
ATTN_BOUNDED_LOGIT = 60.0
ATTN_PV_LAG = 4
ATTN_BOUND_MARGIN = 1.0 + 2.0 ** -6
RET_CHUNK = 256
RET_SEQ_BLOCK = 1024
MLP_ROWS = 512
MLP_FF_CHUNK = 1024


def _bf16(t):
    return t.astype(jnp.bfloat16)


def _dot(a, b):
    return jnp.dot(a, b, preferred_element_type=jnp.float32)


def _dot_nt(a, b):
    return lax.dot_general(a, b, (((1,), (1,)), ((), ())), preferred_element_type=jnp.float32)


def _dot_tn(a, b):
    return lax.dot_general(a, b, (((0,), (0,)), ((), ())), preferred_element_type=jnp.float32)


def _split_bf16(t):
    hi = _bf16(t)
    lo = _bf16(t - hi.astype(jnp.float32))
    return jnp.concatenate([hi, lo], axis=0)


def _in_proj_kernel(x_ref, pos_ref, n1w_ref, w_ref, qkw_ref, freq_ref, coarse_ref, fine_ref, eda_ref,
                    ert_ref, ones_ref, *refs, table_trig):
    n_side = (len(refs) - len(SECTION_WIDTHS)) // 2
    side_in, side_out = refs[:n_side], refs[n_side + len(SECTION_WIDTHS):]
    dq_ref, dk_ref, dv_ref, rq_ref, rk_ref, rv_ref, rg_ref, gl_ref = refs[n_side:n_side + len(SECTION_WIDTHS)]
    for src, dst in zip(side_in, side_out):
        dst[...] = _bf16(src[...])
    x = x_ref[...]
    h = x * lax.rsqrt(jnp.mean(x * x, axis=-1, keepdims=True) + NORM_EPS) * n1w_ref[...]
    h = _bf16(h)

    def section(idx):
        lo = SECTION_STARTS[idx]
        return _dot(h, w_ref[:, lo:lo + SECTION_WIDTHS[idx]])

    rows = x.shape[0]
    n_da = DA_ROT_DIM // 2
    n_freq = freq_ref.shape[0]
    pos = pos_ref[0]
    if table_trig:
        a = jnp.floor(pos * (1.0 / ROT_FINE))
        b = pos - ROT_FINE * a
        pick_a = jnp.where(lax.broadcasted_iota(jnp.int32, (ROT_COARSE, rows), 0).astype(jnp.float32) == a,
                           1.0, 0.0).astype(jnp.bfloat16)
        pick_b = jnp.where(lax.broadcasted_iota(jnp.int32, (ROT_FINE, rows), 0).astype(jnp.float32) == b,
                           1.0, 0.0).astype(jnp.bfloat16)
        cs_a = _dot(coarse_ref[...], jnp.concatenate([pick_a, pick_a], axis=0))
        cs_b = _dot(fine_ref[...], jnp.concatenate([pick_b, pick_b], axis=0))
        cos_t = cs_a[:n_freq] * cs_b[:n_freq] - cs_a[n_freq:] * cs_b[n_freq:]
        sin_t = cs_a[n_freq:] * cs_b[:n_freq] + cs_a[:n_freq] * cs_b[n_freq:]
    else:
        ang = jnp.concatenate([freq_ref[...]] * (rows // V7X_LANES), axis=1) * pos
        cos_t, sin_t = jnp.cos(ang), jnp.sin(ang)
    tab_da = _dot_tn(_split_bf16(jnp.concatenate([cos_t[:n_da], sin_t[:n_da]], axis=0)),
                     eda_ref[...])
    tab_rt = _dot_tn(_split_bf16(jnp.concatenate([cos_t[n_da:], sin_t[n_da:]], axis=0)),
                     ert_ref[...])
    lane1 = lax.broadcasted_iota(jnp.int32, (1, V7X_LANES), 1)
    unrotated = jnp.where(lane1 % DA_QK_DIM >= DA_ROT_DIM, 1.0, 0.0)

    def plain(idx, out_ref, lo_col, width):
        lo = SECTION_STARTS[idx] + lo_col
        out_ref[:, lo_col:lo_col + width] = _bf16(_dot(h, w_ref[:, lo:lo + width]))

    cos_rt = tab_rt[:, :V7X_LANES]
    sin_rt = tab_rt[:, V7X_LANES:]

    def retention_qk(idx, out_ref, scale):
        t = section(idx)
        for hh in range(RET_HEADS):
            th = t[:, hh * RET_QK_DIM:(hh + 1) * RET_QK_DIM]
            rot = th * cos_rt + pltpu.roll(th, RET_QK_DIM // 2, axis=1) * sin_rt
            if scale != 1.0:
                rot = rot * scale
            out_ref[:, hh * RET_QK_DIM:(hh + 1) * RET_QK_DIM] = _bf16(rot)

    reps = DA_QK_W // V7X_LANES
    cos_da = jnp.concatenate([tab_da[:, :V7X_LANES] + unrotated] * reps, axis=1)
    sin_da = jnp.concatenate([tab_da[:, V7X_LANES:]] * reps, axis=1)
    lane = lax.broadcasted_iota(jnp.int32, (rows, DA_QK_W), 1)
    first_half = (lane % DA_QK_DIM) < (DA_ROT_DIM // 2)
    ones_bd = ones_ref[...]

    def diff_attn_qk(idx, out_ref):
        t = section(idx)
        tt = _bf16(t * t)
        ms = jnp.concatenate(
            [_dot(tt[:, g:g + V7X_MXU_WIDTH], ones_bd) for g in range(0, DA_QK_W, V7X_MXU_WIDTH)],
            axis=1) * (1.0 / DA_QK_DIM)
        t = t * lax.rsqrt(ms + NORM_EPS) * qkw_ref[idx:idx + 1, :]
        partner = jnp.where(first_half,
                            pltpu.roll(t, DA_QK_W - DA_ROT_DIM // 2, axis=1),
                            pltpu.roll(t, DA_ROT_DIM // 2, axis=1))
        rot = t * cos_da + partner * sin_da
        out_ref[...] = _bf16(rot.T if idx == 0 else rot)

    retention_qk(3, rq_ref, 1.0)
    plain(2, dv_ref, 0, DA_V_W)
    retention_qk(4, rk_ref, RET_QK_DIM ** -0.5)
    plain(5, rv_ref, 0, RET_V_W // 2)
    diff_attn_qk(0, dq_ref)
    plain(5, rv_ref, RET_V_W // 2, RET_V_W // 2)
    diff_attn_qk(1, dk_ref)
    for piece in range(IN_PROJ_GATE_PIECES):
        wg, wl = RET_V_W // IN_PROJ_GATE_PIECES, GATE_W // IN_PROJ_GATE_PIECES
        lo = SECTION_STARTS[6] + piece * wg
        rg = _dot(h, w_ref[:, lo:lo + wg])
        rg_ref[:, piece * wg:(piece + 1) * wg] = _bf16(rg * jax.nn.sigmoid(rg))
        plain(7, gl_ref, piece * wl, wl)


def _rotary_constants():
    n_da, n_rt = DA_ROT_DIM // 2, RET_QK_DIM // 2
    inv_da = ROPE_THETA ** (-jnp.arange(0, DA_ROT_DIM, 2, dtype=jnp.float32) / DA_ROT_DIM)
    inv_rt = 1.0 / (RET_THETA_BASE ** jnp.linspace(0.0, 1.0, n_rt, dtype=jnp.float32))
    freqs = jnp.concatenate([inv_da, inv_rt])
    freq = jnp.broadcast_to(freqs[:, None], (n_da + n_rt, V7X_LANES))

    def trig_table(multiples):
        keep_11_bits = jnp.uint32(0xFFFFFFFF << (24 - 11) & 0xFFFFFFFF)
        bits = lax.bitcast_convert_type(freqs, jnp.uint32) & keep_11_bits
        f_hi = lax.bitcast_convert_type(bits, jnp.float32)
        big = f_hi[:, None] * multiples[None, :]
        small = (freqs - f_hi)[:, None] * multiples[None, :]
        cos = jnp.cos(big) * jnp.cos(small) - jnp.sin(big) * jnp.sin(small)
        sin = jnp.sin(big) * jnp.cos(small) + jnp.cos(big) * jnp.sin(small)
        t = jnp.concatenate([cos, sin], axis=0)
        hi = _bf16(t)
        return jnp.concatenate([hi, _bf16(t - hi.astype(jnp.float32))], axis=1)

    coarse = trig_table(ROT_FINE * jnp.arange(ROT_COARSE, dtype=jnp.float32))
    fine = trig_table(jnp.arange(ROT_FINE, dtype=jnp.float32))

    def selection(n, period):
        e = np.zeros((4 * n, 2 * V7X_LANES), np.float32)
        for f in range(n):
            for base in range(0, V7X_LANES, period):
                for part in (0, 2 * n):
                    e[part + f, base + f] = e[part + f, base + n + f] = 1.0
                    e[part + n + f, V7X_LANES + base + f] = -1.0
                    e[part + n + f, V7X_LANES + base + n + f] = 1.0
        return jnp.asarray(e, jnp.bfloat16)

    return freq, coarse, fine, selection(n_da, DA_QK_DIM), selection(n_rt, RET_QK_DIM)


def _in_proj(x2, pos, n1w, w_in, qkw, ones_bd, *side_weights, table_trig):
    T = x2.shape[0]
    rows = min(PROJ_ROWS, T)
    assert T % rows == 0 and rows % V7X_LANES == 0
    side_specs = [pl.BlockSpec((w.shape[0] // (T // rows), w.shape[1]), lambda i: (i, 0))
                  for w in side_weights]
    freq, coarse, fine, e_da, e_rt = _rotary_constants()
    pos = pos.reshape(T // rows, 1, rows)
    row_spec = lambda width: pl.BlockSpec((rows, width), lambda i: (i, 0))
    const_spec = lambda shape: pl.BlockSpec(shape, lambda i: (0, 0))
    out_widths = SECTION_WIDTHS
    return pl.pallas_call(
        functools.partial(_in_proj_kernel, table_trig=table_trig),
        grid=(T // rows,),
        in_specs=[
            row_spec(D_MODEL),
            pl.BlockSpec((1, 1, rows), lambda i: (i, 0, 0)),
            const_spec((1, D_MODEL)),
            pl.BlockSpec((D_MODEL, IN_W), lambda i: (0, 0), pipeline_mode=pl.Buffered(1)),
            const_spec((2, DA_QK_W)),
            const_spec(freq.shape), const_spec(coarse.shape), const_spec(fine.shape),
            const_spec(e_da.shape), const_spec(e_rt.shape),
            const_spec((V7X_MXU_WIDTH, V7X_MXU_WIDTH)),
        ] + side_specs,
        out_specs=[pl.BlockSpec((DA_QK_W, rows), lambda i: (0, i))] + [row_spec(w) for w in out_widths[1:]]
        + side_specs,
        out_shape=[jax.ShapeDtypeStruct((DA_QK_W, T), jnp.bfloat16)]
        + [jax.ShapeDtypeStruct((T, w), jnp.bfloat16) for w in out_widths[1:]]
        + [jax.ShapeDtypeStruct(w.shape, jnp.bfloat16) for w in side_weights],
        compiler_params=pltpu.CompilerParams(
            dimension_semantics=("parallel",), vmem_limit_bytes=V7X_VMEM_LIMIT_BYTES),
        name="in_proj",
    )(x2, pos, n1w, w_in, qkw, freq, coarse, fine, e_da, e_rt, ones_bd, *side_weights)


def _diff_attn_kernel(q_ref, k_ref, v_ref, lam_ref, sw_ref, o_ref,
                      qst_ref, s_ref, p_ref, acc_ref, *, qb, kb):
    qi = pl.program_id(2)
    nq = 2 * qb
    n_strip = nq // V7X_LANES
    f32 = jnp.float32

    qt = q_ref[...]
    dim = lax.broadcasted_iota(jnp.int32, qt.shape, 0)
    qst_ref[:, :qb] = jnp.where(dim < DA_QK_DIM, qt, jnp.zeros_like(qt))
    qst_ref[:, qb:] = jnp.where(dim >= DA_QK_DIM, qt, jnp.zeros_like(qt))
    acc_ref[...] = jnp.zeros(acc_ref.shape, f32)
    p_ref[1] = jnp.zeros(p_ref.shape[1:], p_ref.dtype)

    def stage_a(t, slot):
        start = pl.multiple_of(t * kb, kb)
        s_ref[slot] = _dot(k_ref[pl.ds(start, kb), :], qst_ref[...])

    def stage_b(slot, m, l, key_offset):
        m_out, l_out, a_out = [], [], []
        for c in range(n_strip):
            cols = slice(c * V7X_LANES, (c + 1) * V7X_LANES)
            st = s_ref[slot, :, cols]
            if key_offset is not None:
                krow = lax.broadcasted_iota(jnp.int32, st.shape, 0) + key_offset
                qcol = lax.broadcasted_iota(jnp.int32, st.shape, 1) + (c * V7X_LANES) % qb
                st = jnp.where(krow <= qcol, st, MASK_VALUE)
            m_new = jnp.maximum(m[c], jnp.max(st, axis=0, keepdims=True))
            alpha = jnp.exp2(m[c] - m_new)
            p = jnp.exp2(st - m_new)
            l_out.append(alpha * l[c] + jnp.sum(p, axis=0, keepdims=True))
            p_ref[slot, :, cols] = _bf16(p)
            m_out.append(m_new)
            a_out.append(alpha)
        return tuple(m_out), tuple(l_out), tuple(a_out)

    def stage_c(t, slot, alpha):
        start = pl.multiple_of(jnp.maximum(t, 0) * kb, kb)
        vt = v_ref[pl.ds(start, kb), :]
        for c in range(0, n_strip, 2):
            cols = slice(c * V7X_LANES, (c + 2) * V7X_LANES)
            a = jnp.concatenate([alpha[c], alpha[c + 1]], axis=1)
            acc_ref[:, cols] = a * acc_ref[:, cols] + _dot_tn(vt, p_ref[slot, :, cols])

    def body(u, carry):
        m, l, alpha = carry
        t = 2 * u
        stage_c(t - 1, 1, alpha)
        stage_a(t + 1, 1)
        m, l, alpha = stage_b(0, m, l, None)
        stage_c(t, 0, alpha)
        stage_a(t + 2, 0)
        m, l, alpha = stage_b(1, m, l, None)
        return m, l, alpha

    row = lambda v: tuple(jnp.full((1, V7X_LANES), v, f32) for _ in range(n_strip))
    stage_a(0, 0)
    m, l, alpha = lax.fori_loop(0, qi, body, (row(MASK_VALUE), row(0.0), row(1.0)))
    t = 2 * qi
    stage_c(t - 1, 1, alpha)
    stage_a(t + 1, 1)
    m, l, alpha = stage_b(0, m, l, 0)
    stage_c(t, 0, alpha)
    m, l, alpha = stage_b(1, m, l, kb)
    stage_c(t + 1, 1, alpha)

    o = acc_ref[...] / jnp.concatenate(l, axis=1)
    o = o[:, :qb] - lam_ref[...] * o[:, qb:]
    o = o * lax.rsqrt(jnp.mean(o * o, axis=0, keepdims=True) + SUBLN_EPS)
    o_ref[...] = _bf16(o * jnp.concatenate([sw_ref[...]] * (qb // V7X_LANES), axis=1))


def _diff_attn(dq, dk, dv, lam, subln_w, B, S):
    blk = min(ATTN_BLOCK, S)
    kb = blk // 2
    assert S % blk == 0 and kb % V7X_LANES == 0
    nq = S // blk
    kernel = functools.partial(_diff_attn_kernel, qb=blk, kb=kb)
    return pl.pallas_call(
        kernel,
        grid=(B, DA_HEADS, nq),
        scratch_shapes=[
            pltpu.VMEM((DA_V_DIM, 2 * blk), jnp.bfloat16),
            pltpu.VMEM((2, kb, 2 * blk), jnp.float32),
            pltpu.VMEM((2, kb, 2 * blk), jnp.bfloat16),
            pltpu.VMEM((DA_V_DIM, 2 * blk), jnp.float32),
        ],
        in_specs=[
            pl.BlockSpec((2 * DA_QK_DIM, blk), lambda b, h, i: (h, b * nq + i)),
            pl.BlockSpec((S, V7X_LANES), lambda b, h, i: (b, h)),
            pl.BlockSpec((S, V7X_LANES), lambda b, h, i: (b, h)),
            pl.BlockSpec((1, 1), lambda b, h, i: (0, 0)),
            pl.BlockSpec((DA_V_DIM, V7X_LANES), lambda b, h, i: (0, 0)),
        ],
        out_specs=pl.BlockSpec((DA_V_DIM, blk), lambda b, h, i: (h, b * nq + i)),
        out_shape=jax.ShapeDtypeStruct((DA_V_W, B * S), jnp.bfloat16),
        compiler_params=pltpu.CompilerParams(
            dimension_semantics=("parallel", "parallel", "arbitrary"),
            vmem_limit_bytes=V7X_VMEM_LIMIT_BYTES),
        name="diff_attn",
    )(dq, dk, dv, lam, subln_w)


def _diff_attn_bounded_kernel(q_ref, k_ref, v_ref, lam_ref, sw_ref, o_ref, qst_ref, p_ref, *, seq):
    tile = V7X_MXU_WIDTH
    n_tok_tile = seq // tile

    qt = q_ref[...]
    dim = lax.broadcasted_iota(jnp.int32, qt.shape, 0)
    qst_ref[0] = jnp.where(dim < DA_QK_DIM, qt, jnp.zeros_like(qt))
    qst_ref[1] = jnp.where(dim >= DA_QK_DIM, qt, jnp.zeros_like(qt))
    sw = jnp.concatenate([sw_ref[...]] * (tile // V7X_LANES), axis=1)

    items = [(j, m) for j in reversed(range(n_tok_tile)) for m in range(2)]
    slots = p_ref.shape[0]

    def probs(n):
        j, m = items[n]
        keys = (j + 1) * tile
        s = _dot(k_ref[:keys, :], qst_ref[m, :, j * tile:(j + 1) * tile])
        krow = lax.broadcasted_iota(jnp.int32, (tile, tile), 0)
        qcol = lax.broadcasted_iota(jnp.int32, (tile, tile), 1)
        p_diag = jnp.exp2(jnp.where(krow <= qcol, s[keys - tile:], MASK_VALUE))
        p_ref[n % slots, keys - tile:keys] = _bf16(p_diag)
        sub = V7X_SUBLANES
        part = jnp.sum(p_diag.reshape(tile // sub, sub, tile), axis=0)
        if keys > tile:
            p_full = jnp.exp2(s[:keys - tile])
            p_ref[n % slots, :keys - tile] = _bf16(p_full)
            part = part + jnp.sum(p_full.reshape((keys - tile) // sub, sub, tile), axis=0)
        return jnp.sum(part, axis=0, keepdims=True)

    def weighted_values(n):
        j, _ = items[n]
        keys = (j + 1) * tile
        return _dot_tn(v_ref[:keys, :], p_ref[n % slots, :keys])

    lsum, out = {}, {}
    for n in range(len(items) + ATTN_PV_LAG):
        if n < len(items):
            lsum[n] = probs(n)
        done = n - ATTN_PV_LAG
        if done >= 0:
            out[done] = weighted_values(done) * (1.0 / lsum.pop(done))
            j, m = items[done]
            if m == 1:
                o = out.pop(done - 1) - lam_ref[...] * out.pop(done)
                o = o * lax.rsqrt(jnp.mean(o * o, axis=0, keepdims=True) + SUBLN_EPS)
                o_ref[:, j * tile:(j + 1) * tile] = _bf16(o * sw)


def _diff_attn_bounded(dq, dk, dv, lam, subln_w, B, S):
    assert S % V7X_MXU_WIDTH == 0
    kernel = functools.partial(_diff_attn_bounded_kernel, seq=S)
    return pl.pallas_call(
        kernel,
        grid=(B, DA_HEADS),
        scratch_shapes=[
            pltpu.VMEM((2, DA_V_DIM, S), jnp.bfloat16),
            pltpu.VMEM((2 * ATTN_PV_LAG + 2, S, V7X_MXU_WIDTH), jnp.bfloat16),
        ],
        in_specs=[
            pl.BlockSpec((2 * DA_QK_DIM, S), lambda b, h: (h, b)),
            pl.BlockSpec((S, V7X_LANES), lambda b, h: (b, h)),
            pl.BlockSpec((S, V7X_LANES), lambda b, h: (b, h)),
            pl.BlockSpec((1, 1), lambda b, h: (0, 0)),
            pl.BlockSpec((DA_V_DIM, V7X_LANES), lambda b, h: (0, 0)),
        ],
        out_specs=pl.BlockSpec((DA_V_DIM, S), lambda b, h: (h, b)),
        out_shape=jax.ShapeDtypeStruct((DA_V_W, B * S), jnp.bfloat16),
        compiler_params=pltpu.CompilerParams(
            dimension_semantics=("parallel", "parallel"),
            vmem_limit_bytes=V7X_VMEM_LIMIT_BYTES),
        name="diff_attn_bounded",
    )(dq, dk, dv, lam, subln_w)


def _retention_kernel(q_ref, k_ref, v_ref, g_ref, lg_ref, *refs, chunk, n_chunks, n_side):
    side_in, o_ref, side_out = refs[:n_side], refs[n_side], refs[n_side + 1:2 * n_side + 1]
    state_ref, decay_ref, xi_ref, zeta_ref = refs[2 * n_side + 1:]
    f32 = jnp.float32
    for src, dst in zip(side_in, side_out):
        dst[...] = _bf16(src[...])

    @pl.when(pl.program_id(1) == 0)
    def _():
        state_ref[...] = jnp.zeros(state_ref.shape, f32)
        row = lax.broadcasted_iota(jnp.int32, (chunk, chunk), 0)
        col = lax.broadcasted_iota(jnp.int32, (chunk, chunk), 1)
        rel = (row - col).astype(f32)
        idx = lax.broadcasted_iota(jnp.int32, (chunk, V7X_LANES), 0).astype(f32)
        for h in range(RET_HEADS):
            lg = lg_ref[h, :, 0:1]
            decay_ref[h] = jnp.where(rel >= 0, jnp.exp(lg * jnp.maximum(rel, 0.0)), 0.0)
            xi_ref[h] = jnp.exp(lg * (idx + 1.0))
            zeta_ref[h] = jnp.exp(lg * (chunk - 1.0 - idx))

    def head_chunk(c, h):
        rows = pl.ds(c * chunk, chunk)
        qk_cols = slice(h * RET_QK_DIM, (h + 1) * RET_QK_DIM)
        v_cols = slice(h * RET_V_DIM, (h + 1) * RET_V_DIM)
        lg = lg_ref[h, :, 0:1]
        qc = q_ref[rows, qk_cols]
        kc = k_ref[rows, qk_cols]
        vc = v_ref[rows, v_cols]
        scores = _bf16(_dot_nt(qc, kc) * decay_ref[h])
        xi = jnp.concatenate([xi_ref[h]] * (RET_V_DIM // V7X_LANES), axis=1)
        o = _dot(scores, vc) + _dot(qc, _bf16(state_ref[h])) * xi
        kz = _bf16(kc.astype(f32) * zeta_ref[h])
        state_ref[h] = jnp.exp(lg * float(chunk)) * state_ref[h] + _dot_tn(kz, vc)
        d = o - jnp.mean(o, axis=-1, keepdims=True)
        var = jnp.mean(d * d, axis=-1, keepdims=True)
        on = d * lax.rsqrt(var + GN_EPS)
        o_ref[rows, v_cols] = _bf16(g_ref[rows, v_cols].astype(f32) * on)

    for c in range(n_chunks):
        for h in range(RET_HEADS):
            head_chunk(c, h)


def _retention(rq, rk, rv, rg, lg_rows, B, S, side_weights=()):
    chunk = min(RET_CHUNK, S)
    sblk = min(RET_SEQ_BLOCK, S)
    assert S % sblk == 0 and sblk % chunk == 0
    nsb = S // sblk
    n_steps = B * nsb
    bf16_rows = 2 * V7X_SUBLANES
    assert all(w.shape[0] % (n_steps * bf16_rows) == 0 for w in side_weights)
    side_specs = [pl.BlockSpec((w.shape[0] // n_steps, w.shape[1]), lambda b, s: (b * nsb + s, 0))
                  for w in side_weights]
    kernel = functools.partial(_retention_kernel, chunk=chunk, n_chunks=sblk // chunk,
                               n_side=len(side_weights))
    qk_spec = pl.BlockSpec((sblk, RET_QK_W), lambda b, s: (b * nsb + s, 0))
    v_spec = pl.BlockSpec((sblk, RET_V_W), lambda b, s: (b * nsb + s, 0))
    outs = pl.pallas_call(
        kernel,
        grid=(B, nsb),
        scratch_shapes=[
            pltpu.VMEM((RET_HEADS, RET_QK_DIM, RET_V_DIM), jnp.float32),
            pltpu.VMEM((RET_HEADS, chunk, chunk), jnp.float32),
            pltpu.VMEM((RET_HEADS, chunk, V7X_LANES), jnp.float32),
            pltpu.VMEM((RET_HEADS, chunk, V7X_LANES), jnp.float32),
        ],
        in_specs=[
            qk_spec, qk_spec, v_spec, v_spec,
            pl.BlockSpec((RET_HEADS, 1, V7X_LANES), lambda b, s: (0, 0, 0)),
        ] + side_specs,
        out_specs=[v_spec] + side_specs,
        out_shape=[jax.ShapeDtypeStruct((B * S, RET_V_W), jnp.bfloat16)]
        + [jax.ShapeDtypeStruct(w.shape, jnp.bfloat16) for w in side_weights],
        compiler_params=pltpu.CompilerParams(
            dimension_semantics=("parallel", "arbitrary"),
            vmem_limit_bytes=V7X_VMEM_LIMIT_BYTES),
        name="retention",
    )(rq, rk, rv, rg, lg_rows, *side_weights)
    return outs[0], tuple(outs[1:])


def _out_mlp_kernel(x_ref, oa_ref, or_ref, gl_ref, wa_ref, wr_ref, wo_ref, n2w_ref, w1_ref, w2_ref,
                    o_ref):
    ya = _dot_tn(oa_ref[...], wa_ref[...])
    yr = _dot(or_ref[...], wr_ref[...])
    ga = jax.nn.sigmoid(gl_ref[:, :D_MODEL].astype(jnp.float32))
    gr = jax.nn.sigmoid(gl_ref[:, D_MODEL:].astype(jnp.float32))
    merged = _bf16(ga * ya + gr * yr)
    x1 = x_ref[...] + _dot(merged, wo_ref[...])
    h2 = x1 * lax.rsqrt(jnp.mean(x1 * x1, axis=-1, keepdims=True) + NORM_EPS) * n2w_ref[...]
    h2 = _bf16(h2)
    acc = x1
    for c in range(D_FF // MLP_FF_CHUNK):
        lo = c * MLP_FF_CHUNK
        hid = jnp.maximum(_dot(h2, w1_ref[:, lo:lo + MLP_FF_CHUNK]), 0.0)
        acc = acc + _dot(_bf16(hid * hid), w2_ref[lo:lo + MLP_FF_CHUNK, :])
    o_ref[...] = acc


def _out_mlp(x2, oa, orr, gl, wa, wr, wo, n2w, w1, w2):
    T = x2.shape[0]
    rows = min(MLP_ROWS, T)
    assert T % rows == 0
    row_spec = lambda width: pl.BlockSpec((rows, width), lambda i: (i, 0))
    weight_spec = lambda shape: pl.BlockSpec(shape, lambda i: (0, 0), pipeline_mode=pl.Buffered(1))
    return pl.pallas_call(
        _out_mlp_kernel,
        grid=(T // rows,),
        in_specs=[
            row_spec(D_MODEL), pl.BlockSpec((DA_V_W, rows), lambda i: (0, i)),
            row_spec(RET_V_W), row_spec(GATE_W),
            weight_spec((DA_V_W, D_MODEL)), weight_spec((RET_V_W, D_MODEL)),
            weight_spec((D_MODEL, D_MODEL)),
            pl.BlockSpec((1, D_MODEL), lambda i: (0, 0)),
            weight_spec((D_MODEL, D_FF)), weight_spec((D_FF, D_MODEL)),
        ],
        out_specs=row_spec(D_MODEL),
        out_shape=jax.ShapeDtypeStruct((T, D_MODEL), jnp.float32),
        compiler_params=pltpu.CompilerParams(
            dimension_semantics=("parallel",), vmem_limit_bytes=V7X_VMEM_LIMIT_BYTES),
        name="out_mlp",
    )(x2, oa, orr, gl, wa, wr, wo, n2w, w1, w2)


def kernel(x, positions, norm1_w, w_in, q_norm_w, k_norm_w, lambda_q1, lambda_k1, lambda_q2,
           lambda_k2, da_subln_w, w_da_branch, ret_gn_w, w_ret_branch, w_out, norm2_w, w_mlp_in,
           w_mlp_out):
    B, S, _ = x.shape
    T = B * S
    f32 = jnp.float32
    x2 = x.reshape(T, D_MODEL)

    q_scale = (DA_QK_DIM ** -0.5) * LOG2_E
    qkw = jnp.stack([jnp.tile(q_norm_w[0].astype(f32) * q_scale, DA_QK_W // DA_QK_DIM),
                     jnp.tile(k_norm_w[0].astype(f32), DA_QK_W // DA_QK_DIM)])
    blk = np.arange(V7X_MXU_WIDTH) // DA_QK_DIM
    ones_bd = jnp.asarray(blk[:, None] == blk[None, :], jnp.bfloat16)

    in_table = jnp.logical_and(jnp.min(positions) >= 0, jnp.max(positions) < ROT_FINE * ROT_COARSE)
    later_weights = tuple(w[0].astype(f32) for w in (w_out, w_mlp_in, w_mlp_out))
    splits = lambda steps: all(w.shape[0] % (steps * 2 * V7X_SUBLANES) == 0 for w in later_weights)
    in_proj_hosts = splits(T // min(PROJ_ROWS, T))
    in_proj_args = (x2, positions.astype(f32), norm1_w[0].astype(f32).reshape(1, D_MODEL),
                    _bf16(w_in[0]), qkw, ones_bd) + (later_weights if in_proj_hosts else ())
    dq, dk, dv, rq, rk, rv, rg, gl, *later_bf16 = lax.cond(
        in_table, functools.partial(_in_proj, table_trig=True),
        functools.partial(_in_proj, table_trig=False), *in_proj_args)

    lam = (jnp.exp(jnp.sum(lambda_q1[0].astype(f32) * lambda_k1[0].astype(f32)))
           - jnp.exp(jnp.sum(lambda_q2[0].astype(f32) * lambda_k2[0].astype(f32)))
           + LAMBDA_INIT).reshape(1, 1)
    subln = jnp.broadcast_to((da_subln_w[0].astype(f32) * (1.0 - LAMBDA_INIT))[:, None],
                             (DA_V_DIM, V7X_LANES))
    logit_bound = (DA_QK_DIM * jnp.max(jnp.abs(qkw[0])) * jnp.max(jnp.abs(qkw[1]))
                   * ATTN_BOUND_MARGIN)
    oa = lax.cond(logit_bound <= ATTN_BOUNDED_LOGIT,
                  functools.partial(_diff_attn_bounded, B=B, S=S),
                  functools.partial(_diff_attn, B=B, S=S),
                  dq, dk, dv, lam, subln)

    log_gamma = np.log1p(-np.exp2(-5.0 - np.arange(RET_HEADS, dtype=np.float64)))
    lg_rows = jnp.asarray(np.broadcast_to(log_gamma[:, None, None], (RET_HEADS, 1, V7X_LANES)), f32)
    if in_proj_hosts:
        orr, _ = _retention(rq, rk, rv, rg, lg_rows, B, S)
    elif splits(B * (S // min(RET_SEQ_BLOCK, S))):
        orr, later_bf16 = _retention(rq, rk, rv, rg, lg_rows, B, S, later_weights)
    else:
        orr, _ = _retention(rq, rk, rv, rg, lg_rows, B, S)
        later_bf16 = [_bf16(w) for w in later_weights]
    w_out_b, w_mlp_in_b, w_mlp_out_b = later_bf16

    w_ret = _bf16(ret_gn_w[0].astype(f32)[:, None] * w_ret_branch[0].astype(f32))
    out = _out_mlp(x2, oa, orr, gl, _bf16(w_da_branch[0]), w_ret, w_out_b,
                   norm2_w[0].astype(f32).reshape(1, D_MODEL), w_mlp_in_b, w_mlp_out_b)
    return out.reshape(B, S, D_MODEL)
```

```python
import functools
import math

import jax
import jax.numpy as jnp
import numpy as np
from jax import lax
from jax.experimental import pallas as pl
from jax.experimental.pallas import tpu as pltpu

D_MODEL = 1024
DA_HEADS = 4
DA_QK_DIM = 64
DA_V_DIM = 2 * DA_QK_DIM
DA_ROT_DIM = DA_QK_DIM // 4
ROPE_THETA = 500000.0
RET_HEADS = 4
RET_QK_DIM = 128
RET_V_DIM = 256
RET_THETA_BASE = 10000.0
D_FF = 4 * D_MODEL
NORM_EPS = 1e-6
SUBLN_EPS = 1e-5
GN_EPS = 1e-5
MASK_VALUE = -1e30
LAMBDA_INIT = 0.8 - 0.6 * math.exp(-0.3 * 0)

DA_QK_W = DA_HEADS * 2 * DA_QK_DIM
DA_V_W = DA_HEADS * DA_V_DIM
RET_QK_W = RET_HEADS * RET_QK_DIM
RET_V_W = RET_HEADS * RET_V_DIM
GATE_W = 2 * D_MODEL
SECTION_WIDTHS = (DA_QK_W, DA_QK_W, DA_V_W, RET_QK_W, RET_QK_W, RET_V_W, RET_V_W, GATE_W)
SECTION_STARTS = tuple(int(v) for v in np.cumsum((0,) + SECTION_WIDTHS[:-1]))
IN_W = sum(SECTION_WIDTHS)

V7X_LANES = 128
V7X_SUBLANES = 8
V7X_MXU_WIDTH = 256
V7X_VMEM_LIMIT_BYTES = 56 * 1024 * 1024

LOG2_E = math.log2(math.e)

PROJ_ROWS = 512
IN_PROJ_GATE_PIECES = 4
ROT_FINE = 64
ROT_COARSE = 128
ATTN_BLOCK = 512
ATTN_BOUNDED_LOGIT = 60.0
ATTN_PV_LAG = 4
ATTN_BOUND_MARGIN = 1.0 + 2.0 ** -6
RET_CHUNK = 256
RET_SEQ_BLOCK = 2048
MLP_ROWS = 512
MLP_FF_CHUNK = 1024


def _bf16(t):
    return t.astype(jnp.bfloat16)


def _dot(a, b):
    return jnp.dot(a, b, preferred_element_type=jnp.float32)


def _dot_nt(a, b):
    return lax.dot_general(a, b, (((1,), (1,)), ((), ())), preferred_element_type=jnp.float32)


def _dot_tn(a, b):
    return lax.dot_general(a, b, (((0,), (0,)), ((), ())), preferred_element_type=jnp.float32)


def _split_bf16(t):
    hi = _bf16(t)
    lo = _bf16(t - hi.astype(jnp.float32))
    return jnp.concatenate([hi, lo], axis=0)


def _in_proj_kernel(x_ref, pos_ref, n1w_ref, w_ref, qkw_ref, freq_ref, coarse_ref, fine_ref, eda_ref,
                    ert_ref, ones_ref, *refs, table_trig):
    n_side = (len(refs) - len(SECTION_WIDTHS)) // 2
    side_in, side_out = refs[:n_side], refs[n_side + len(SECTION_WIDTHS):]
    dq_ref, dk_ref, dv_ref, rq_ref, rk_ref, rv_ref, rg_ref, gl_ref = refs[n_side:n_side + len(SECTION_WIDTHS)]
    for src, dst in zip(side_in, side_out):
        dst[...] = _bf16(src[...])
    x = x_ref[...]
    h = x * lax.rsqrt(jnp.mean(x * x, axis=-1, keepdims=True) + NORM_EPS) * n1w_ref[...]
    h = _bf16(h)

    def section(idx):
        lo = SECTION_STARTS[idx]
        return _dot(h, w_ref[:, lo:lo + SECTION_WIDTHS[idx]])

    rows = x.shape[0]
    n_da = DA_ROT_DIM // 2
    n_freq = freq_ref.shape[0]
    pos = pos_ref[0]
    if table_trig:
        a = jnp.floor(pos * (1.0 / ROT_FINE))
        b = pos - ROT_FINE * a
        pick_a = jnp.where(lax.broadcasted_iota(jnp.int32, (ROT_COARSE, rows), 0).astype(jnp.float32) == a,
                           1.0, 0.0).astype(jnp.bfloat16)
        pick_b = jnp.where(lax.broadcasted_iota(jnp.int32, (ROT_FINE, rows), 0).astype(jnp.float32) == b,
                           1.0, 0.0).astype(jnp.bfloat16)
        cs_a = _dot(coarse_ref[...], jnp.concatenate([pick_a, pick_a], axis=0))
        cs_b = _dot(fine_ref[...], jnp.concatenate([pick_b, pick_b], axis=0))
        cos_t = cs_a[:n_freq] * cs_b[:n_freq] - cs_a[n_freq:] * cs_b[n_freq:]
        sin_t = cs_a[n_freq:] * cs_b[:n_freq] + cs_a[:n_freq] * cs_b[n_freq:]
    else:
        ang = jnp.concatenate([freq_ref[...]] * (rows // V7X_LANES), axis=1) * pos
        cos_t, sin_t = jnp.cos(ang), jnp.sin(ang)
    tab_da = _dot_tn(_split_bf16(jnp.concatenate([cos_t[:n_da], sin_t[:n_da]], axis=0)),
                     eda_ref[...])
    tab_rt = _dot_tn(_split_bf16(jnp.concatenate([cos_t[n_da:], sin_t[n_da:]], axis=0)),
                     ert_ref[...])
    lane1 = lax.broadcasted_iota(jnp.int32, (1, V7X_LANES), 1)
    unrotated = jnp.where(lane1 % DA_QK_DIM >= DA_ROT_DIM, 1.0, 0.0)

    def plain(idx, out_ref, lo_col, width):
        lo = SECTION_STARTS[idx] + lo_col
        out_ref[:, lo_col:lo_col + width] = _bf16(_dot(h, w_ref[:, lo:lo + width]))

    cos_rt = tab_rt[:, :V7X_LANES]
    sin_rt = tab_rt[:, V7X_LANES:]

    def retention_qk(idx, out_ref, scale):
        t = section(idx)
        for hh in range(RET_HEADS):
            th = t[:, hh * RET_QK_DIM:(hh + 1) * RET_QK_DIM]
            rot = th * cos_rt + pltpu.roll(th, RET_QK_DIM // 2, axis=1) * sin_rt
            if scale != 1.0:
                rot = rot * scale
            out_ref[:, hh * RET_QK_DIM:(hh + 1) * RET_QK_DIM] = _bf16(rot)

    reps = DA_QK_W // V7X_LANES
    cos_da = jnp.concatenate([tab_da[:, :V7X_LANES] + unrotated] * reps, axis=1)
    sin_da = jnp.concatenate([tab_da[:, V7X_LANES:]] * reps, axis=1)
    lane = lax.broadcasted_iota(jnp.int32, (rows, DA_QK_W), 1)
    first_half = (lane % DA_QK_DIM) < (DA_ROT_DIM // 2)
    ones_bd = ones_ref[...]

    def diff_attn_qk(idx, out_ref):
        t = section(idx)
        tt = _bf16(t * t)
        ms = jnp.concatenate(
            [_dot(tt[:, g:g + V7X_MXU_WIDTH], ones_bd) for g in range(0, DA_QK_W, V7X_MXU_WIDTH)],
            axis=1) * (1.0 / DA_QK_DIM)
        t = t * lax.rsqrt(ms + NORM_EPS) * qkw_ref[idx:idx + 1, :]
        partner = jnp.where(first_half,
                            pltpu.roll(t, DA_QK_W - DA_ROT_DIM // 2, axis=1),
                            pltpu.roll(t, DA_ROT_DIM // 2, axis=1))
        rot = t * cos_da + partner * sin_da
        out_ref[...] = _bf16(rot.T if idx == 0 else rot)

    retention_qk(3, rq_ref, 1.0)
    plain(2, dv_ref, 0, DA_V_W)
    retention_qk(4, rk_ref, RET_QK_DIM ** -0.5)
    plain(5, rv_ref, 0, RET_V_W // 2)
    diff_attn_qk(0, dq_ref)
    plain(5, rv_ref, RET_V_W // 2, RET_V_W // 2)
    diff_attn_qk(1, dk_ref)
    for piece in range(IN_PROJ_GATE_PIECES):
        wg, wl = RET_V_W // IN_PROJ_GATE_PIECES, GATE_W // IN_PROJ_GATE_PIECES
        lo = SECTION_STARTS[6] + piece * wg
        rg = _dot(h, w_ref[:, lo:lo + wg])
        rg_ref[:, piece * wg:(piece + 1) * wg] = _bf16(rg * jax.nn.sigmoid(rg))
        plain(7, gl_ref, piece * wl, wl)


def _rotary_constants():
    n_da, n_rt = DA_ROT_DIM // 2, RET_QK_DIM // 2
    inv_da = ROPE_THETA ** (-jnp.arange(0, DA_ROT_DIM, 2, dtype=jnp.float32) / DA_ROT_DIM)
    inv_rt = 1.0 / (RET_THETA_BASE ** jnp.linspace(0.0, 1.0, n_rt, dtype=jnp.float32))
    freqs = jnp.concatenate([inv_da, inv_rt])
    freq = jnp.broadcast_to(freqs[:, None], (n_da + n_rt, V7X_LANES))

    def trig_table(multiples):
        keep_11_bits = jnp.uint32(0xFFFFFFFF << (24 - 11) & 0xFFFFFFFF)
        bits = lax.bitcast_convert_type(freqs, jnp.uint32) & keep_11_bits
        f_hi = lax.bitcast_convert_type(bits, jnp.float32)
        big = f_hi[:, None] * multiples[None, :]
        small = (freqs - f_hi)[:, None] * multiples[None, :]
        cos = jnp.cos(big) * jnp.cos(small) - jnp.sin(big) * jnp.sin(small)
        sin = jnp.sin(big) * jnp.cos(small) + jnp.cos(big) * jnp.sin(small)
        t = jnp.concatenate([cos, sin], axis=0)
        hi = _bf16(t)
        return jnp.concatenate([hi, _bf16(t - hi.astype(jnp.float32))], axis=1)

    coarse = trig_table(ROT_FINE * jnp.arange(ROT_COARSE, dtype=jnp.float32))
    fine = trig_table(jnp.arange(ROT_FINE, dtype=jnp.float32))

    def selection(n, period):
        e = np.zeros((4 * n, 2 * V7X_LANES), np.float32)
        for f in range(n):
            for base in range(0, V7X_LANES, period):
                for part in (0, 2 * n):
                    e[part + f, base + f] = e[part + f, base + n + f] = 1.0
                    e[part + n + f, V7X_LANES + base + f] = -1.0
                    e[part + n + f, V7X_LANES + base + n + f] = 1.0
        return jnp.asarray(e, jnp.bfloat16)

    return freq, coarse, fine, selection(n_da, DA_QK_DIM), selection(n_rt, RET_QK_DIM)


def _in_proj(x2, pos, n1w, w_in, qkw, ones_bd, *side_weights, table_trig):
    T = x2.shape[0]
    rows = min(PROJ_ROWS, T)
    assert T % rows == 0 and rows % V7X_LANES == 0
    side_specs = [pl.BlockSpec((w.shape[0] // (T // rows), w.shape[1]), lambda i: (i, 0))
                  for w in side_weights]
    freq, coarse, fine, e_da, e_rt = _rotary_constants()
    pos = pos.reshape(T // rows, 1, rows)
    row_spec = lambda width: pl.BlockSpec((rows, width), lambda i: (i, 0))
    const_spec = lambda shape: pl.BlockSpec(shape, lambda i: (0, 0))
    out_widths = SECTION_WIDTHS
    return pl.pallas_call(
        functools.partial(_in_proj_kernel, table_trig=table_trig),
        grid=(T // rows,),
        in_specs=[
            row_spec(D_MODEL),
            pl.BlockSpec((1, 1, rows), lambda i: (i, 0, 0)),
            const_spec((1, D_MODEL)),
            pl.BlockSpec((D_MODEL, IN_W), lambda i: (0, 0), pipeline_mode=pl.Buffered(1)),
            const_spec((2, DA_QK_W)),
            const_spec(freq.shape), const_spec(coarse.shape), const_spec(fine.shape),
            const_spec(e_da.shape), const_spec(e_rt.shape),
            const_spec((V7X_MXU_WIDTH, V7X_MXU_WIDTH)),
        ] + side_specs,
        out_specs=[pl.BlockSpec((DA_QK_W, rows), lambda i: (0, i))] + [row_spec(w) for w in out_widths[1:]]
        + side_specs,
        out_shape=[jax.ShapeDtypeStruct((DA_QK_W, T), jnp.bfloat16)]
        + [jax.ShapeDtypeStruct((T, w), jnp.bfloat16) for w in out_widths[1:]]
        + [jax.ShapeDtypeStruct(w.shape, jnp.bfloat16) for w in side_weights],
        compiler_params=pltpu.CompilerParams(
            dimension_semantics=("parallel",), vmem_limit_bytes=V7X_VMEM_LIMIT_BYTES),
        name="in_proj",
    )(x2, pos, n1w, w_in, qkw, freq, coarse, fine, e_da, e_rt, ones_bd, *side_weights)


def _diff_attn_kernel(q_ref, k_ref, v_ref, lam_ref, sw_ref, o_ref,
                      qst_ref, s_ref, p_ref, acc_ref, *, qb, kb):
    qi = pl.program_id(2)
    nq = 2 * qb
    n_strip = nq // V7X_LANES
    f32 = jnp.float32

    qt = q_ref[...]
    dim = lax.broadcasted_iota(jnp.int32, qt.shape, 0)
    qst_ref[:, :qb] = jnp.where(dim < DA_QK_DIM, qt, jnp.zeros_like(qt))
    qst_ref[:, qb:] = jnp.where(dim >= DA_QK_DIM, qt, jnp.zeros_like(qt))
    acc_ref[...] = jnp.zeros(acc_ref.shape, f32)
    p_ref[1] = jnp.zeros(p_ref.shape[1:], p_ref.dtype)

    def stage_a(t, slot):
        start = pl.multiple_of(t * kb, kb)
        s_ref[slot] = _dot(k_ref[pl.ds(start, kb), :], qst_ref[...])

    def stage_b(slot, m, l, key_offset):
        m_out, l_out, a_out = [], [], []
        for c in range(n_strip):
            cols = slice(c * V7X_LANES, (c + 1) * V7X_LANES)
            st = s_ref[slot, :, cols]
            if key_offset is not None:
                krow = lax.broadcasted_iota(jnp.int32, st.shape, 0) + key_offset
                qcol = lax.broadcasted_iota(jnp.int32, st.shape, 1) + (c * V7X_LANES) % qb
                st = jnp.where(krow <= qcol, st, MASK_VALUE)
            m_new = jnp.maximum(m[c], jnp.max(st, axis=0, keepdims=True))
            alpha = jnp.exp2(m[c] - m_new)
            p = jnp.exp2(st - m_new)
            l_out.append(alpha * l[c] + jnp.sum(p, axis=0, keepdims=True))
            p_ref[slot, :, cols] = _bf16(p)
            m_out.append(m_new)
            a_out.append(alpha)
        return tuple(m_out), tuple(l_out), tuple(a_out)

    def stage_c(t, slot, alpha):
        start = pl.multiple_of(jnp.maximum(t, 0) * kb, kb)
        vt = v_ref[pl.ds(start, kb), :]
        for c in range(0, n_strip, 2):
            cols = slice(c * V7X_LANES, (c + 2) * V7X_LANES)
            a = jnp.concatenate([alpha[c], alpha[c + 1]], axis=1)
            acc_ref[:, cols] = a * acc_ref[:, cols] + _dot_tn(vt, p_ref[slot, :, cols])

    def body(u, carry):
        m, l, alpha = carry
        t = 2 * u
        stage_c(t - 1, 1, alpha)
        stage_a(t + 1, 1)
        m, l, alpha = stage_b(0, m, l, None)
        stage_c(t, 0, alpha)
        stage_a(t + 2, 0)
        m, l, alpha = stage_b(1, m, l, None)
        return m, l, alpha

    row = lambda v: tuple(jnp.full((1, V7X_LANES), v, f32) for _ in range(n_strip))
    stage_a(0, 0)
    m, l, alpha = lax.fori_loop(0, qi, body, (row(MASK_VALUE), row(0.0), row(1.0)))
    t = 2 * qi
    stage_c(t - 1, 1, alpha)
    stage_a(t + 1, 1)
    m, l, alpha = stage_b(0, m, l, 0)
    stage_c(t, 0, alpha)
    m, l, alpha = stage_b(1, m, l, kb)
    stage_c(t + 1, 1, alpha)

    o = acc_ref[...] / jnp.concatenate(l, axis=1)
    o = o[:, :qb] - lam_ref[...] * o[:, qb:]
    o = o * lax.rsqrt(jnp.mean(o * o, axis=0, keepdims=True) + SUBLN_EPS)
    o_ref[...] = _bf16(o * jnp.concatenate([sw_ref[...]] * (qb // V7X_LANES), axis=1))


def _diff_attn(dq, dk, dv, lam, subln_w, B, S):
    blk = min(ATTN_BLOCK, S)
    kb = blk // 2
    assert S % blk == 0 and kb % V7X_LANES == 0
    nq = S // blk
    kernel = functools.partial(_diff_attn_kernel, qb=blk, kb=kb)
    return pl.pallas_call(
        kernel,
        grid=(B, DA_HEADS, nq),
        scratch_shapes=[
            pltpu.VMEM((DA_V_DIM, 2 * blk), jnp.bfloat16),
            pltpu.VMEM((2, kb, 2 * blk), jnp.float32),
            pltpu.VMEM((2, kb, 2 * blk), jnp.bfloat16),
            pltpu.VMEM((DA_V_DIM, 2 * blk), jnp.float32),
        ],
        in_specs=[
            pl.BlockSpec((2 * DA_QK_DIM, blk), lambda b, h, i: (h, b * nq + i)),
            pl.BlockSpec((S, V7X_LANES), lambda b, h, i: (b, h)),
            pl.BlockSpec((S, V7X_LANES), lambda b, h, i: (b, h)),
            pl.BlockSpec((1, 1), lambda b, h, i: (0, 0)),
            pl.BlockSpec((DA_V_DIM, V7X_LANES), lambda b, h, i: (0, 0)),
        ],
        out_specs=pl.BlockSpec((DA_V_DIM, blk), lambda b, h, i: (h, b * nq + i)),
        out_shape=jax.ShapeDtypeStruct((DA_V_W, B * S), jnp.bfloat16),
        compiler_params=pltpu.CompilerParams(
            dimension_semantics=("parallel", "parallel", "arbitrary"),
            vmem_limit_bytes=V7X_VMEM_LIMIT_BYTES),
        name="diff_attn",
    )(dq, dk, dv, lam, subln_w)


def _diff_attn_bounded_kernel(q_ref, k_ref, v_ref, lam_ref, sw_ref, o_ref, qst_ref, p_ref, *, seq):
    tile = V7X_MXU_WIDTH
    n_tok_tile = seq // tile

    qt = q_ref[...]
    dim = lax.broadcasted_iota(jnp.int32, qt.shape, 0)
    qst_ref[0] = jnp.where(dim < DA_QK_DIM, qt, jnp.zeros_like(qt))
    qst_ref[1] = jnp.where(dim >= DA_QK_DIM, qt, jnp.zeros_like(qt))
    sw = jnp.concatenate([sw_ref[...]] * (tile // V7X_LANES), axis=1)

    items = [(j, m) for j in reversed(range(n_tok_tile)) for m in range(2)]
    slots = p_ref.shape[0]

    def probs(n):
        j, m = items[n]
        keys = (j + 1) * tile
        s = _dot(k_ref[:keys, :], qst_ref[m, :, j * tile:(j + 1) * tile])
        krow = lax.broadcasted_iota(jnp.int32, (tile, tile), 0)
        qcol = lax.broadcasted_iota(jnp.int32, (tile, tile), 1)
        p_diag = jnp.exp2(jnp.where(krow <= qcol, s[keys - tile:], MASK_VALUE))
        p_ref[n % slots, keys - tile:keys] = _bf16(p_diag)
        sub = V7X_SUBLANES
        part = jnp.sum(p_diag.reshape(tile // sub, sub, tile), axis=0)
        if keys > tile:
            p_full = jnp.exp2(s[:keys - tile])
            p_ref[n % slots, :keys - tile] = _bf16(p_full)
            part = part + jnp.sum(p_full.reshape((keys - tile) // sub, sub, tile), axis=0)
        return jnp.sum(part, axis=0, keepdims=True)

    def weighted_values(n):
        j, _ = items[n]
        keys = (j + 1) * tile
        return _dot_tn(v_ref[:keys, :], p_ref[n % slots, :keys])

    lsum, out = {}, {}
    for n in range(len(items) + ATTN_PV_LAG):
        if n < len(items):
            lsum[n] = probs(n)
        done = n - ATTN_PV_LAG
        if done >= 0:
            out[done] = weighted_values(done) * (1.0 / lsum.pop(done))
            j, m = items[done]
            if m == 1:
                o = out.pop(done - 1) - lam_ref[...] * out.pop(done)
                o = o * lax.rsqrt(jnp.mean(o * o, axis=0, keepdims=True) + SUBLN_EPS)
                o_ref[:, j * tile:(j + 1) * tile] = _bf16(o * sw)


def _diff_attn_bounded(dq, dk, dv, lam, subln_w, B, S):
    assert S % V7X_MXU_WIDTH == 0
    kernel = functools.partial(_diff_attn_bounded_kernel, seq=S)
    return pl.pallas_call(
        kernel,
        grid=(B, DA_HEADS),
        scratch_shapes=[
            pltpu.VMEM((2, DA_V_DIM, S), jnp.bfloat16),
            pltpu.VMEM((2 * ATTN_PV_LAG + 2, S, V7X_MXU_WIDTH), jnp.bfloat16),
        ],
        in_specs=[
            pl.BlockSpec((2 * DA_QK_DIM, S), lambda b, h: (h, b)),
            pl.BlockSpec((S, V7X_LANES), lambda b, h: (b, h)),
            pl.BlockSpec((S, V7X_LANES), lambda b, h: (b, h)),
            pl.BlockSpec((1, 1), lambda b, h: (0, 0)),
            pl.BlockSpec((DA_V_DIM, V7X_LANES), lambda b, h: (0, 0)),
        ],
        out_specs=pl.BlockSpec((DA_V_DIM, S), lambda b, h: (h, b)),
        out_shape=jax.ShapeDtypeStruct((DA_V_W, B * S), jnp.bfloat16),
        compiler_params=pltpu.CompilerParams(
            dimension_semantics=("parallel", "parallel"),
            vmem_limit_bytes=V7X_VMEM_LIMIT_BYTES),
        name="diff_attn_bounded",
    )(dq, dk, dv, lam, subln_w)


def _retention_kernel(q_ref, k_ref, v_ref, g_ref, lg_ref, *refs, chunk, n_chunks, n_side):
    side_in, o_ref, side_out = refs[:n_side], refs[n_side], refs[n_side + 1:2 * n_side + 1]
    state_ref, decay_ref, xi_ref, zeta_ref = refs[2 * n_side + 1:]
    f32 = jnp.float32
    for src, dst in zip(side_in, side_out):
        dst[...] = _bf16(src[...])

    @pl.when(pl.program_id(1) == 0)
    def _():
        state_ref[...] = jnp.zeros(state_ref.shape, f32)
        row = lax.broadcasted_iota(jnp.int32, (chunk, chunk), 0)
        col = lax.broadcasted_iota(jnp.int32, (chunk, chunk), 1)
        rel = (row - col).astype(f32)
        idx = lax.broadcasted_iota(jnp.int32, (chunk, V7X_LANES), 0).astype(f32)
        for h in range(RET_HEADS):
            lg = lg_ref[h, :, 0:1]
            decay_ref[h] = jnp.where(rel >= 0, jnp.exp(lg * jnp.maximum(rel, 0.0)), 0.0)
            xi_ref[h] = jnp.exp(lg * (idx + 1.0))
            zeta_ref[h] = jnp.exp(lg * (chunk - 1.0 - idx))

    def head_chunk(c, h):
        rows = pl.ds(c * chunk, chunk)
        qk_cols = slice(h * RET_QK_DIM, (h + 1) * RET_QK_DIM)
        v_cols = slice(h * RET_V_DIM, (h + 1) * RET_V_DIM)
        lg = lg_ref[h, :, 0:1]
        qc = q_ref[rows, qk_cols]
        kc = k_ref[rows, qk_cols]
        vc = v_ref[rows, v_cols]
        scores = _bf16(_dot_nt(qc, kc) * decay_ref[h])
        xi = jnp.concatenate([xi_ref[h]] * (RET_V_DIM // V7X_LANES), axis=1)
        o = _dot(scores, vc) + _dot(qc, _bf16(state_ref[h])) * xi
        kz = _bf16(kc.astype(f32) * zeta_ref[h])
        state_ref[h] = jnp.exp(lg * float(chunk)) * state_ref[h] + _dot_tn(kz, vc)
        d = o - jnp.mean(o, axis=-1, keepdims=True)
        var = jnp.mean(d * d, axis=-1, keepdims=True)
        on = d * lax.rsqrt(var + GN_EPS)
        o_ref[rows, v_cols] = _bf16(g_ref[rows, v_cols].astype(f32) * on)

    for c in range(n_chunks):
        for h in range(RET_HEADS):
            head_chunk(c, h)


def _retention(rq, rk, rv, rg, lg_rows, B, S, side_weights=()):
    chunk = min(RET_CHUNK, S)
    sblk = min(RET_SEQ_BLOCK, S)
    assert S % sblk == 0 and sblk % chunk == 0
    nsb = S // sblk
    n_steps = B * nsb
    bf16_rows = 2 * V7X_SUBLANES
    assert all(w.shape[0] % (n_steps * bf16_rows) == 0 for w in side_weights)
    side_specs = [pl.BlockSpec((w.shape[0] // n_steps, w.shape[1]), lambda b, s: (b * nsb + s, 0))
                  for w in side_weights]
    kernel = functools.partial(_retention_kernel, chunk=chunk, n_chunks=sblk // chunk,
                               n_side=len(side_weights))
    qk_spec = pl.BlockSpec((sblk, RET_QK_W), lambda b, s: (b * nsb + s, 0))
    v_spec = pl.BlockSpec((sblk, RET_V_W), lambda b, s: (b * nsb + s, 0))
    outs = pl.pallas_call(
        kernel,
        grid=(B, nsb),
        scratch_shapes=[
            pltpu.VMEM((RET_HEADS, RET_QK_DIM, RET_V_DIM), jnp.float32),
            pltpu.VMEM((RET_HEADS, chunk, chunk), jnp.float32),
            pltpu.VMEM((RET_HEADS, chunk, V7X_LANES), jnp.float32),
            pltpu.VMEM((RET_HEADS, chunk, V7X_LANES), jnp.float32),
        ],
        in_specs=[
            qk_spec, qk_spec, v_spec, v_spec,
            pl.BlockSpec((RET_HEADS, 1, V7X_LANES), lambda b, s: (0, 0, 0)),
        ] + side_specs,
        out_specs=[v_spec] + side_specs,
        out_shape=[jax.ShapeDtypeStruct((B * S, RET_V_W), jnp.bfloat16)]
        + [jax.ShapeDtypeStruct(w.shape, jnp.bfloat16) for w in side_weights],
        compiler_params=pltpu.CompilerParams(
            dimension_semantics=("parallel", "arbitrary"),
            vmem_limit_bytes=V7X_VMEM_LIMIT_BYTES),
        name="retention",
    )(rq, rk, rv, rg, lg_rows, *side_weights)
    return outs[0], tuple(outs[1:])


def _out_mlp_kernel(x_ref, oa_ref, or_ref, gl_ref, wa_ref, wr_ref, wo_ref, n2w_ref, w1_ref, w2_ref,
                    o_ref):
    ya = _dot_tn(oa_ref[...], wa_ref[...])
    yr = _dot(or_ref[...], wr_ref[...])
    ga = jax.nn.sigmoid(gl_ref[:, :D_MODEL].astype(jnp.float32))
    gr = jax.nn.sigmoid(gl_ref[:, D_MODEL:].astype(jnp.float32))
    merged = _bf16(ga * ya + gr * yr)
    x1 = x_ref[...] + _dot(merged, wo_ref[...])
    h2 = x1 * lax.rsqrt(jnp.mean(x1 * x1, axis=-1, keepdims=True) + NORM_EPS) * n2w_ref[...]
    h2 = _bf16(h2)
    acc = x1
    for c in range(D_FF // MLP_FF_CHUNK):
        lo = c * MLP_FF_CHUNK
        hid = jnp.maximum(_dot(h2, w1_ref[:, lo:lo + MLP_FF_CHUNK]), 0.0)
        acc = acc + _dot(_bf16(hid * hid), w2_ref[lo:lo + MLP_FF_CHUNK, :])
    o_ref[...] = acc


def _out_mlp(x2, oa, orr, gl, wa, wr, wo, n2w, w1, w2):
    T = x2.shape[0]
    rows = min(MLP_ROWS, T)
    assert T % rows == 0
    row_spec = lambda width: pl.BlockSpec((rows, width), lambda i: (i, 0))
    weight_spec = lambda shape: pl.BlockSpec(shape, lambda i: (0, 0), pipeline_mode=pl.Buffered(1))
    return pl.pallas_call(
        _out_mlp_kernel,
        grid=(T // rows,),
        in_specs=[
            row_spec(D_MODEL), pl.BlockSpec((DA_V_W, rows), lambda i: (0, i)),
            row_spec(RET_V_W), row_spec(GATE_W),
            weight_spec((DA_V_W, D_MODEL)), weight_spec((RET_V_W, D_MODEL)),
            weight_spec((D_MODEL, D_MODEL)),
            pl.BlockSpec((1, D_MODEL), lambda i: (0, 0)),
            weight_spec((D_MODEL, D_FF)), weight_spec((D_FF, D_MODEL)),
        ],
        out_specs=row_spec(D_MODEL),
        out_shape=jax.ShapeDtypeStruct((T, D_MODEL), jnp.float32),
        compiler_params=pltpu.CompilerParams(
            dimension_semantics=("parallel",), vmem_limit_bytes=V7X_VMEM_LIMIT_BYTES),
        name="out_mlp",
    )(x2, oa, orr, gl, wa, wr, wo, n2w, w1, w2)


def kernel(x, positions, norm1_w, w_in, q_norm_w, k_norm_w, lambda_q1, lambda_k1, lambda_q2,
           lambda_k2, da_subln_w, w_da_branch, ret_gn_w, w_ret_branch, w_out, norm2_w, w_mlp_in,
           w_mlp_out):
    B, S, _ = x.shape
    T = B * S
    f32 = jnp.float32
    x2 = x.reshape(T, D_MODEL)

    q_scale = (DA_QK_DIM ** -0.5) * LOG2_E
    qkw = jnp.stack([jnp.tile(q_norm_w[0].astype(f32) * q_scale, DA_QK_W // DA_QK_DIM),
                     jnp.tile(k_norm_w[0].astype(f32), DA_QK_W // DA_QK_DIM)])
    blk = np.arange(V7X_MXU_WIDTH) // DA_QK_DIM
    ones_bd = jnp.asarray(blk[:, None] == blk[None, :], jnp.bfloat16)

    in_table = jnp.logical_and(jnp.min(positions) >= 0, jnp.max(positions) < ROT_FINE * ROT_COARSE)
    later_weights = tuple(w[0].astype(f32) for w in (w_out, w_mlp_in, w_mlp_out))
    splits = lambda steps: all(w.shape[0] % (steps * 2 * V7X_SUBLANES) == 0 for w in later_weights)
    in_proj_hosts = splits(T // min(PROJ_ROWS, T))
    in_proj_args = (x2, positions.astype(f32), norm1_w[0].astype(f32).reshape(1, D_MODEL),
                    _bf16(w_in[0]), qkw, ones_bd) + (later_weights if in_proj_hosts else ())
    dq, dk, dv, rq, rk, rv, rg, gl, *later_bf16 = lax.cond(
        in_table, functools.partial(_in_proj, table_trig=True),
        functools.partial(_in_proj, table_trig=False), *in_proj_args)

    lam = (jnp.exp(jnp.sum(lambda_q1[0].astype(f32) * lambda_k1[0].astype(f32)))
           - jnp.exp(jnp.sum(lambda_q2[0].astype(f32) * lambda_k2[0].astype(f32)))
           + LAMBDA_INIT).reshape(1, 1)
    subln = jnp.broadcast_to((da_subln_w[0].astype(f32) * (1.0 - LAMBDA_INIT))[:, None],
                             (DA_V_DIM, V7X_LANES))
    logit_bound = (DA_QK_DIM * jnp.max(jnp.abs(qkw[0])) * jnp.max(jnp.abs(qkw[1]))
                   * ATTN_BOUND_MARGIN)
    oa = lax.cond(logit_bound <= ATTN_BOUNDED_LOGIT,
                  functools.partial(_diff_attn_bounded, B=B, S=S),
                  functools.partial(_diff_attn, B=B, S=S),
                  dq, dk, dv, lam, subln)

    log_gamma = np.log1p(-np.exp2(-5.0 - np.arange(RET_HEADS, dtype=np.float64)))
    lg_rows = jnp.asarray(np.broadcast_to(log_gamma[:, None, None], (RET_HEADS, 1, V7X_LANES)), f32)
    if in_proj_hosts:
        orr, _ = _retention(rq, rk, rv, rg, lg_rows, B, S)
    elif splits(B * (S // min(RET_SEQ_BLOCK, S))):
        orr, later_bf16 = _retention(rq, rk, rv, rg, lg_rows, B, S, later_weights)
    else:
        orr, _ = _retention(rq, rk, rv, rg, lg_rows, B, S)
        later_bf16 = [_bf16(w) for w in later_weights]
    w_out_b, w_mlp_in_b, w_mlp_out_b = later_bf16

    w_ret = _bf16(ret_gn_w[0].astype(f32)[:, None] * w_ret_branch[0].astype(f32))
    out = _out_mlp(x2, oa, orr, gl, _bf16(w_da_branch[0]), w_ret, w_out_b,
                   norm2_w[0].astype(f32).reshape(1, D_MODEL), w_mlp_in_b, w_mlp_out_b)
    return out.reshape(B, S, D_MODEL)
```
